```python
import math
import jax
import jax.numpy as jnp
from jax import lax
import numpy as np

D_MODEL = 4096
BATCH = 1
SEQ = 8192
DEPTH = 1
DEC_BATCH = 32
DEC_SEQ = 16
PAST_LEN = 2048

CHUNK = 64
HEAD_DIM = 128
A_HEADS = 16
A_KV_HEADS = 4
IDX_HEADS = 32
IDX_DIM = 64
TOPK_MAX = 256
B_HEADS = 16
N_BUCKETS = 32
MAX_DISTANCE = 128
N_GROUPS = 4
EXPERTS_PER_GROUP = 8
N_EXPERTS = N_GROUPS * EXPERTS_PER_GROUP
TOP_K_IN_GROUP = 2
D_EXPERT = 1024
Q_BLOCK = 128
ROW_BLOCK = 128
EPS = 1e-6
FORGET_BIAS_INIT = 3.0

A_WIDTH = A_HEADS * HEAD_DIM
A_KV_WIDTH = A_KV_HEADS * HEAD_DIM
B_WIDTH = B_HEADS * HEAD_DIM
MIX_WIDTH = A_WIDTH + B_WIDTH
IN_SPLITS = (A_WIDTH, A_KV_WIDTH, A_KV_WIDTH, IDX_HEADS * IDX_DIM, IDX_DIM, IDX_HEADS,
             B_WIDTH, B_WIDTH, B_WIDTH, B_HEADS)
IN_WIDTH = A_WIDTH + 2 * A_KV_WIDTH + IDX_HEADS * IDX_DIM + IDX_DIM + IDX_HEADS + 3 * B_WIDTH + B_HEADS

kernel_name = "hymba_dsa_fox_hmoe_stream_step"


def rmsnorm(x, g):
    x32 = x.astype(jnp.float32)
    y = x32 * lax.rsqrt(jnp.mean(x32 * x32, axis=-1, keepdims=True) + EPS) * g.astype(jnp.float32)
    return y.astype(x.dtype)


def split_cols(a):
    parts, start = [], 0
    for w in IN_SPLITS:
        parts.append(a[..., start:start + w])
        start += w
    return parts


def to_blocks(a, qb):
    b, t = a.shape[:2]
    return jnp.moveaxis(a.reshape((b, t // qb, qb) + a.shape[2:]), 1, 0)


def from_blocks(a):
    a = jnp.moveaxis(a, 0, 1)
    return a.reshape(a.shape[0], -1, a.shape[-1])


def t5_bucket(rel):
    half = N_BUCKETS // 2
    max_exact = half // 2
    bucket = jnp.where(rel > 0, half, 0).astype(jnp.int32)
    n = jnp.abs(rel)
    large = max_exact + (jnp.log(jnp.maximum(n, 1).astype(jnp.float32) / max_exact)
                         / math.log(MAX_DISTANCE / max_exact) * (half - max_exact)).astype(jnp.int32)
    large = jnp.minimum(large, half - 1)
    return bucket + jnp.where(n < max_exact, n, large).astype(jnp.int32)


def dsa_attention(q, k, v, qi, ki, wi, q_pos, rel_bias):
    b, t = q.shape[:2]
    l = k.shape[1]
    n_sel = min(TOPK_MAX, l // 4)
    qb = min(Q_BLOCK, t)
    grp = A_HEADS // A_KV_HEADS
    k_chunk = jnp.arange(l, dtype=jnp.int32) // CHUNK
    gather = jax.vmap(lambda a, i: a[i])

    def block(args):
        q_b, qi_b, wi_b, pos_b = args
        q_chunk = pos_b // CHUNK
        s = jnp.einsum('bqhd,bsd->bqhs', qi_b, ki, preferred_element_type=jnp.float32) * (IDX_DIM ** -0.5)
        score = jnp.einsum('bqh,bqhs->bqs', wi_b.astype(jnp.float32), jax.nn.relu(s))
        allowed = k_chunk[None, :] <= q_chunk[:, None]
        score = jnp.where(allowed[None], score, -jnp.inf)
        _, sel = lax.top_k(score, n_sel)
        k_g = gather(k, sel)
        v_g = gather(v, sel)
        valid = (sel // CHUNK) <= q_chunk[None, :, None]
        bias = rel_bias[t5_bucket(sel - pos_b[None, :, None])].astype(jnp.float32)
        bias = bias.reshape(b, qb, n_sel, A_KV_HEADS, grp).transpose(0, 1, 3, 4, 2)
        qg = q_b.reshape(b, qb, A_KV_HEADS, grp, HEAD_DIM)
        logits = jnp.einsum('bqkgd,bqnkd->bqkgn', qg, k_g,
                            preferred_element_type=jnp.float32) * (HEAD_DIM ** -0.5) + bias
        logits = jnp.where(valid[:, :, None, None, :], logits, -jnp.inf)
        p = jax.nn.softmax(logits, axis=-1).astype(v.dtype)
        o = jnp.einsum('bqkgn,bqnkd->bqkgd', p, v_g)
        return o.reshape(b, qb, A_WIDTH)

    out = lax.map(block, (to_blocks(q, qb), to_blocks(qi, qb), to_blocks(wi, qb),
                          q_pos.reshape(t // qb, qb)))
    return from_blocks(out)


def fox_attention(q, k, v, fq, fk, q_pos):
    b, t = q.shape[:2]
    l = k.shape[1]
    qb = min(Q_BLOCK, t)
    k_pos = jnp.arange(l, dtype=jnp.int32)
    fk_t = jnp.swapaxes(fk, 1, 2)[:, :, None, :]

    def block(args):
        q_b, fq_b, pos_b = args
        logits = jnp.einsum('bqhd,bshd->bhqs', q_b, k,
                            preferred_element_type=jnp.float32) * (HEAD_DIM ** -0.5)
        logits = logits + jnp.swapaxes(fq_b, 1, 2)[..., None] - fk_t
        causal = k_pos[None, :] <= pos_b[:, None]
        logits = jnp.where(causal[None, None], logits, -jnp.inf)
        p = jax.nn.softmax(logits, axis=-1).astype(v.dtype)
        o = jnp.einsum('bhqs,bshd->bqhd', p, v)
        return o.reshape(b, qb, B_WIDTH)

    out = lax.map(block, (to_blocks(q, qb), to_blocks(fq, qb), q_pos.reshape(t // qb, qb)))
    return from_blocks(out)


def token_mixers(h, past, w_in, b_forget, w_out, rel_bias):
    b, t, _ = h.shape
    p_len = 0 if past is None else past[0].shape[1]
    qa, ka, va, qi, ki, wi, qb_, kb, vb, fl = split_cols(h @ w_in)
    qa = qa.reshape(b, t, A_HEADS, HEAD_DIM)
    ka = ka.reshape(b, t, A_KV_HEADS, HEAD_DIM)
    va = va.reshape(b, t, A_KV_HEADS, HEAD_DIM)
    qi = qi.reshape(b, t, IDX_HEADS, IDX_DIM)
    wi = wi * (IDX_HEADS ** -0.5)
    qb_ = qb_.reshape(b, t, B_HEADS, HEAD_DIM)
    kb = kb.reshape(b, t, B_HEADS, HEAD_DIM)
    vb = vb.reshape(b, t, B_HEADS, HEAD_DIM)
    logf = jax.nn.log_sigmoid((fl + b_forget).astype(jnp.float32))
    new_rows = (ka, va, ki, kb, vb, logf)
    if past is None:
        full = new_rows
    else:
        full = tuple(jnp.concatenate([pc, nr], axis=1) for pc, nr in zip(past, new_rows))
    ka_all, va_all, ki_all, kb_all, vb_all, logf_all = full
    q_pos = p_len + jnp.arange(t, dtype=jnp.int32)
    f_cum = jnp.cumsum(logf_all.astype(jnp.float32), axis=1)
    out_a = dsa_attention(qa, ka_all, va_all, qi, ki_all, wi, q_pos, rel_bias)
    out_b = fox_attention(qb_, kb_all, vb_all, f_cum[:, p_len:], f_cum, q_pos)
    y = jnp.concatenate([out_a, out_b], axis=-1) @ w_out
    return y, new_rows


def routed_experts(hf, expert, gate, w1, w3, w2):
    n, d = hf.shape
    a = n * TOP_K_IN_GROUP
    flat_e = expert.reshape(a)
    flat_tok = jnp.arange(a, dtype=jnp.int32) // TOP_K_IN_GROUP
    order = jnp.argsort(flat_e)
    se = flat_e[order]
    tok_sorted = flat_tok[order]
    g_sorted = gate.reshape(a)[order]
    counts = jnp.bincount(flat_e, length=N_EXPERTS)
    padded = (counts + ROW_BLOCK - 1) // ROW_BLOCK * ROW_BLOCK
    seg_start = jnp.cumsum(counts) - counts
    pad_end = jnp.cumsum(padded)
    pad_start = pad_end - padded
    dest = pad_start[se] + jnp.arange(a, dtype=jnp.int32) - seg_start[se]
    n_blocks = -(-a // ROW_BLOCK) + N_EXPERTS
    rows = n_blocks * ROW_BLOCK
    row_tok = jnp.full((rows,), n, jnp.int32).at[dest].set(tok_sorted)
    block_e = jnp.minimum(jnp.searchsorted(pad_end, jnp.arange(n_blocks) * ROW_BLOCK, side='right'),
                          N_EXPERTS - 1)
    h_pad = jnp.concatenate([hf, jnp.zeros((1, d), hf.dtype)], axis=0)
    xb = h_pad[row_tok].reshape(n_blocks, ROW_BLOCK, d)

    def expert_block(args):
        xe, e = args
        return (jax.nn.silu(xe @ w1[e]) * (xe @ w3[e])) @ w2[e]

    yb = lax.map(expert_block, (xb, block_e)).reshape(rows, d)
    contrib = yb[dest] * g_sorted[:, None].astype(yb.dtype)
    return jnp.zeros((n, d), hf.dtype).at[tok_sorted].add(contrib)


def hier_moe(h, w_rg, b_rg, w_re, b_re, w1, w3, w2):
    b, t, d = h.shape
    hf = h.reshape(b * t, d)
    n = hf.shape[0]
    g_logits = (hf @ w_rg).astype(jnp.float32) + b_rg.astype(jnp.float32)
    g_prob = jax.nn.softmax(g_logits, axis=-1)
    g_sel = jnp.argmax(g_logits, axis=-1).astype(jnp.int32)
    e_logits = ((hf @ w_re).astype(jnp.float32) + b_re.astype(jnp.float32)).reshape(n, N_GROUPS, EXPERTS_PER_GROUP)
    e_in = jnp.take_along_axis(e_logits, g_sel[:, None, None], axis=1)[:, 0]
    top_v, top_i = lax.top_k(e_in, TOP_K_IN_GROUP)
    gate = jax.nn.softmax(top_v, axis=-1) * jnp.take_along_axis(g_prob, g_sel[:, None], axis=1)
    expert = g_sel[:, None] * EXPERTS_PER_GROUP + top_i.astype(jnp.int32)
    return routed_experts(hf, expert, gate, w1, w3, w2).reshape(b, t, d)


def run_trunk(x, c, past, weights):
    (w_in, b_forget, w_out, rel_bias, w_mod, b_mod, g_mix, g_ffn, w_rg, b_rg,
     w_re, b_re, w_gate, w_up, w_down, g_final) = weights
    rows_per_layer = []
    for l in range(DEPTH):
        mod = (jax.nn.silu(c) @ w_mod[l] + b_mod[l]).reshape(c.shape[0], 6, D_MODEL)
        sh1, sc1, gt1, sh2, sc2, gt2 = [mod[:, i][:, None, :] for i in range(6)]
        h = rmsnorm(x, g_mix[l]) * (1 + sc1) + sh1
        past_l = None if past is None else tuple(pc[l] for pc in past)
        y, rows = token_mixers(h, past_l, w_in[l], b_forget[l], w_out[l], rel_bias)
        x = x + gt1 * y
        h = rmsnorm(x, g_ffn[l]) * (1 + sc2) + sh2
        x = x + gt2 * hier_moe(h, w_rg[l], b_rg[l], w_re[l], b_re[l], w_gate[l], w_up[l], w_down[l])
        rows_per_layer.append(rows)
    new_state = tuple(jnp.stack([r[i] for r in rows_per_layer], axis=0) for i in range(6))
    return rmsnorm(x, g_final), new_state


def setup_inputs(seed: int = 0) -> dict:
    key = jax.random.key(seed)
    ks = jax.random.split(key, 26)

    def nrm(k, shape, s):
        return jax.random.normal(k, shape, jnp.float32) * s

    return {
        "x_prompt": nrm(ks[0], (BATCH, SEQ, D_MODEL), 1.0),
        "x_sample": nrm(ks[1], (DEC_BATCH, DEC_SEQ, D_MODEL), 1.0),
        "cache_a_k": nrm(ks[2], (DEPTH, DEC_BATCH, PAST_LEN, A_KV_HEADS, HEAD_DIM), 1.0),
        "cache_a_v": nrm(ks[3], (DEPTH, DEC_BATCH, PAST_LEN, A_KV_HEADS, HEAD_DIM), 1.0),
        "cache_idx_k": nrm(ks[4], (DEPTH, DEC_BATCH, PAST_LEN, IDX_DIM), 1.0),
        "cache_b_k": nrm(ks[5], (DEPTH, DEC_BATCH, PAST_LEN, B_HEADS, HEAD_DIM), 1.0),
        "cache_b_v": nrm(ks[6], (DEPTH, DEC_BATCH, PAST_LEN, B_HEADS, HEAD_DIM), 1.0),
        "cache_b_logf": jax.nn.log_sigmoid(FORGET_BIAS_INIT + nrm(ks[7], (DEPTH, DEC_BATCH, PAST_LEN, B_HEADS), 1.0)),
        "c_prompt": nrm(ks[8], (BATCH, D_MODEL), 1.0),
        "c_sample": nrm(ks[9], (DEC_BATCH, D_MODEL), 1.0),
        "w_in": nrm(ks[10], (DEPTH, D_MODEL, IN_WIDTH), D_MODEL ** -0.5),
        "b_forget": FORGET_BIAS_INIT + nrm(ks[11], (DEPTH, B_HEADS), 0.5),
        "w_out": nrm(ks[12], (DEPTH, MIX_WIDTH, D_MODEL), MIX_WIDTH ** -0.5),
        "rel_bias": nrm(ks[13], (N_BUCKETS, A_HEADS), 0.5),
        "w_mod": nrm(ks[14], (DEPTH, D_MODEL, 6 * D_MODEL), 0.5 * D_MODEL ** -0.5),
        "b_mod": nrm(ks[15], (DEPTH, 6 * D_MODEL), 0.02),
        "g_mix": 1.0 + nrm(ks[16], (DEPTH, D_MODEL), 0.02),
        "g_ffn": 1.0 + nrm(ks[17], (DEPTH, D_MODEL), 0.02),
        "w_router_group": nrm(ks[18], (DEPTH, D_MODEL, N_GROUPS), D_MODEL ** -0.5),
        "b_router_group": nrm(ks[19], (DEPTH, N_GROUPS), 0.01),
        "w_router_expert": nrm(ks[20], (DEPTH, D_MODEL, N_EXPERTS), D_MODEL ** -0.5),
        "b_router_expert": nrm(ks[21], (DEPTH, N_EXPERTS), 0.01),
        "w_gate": nrm(ks[22], (DEPTH, N_EXPERTS, D_MODEL, D_EXPERT), D_MODEL ** -0.5),
        "w_up": nrm(ks[23], (DEPTH, N_EXPERTS, D_MODEL, D_EXPERT), D_MODEL ** -0.5),
        "w_down": nrm(ks[24], (DEPTH, N_EXPERTS, D_EXPERT, D_MODEL), D_EXPERT ** -0.5),
        "g_final": 1.0 + nrm(ks[25], (D_MODEL,), 0.02),
    }


def reference(x_prompt, x_sample, cache_a_k, cache_a_v, cache_idx_k, cache_b_k, cache_b_v,
              cache_b_logf, c_prompt, c_sample, w_in, b_forget, w_out, rel_bias, w_mod, b_mod,
              g_mix, g_ffn, w_router_group, b_router_group, w_router_expert, b_router_expert,
              w_gate, w_up, w_down, g_final):
    weights = (w_in, b_forget, w_out, rel_bias, w_mod, b_mod, g_mix, g_ffn, w_router_group,
               b_router_group, w_router_expert, b_router_expert, w_gate, w_up, w_down, g_final)
    y_prompt, (pa_k, pa_v, pidx_k, pb_k, pb_v, pb_logf) = run_trunk(x_prompt, c_prompt, None, weights)
    past = (cache_a_k, cache_a_v, cache_idx_k, cache_b_k, cache_b_v, cache_b_logf)
    y_sample, (sa_k, sa_v, sidx_k, sb_k, sb_v, sb_logf) = run_trunk(x_sample, c_sample, past, weights)
    return (y_prompt, y_sample, pa_k, pa_v, pidx_k, pb_k, pb_v, pb_logf,
            sa_k, sa_v, sidx_k, sb_k, sb_v, sb_logf)
```

```python
import functools
import math

import numpy as np
import jax
import jax.numpy as jnp
from jax import lax
from jax.experimental import pallas as pl
from jax.experimental.pallas import tpu as pltpu

CHUNK = 64
HEAD_DIM = 128
A_HEADS = 16
A_KV_HEADS = 4
IDX_HEADS = 32
IDX_DIM = 64
TOPK_MAX = 256
B_HEADS = 16
N_BUCKETS = 32
MAX_DISTANCE = 128
N_GROUPS = 4
EXPERTS_PER_GROUP = 8
N_EXPERTS = N_GROUPS * EXPERTS_PER_GROUP
TOP_K_IN_GROUP = 2
EPS = 1e-6

A_WIDTH = A_HEADS * HEAD_DIM
A_KV_WIDTH = A_KV_HEADS * HEAD_DIM
B_WIDTH = B_HEADS * HEAD_DIM
IDX_WIDTH = IDX_HEADS * IDX_DIM
SMALL_WIDTH = 128

LANES = 128
V7X_VMEM_BYTES = 64 * 1024 * 1024
VMEM_LIMIT = 56 * 1024 * 1024

BF16 = jnp.bfloat16
F32 = jnp.float32
NEG_INF = float("-inf")
INT_MIN = -(2 ** 31)
KEY_NEG_INF = int(np.int32(np.float32(-np.inf).view(np.int32)) ^ np.int32(0x7FFFFFFF))


def _cparams(*sem):
    return pltpu.CompilerParams(dimension_semantics=sem, vmem_limit_bytes=VMEM_LIMIT)


def _matmul_kernel(*refs, n_a, silu_a, epilogue):
    a_refs = refs[:n_a]
    w_ref = refs[n_a]
    rest = refs[n_a + 1:]
    o_ref = rest[-1]
    acc = None
    k0 = 0
    for a_ref in a_refs:
        a = a_ref[...]
        if silu_a:
            a = a * jax.nn.sigmoid(a)
        kk = a.shape[1]
        part = jnp.dot(a.astype(BF16), w_ref[k0:k0 + kk, :].astype(BF16),
                       preferred_element_type=F32)
        acc = part if acc is None else acc + part
        k0 += kk
    if epilogue == "bias":
        acc = acc + rest[0][...]
    elif epilogue == "gated_residual":
        acc = rest[0][...] + rest[1][...] * acc
    o_ref[...] = acc.astype(o_ref.dtype)


def _matmul(a_list, w, *, col_off=0, n=None, tm, tn, silu_a=False, bias=None, res=None,
            gate=None, out_dtype=F32):
    m = a_list[0].shape[0]
    k = w.shape[0]
    n = w.shape[1] - col_off if n is None else n
    assert m % tm == 0 and n % tn == 0 and col_off % tn == 0
    assert sum(a.shape[1] for a in a_list) == k
    cb = col_off // tn
    in_specs = [pl.BlockSpec((tm, a.shape[1]), lambda j, i: (i, 0)) for a in a_list]
    in_specs.append(pl.BlockSpec((k, tn), lambda j, i: (0, j + cb)))
    args = list(a_list) + [w]
    if bias is not None:
        epilogue = "bias"
        in_specs.append(pl.BlockSpec((1, tn), lambda j, i: (0, j)))
        args.append(bias)
    elif res is not None:
        epilogue = "gated_residual"
        in_specs.append(pl.BlockSpec((tm, tn), lambda j, i: (i, j)))
        args.append(res)
        if gate.shape[0] == 1:
            in_specs.append(pl.BlockSpec((1, tn), lambda j, i: (0, j)))
        else:
            in_specs.append(pl.BlockSpec((tm, tn), lambda j, i: (i, j)))
        args.append(gate)
    else:
        epilogue = None
    return pl.pallas_call(
        functools.partial(_matmul_kernel, n_a=len(a_list), silu_a=silu_a, epilogue=epilogue),
        grid=(n // tn, m // tm),
        in_specs=in_specs,
        out_specs=pl.BlockSpec((tm, tn), lambda j, i: (i, j)),
        out_shape=jax.ShapeDtypeStruct((m, n), out_dtype),
        compiler_params=_cparams("arbitrary", "arbitrary"),
    )(*args)


def _rms(x, g):
    return x * lax.rsqrt(jnp.mean(x * x, axis=-1, keepdims=True) + EPS) * g


def _norm_mod_kernel(x_ref, g_ref, sc_ref, sh_ref, o_ref):
    y = _rms(x_ref[...], g_ref[...])
    o_ref[...] = (y * (1.0 + sc_ref[0]) + sh_ref[0]).astype(o_ref.dtype)


def _norm_mod(x, g, sc, sh, *, rows_per_batch, tt):
    n, d = x.shape
    bpb = rows_per_batch // tt
    return pl.pallas_call(
        _norm_mod_kernel,
        grid=(n // tt,),
        in_specs=[pl.BlockSpec((tt, d), lambda i: (i, 0)),
                  pl.BlockSpec((1, d), lambda i: (0, 0)),
                  pl.BlockSpec((1, 1, d), lambda i: (i // bpb, 0, 0)),
                  pl.BlockSpec((1, 1, d), lambda i: (i // bpb, 0, 0))],
        out_specs=pl.BlockSpec((tt, d), lambda i: (i, 0)),
        out_shape=jax.ShapeDtypeStruct((n, d), BF16),
        compiler_params=_cparams("arbitrary"),
    )(x, g, sc, sh)


def _combine_kernel(x_ref, y0_ref, y1_ref, gates_ref, gt_ref, g_ref, o_ref):
    g0 = gates_ref[:, 0:1]
    g1 = gates_ref[:, 1:2]
    moe = y0_ref[...] * g0 + y1_ref[...] * g1
    x = x_ref[...] + gt_ref[0] * moe
    o_ref[...] = _rms(x, g_ref[...])


def _combine(x, y0, y1, gates, gt, g_final, *, rows_per_batch, tt):
    n, d = x.shape
    bpb = rows_per_batch // tt
    row = pl.BlockSpec((tt, d), lambda i: (i, 0))
    return pl.pallas_call(
        _combine_kernel,
        grid=(n // tt,),
        in_specs=[row, row, row,
                  pl.BlockSpec((tt, 2), lambda i: (i, 0)),
                  pl.BlockSpec((1, 1, d), lambda i: (i // bpb, 0, 0)),
                  pl.BlockSpec((1, d), lambda i: (0, 0))],
        out_specs=row,
        out_shape=jax.ShapeDtypeStruct((n, d), F32),
        compiler_params=_cparams("arbitrary"),
    )(x, y0, y1, gates, gt, g_final)


def _lane_prefix_sum(x, lane):
    s = 1
    while s < LANES:
        x = x + jnp.where(lane >= s, pltpu.roll(x, s, axis=1), 0.0)
        s *= 2
    return x


def _logf_cumsum_kernel(*refs, p_len, t_pad):
    if p_len:
        fl_ref, bf_ref, past_ref, logf_ref, cum_ref = refs
    else:
        fl_ref, bf_ref, logf_ref, cum_ref = refs
        past_ref = None
    lane = lax.broadcasted_iota(jnp.int32, (B_HEADS, LANES), 1)
    carry = jnp.zeros((B_HEADS, 1), F32)
    for c in range(p_len // LANES):
        sl = slice(c * LANES, (c + 1) * LANES)
        y = _lane_prefix_sum(past_ref[0, :, sl], lane) + carry
        cum_ref[0, :, sl] = y
        carry = y[:, LANES - 1:LANES]
    for c in range(t_pad // LANES):
        sl = slice(c * LANES, (c + 1) * LANES)
        z = fl_ref[0, :, sl] + bf_ref[...]
        logf = jnp.minimum(z, 0.0) - jnp.log1p(jnp.exp(-jnp.abs(z)))
        logf_ref[0, :, sl] = logf
        y = _lane_prefix_sum(logf, lane) + carry
        cum_ref[0, :, p_len + c * LANES:p_len + (c + 1) * LANES] = y
        carry = y[:, LANES - 1:LANES]


def _logf_cumsum(fl_t, b_forget, past_t):
    b, _, t_pad = fl_t.shape
    p_len = 0 if past_t is None else past_t.shape[2]
    in_specs = [pl.BlockSpec((1, B_HEADS, t_pad), lambda i: (i, 0, 0)),
                pl.BlockSpec((B_HEADS, 1), lambda i: (0, 0))]
    args = [fl_t, b_forget]
    if p_len:
        in_specs.append(pl.BlockSpec((1, B_HEADS, p_len), lambda i: (i, 0, 0)))
        args.append(past_t)
    return pl.pallas_call(
        functools.partial(_logf_cumsum_kernel, p_len=p_len, t_pad=t_pad),
        grid=(b,),
        in_specs=in_specs,
        out_specs=[pl.BlockSpec((1, B_HEADS, t_pad), lambda i: (i, 0, 0)),
                   pl.BlockSpec((1, B_HEADS, p_len + t_pad), lambda i: (i, 0, 0))],
        out_shape=[jax.ShapeDtypeStruct((b, B_HEADS, t_pad), F32),
                   jax.ShapeDtypeStruct((b, B_HEADS, p_len + t_pad), F32)],
        compiler_params=_cparams("arbitrary"),
    )(*args)


def _count(pred):
    return jnp.sum(jnp.where(pred, 1.0, 0.0), axis=1, keepdims=True)


def _indexer_kernel(qi_ref, small_ref, kit_ref, mask_ref, qh_ref, keys_ref, *,
                    tq, tkc, l_pad, l_real, p_len, nq_per_batch, n_sel):
    i = pl.program_id(0)
    row0 = (i % nq_per_batch) * tq
    q = qi_ref[...].astype(BF16)
    for h in range(IDX_HEADS):
        qh_ref[h * tq:(h + 1) * tq, :] = q[:, h * IDX_DIM:(h + 1) * IDX_DIM]
    w = small_ref[:, IDX_DIM:IDX_DIM + IDX_HEADS] * (IDX_HEADS ** -0.5)
    w = w * (IDX_DIM ** -0.5)
    qpos = p_len + row0 + lax.broadcasted_iota(jnp.int32, (tq, 1), 0)
    q_chunk = qpos // CHUNK
    last_allowed = (p_len + row0 + tq - 1) // CHUNK * CHUNK + CHUNK - 1

    def allowed_of(c0, width):
        kpos = c0 + lax.broadcasted_iota(jnp.int32, (tq, width), 1)
        return jnp.logical_and(kpos // CHUNK <= q_chunk, kpos < l_real)

    for c in range(l_pad // tkc):
        c0 = c * tkc

        @pl.when(c0 <= last_allowed)
        def _():
            s = jnp.dot(qh_ref[...], kit_ref[0, :, c0:c0 + tkc], preferred_element_type=F32)
            score = jnp.zeros((tq, tkc), F32)
            for h in range(IDX_HEADS):
                score = score + jnp.maximum(s[h * tq:(h + 1) * tq, :], 0.0) * w[:, h:h + 1]
            score = score + 0.0
            score = jnp.where(allowed_of(c0, tkc), score, NEG_INF)
            bits = lax.bitcast_convert_type(score, jnp.int32)
            keys_ref[:, c0:c0 + tkc] = jnp.where(bits < 0, bits ^ 0x7FFFFFFF, bits)

        @pl.when(c0 > last_allowed)
        def _():
            keys_ref[:, c0:c0 + tkc] = jnp.full((tq, tkc), KEY_NEG_INF, jnp.int32)

    k_f = float(n_sel)
    t0 = jnp.where(_count(keys_ref[...] >= 0) >= k_f, 0, INT_MIN).astype(jnp.int32)

    def bisect(it, t):
        cand = t + jnp.left_shift(jnp.int32(1), 30 - it)
        return jnp.where(_count(keys_ref[...] >= cand) >= k_f, cand, t)

    t = lax.fori_loop(0, 31, bisect, t0)
    keys = keys_ref[...]
    cge = _count(keys >= t)
    excess = jnp.max(jnp.where(t > KEY_NEG_INF, cge, 0.0)) > k_f

    @pl.when(jnp.logical_not(excess))
    def _():
        sel = jnp.logical_and(keys >= t, allowed_of(0, l_pad))
        mask_ref[...] = jnp.where(sel, 1.0, 0.0).astype(mask_ref.dtype)

    @pl.when(excess)
    def _():
        need = k_f - _count(keys > t)
        eq = keys == t
        idx = lax.broadcasted_iota(jnp.int32, (tq, l_pad), 1)
        m = jnp.zeros((tq, 1), jnp.int32)
        for bit in reversed(range(max(1, (l_pad - 1).bit_length()))):
            cand = m + (1 << bit)
            c_lt = _count(jnp.logical_and(eq, idx < cand))
            m = jnp.where(c_lt < need, cand, m)
        sel = jnp.logical_or(keys > t, jnp.logical_and(eq, idx <= m))
        sel = jnp.logical_and(sel, allowed_of(0, l_pad))
        mask_ref[...] = jnp.where(sel, 1.0, 0.0).astype(mask_ref.dtype)


def _indexer_mask(qi, small, ki_t, *, rows_per_batch, tq, tkc, l_real, p_len):
    n = qi.shape[0]
    l_pad = ki_t.shape[2]
    nqb = rows_per_batch // tq
    n_sel = min(TOPK_MAX, l_real // 4)
    kern = functools.partial(_indexer_kernel, tq=tq, tkc=tkc, l_pad=l_pad, l_real=l_real,
                             p_len=p_len, nq_per_batch=nqb, n_sel=n_sel)
    return pl.pallas_call(
        kern,
        grid=(n // tq,),
        in_specs=[pl.BlockSpec((tq, IDX_WIDTH), lambda i: (i, 0)),
                  pl.BlockSpec((tq, SMALL_WIDTH), lambda i: (i, 0)),
                  pl.BlockSpec((1, IDX_DIM, l_pad), lambda i: (i // nqb, 0, 0))],
        out_specs=pl.BlockSpec((tq, l_pad), lambda i: (i, 0)),
        out_shape=jax.ShapeDtypeStruct((n, l_pad), BF16),
        scratch_shapes=[pltpu.VMEM((IDX_HEADS * tq, IDX_DIM), BF16),
                        pltpu.VMEM((tq, l_pad), jnp.int32)],
        compiler_params=_cparams("arbitrary"),
    )(qi, small, ki_t)


def _attend(q_ref, k, v, bias_of_head, valid, m_ref, l_ref, acc_ref, *, tq, kv_heads, grp):
    scale = HEAD_DIM ** -0.5
    for g in range(kv_heads):
        kg = k[:, g * HEAD_DIM:(g + 1) * HEAD_DIM].astype(BF16)
        vg = v[:, g * HEAD_DIM:(g + 1) * HEAD_DIM].astype(BF16)
        qs = [q_ref[:, (g * grp + j) * HEAD_DIM:(g * grp + j + 1) * HEAD_DIM] for j in range(grp)]
        qg = (qs[0] if grp == 1 else jnp.concatenate(qs, axis=0)).astype(BF16)
        s_all = lax.dot_general(qg, kg, (((1,), (1,)), ((), ())),
                                preferred_element_type=F32) * scale
        for j in range(grp):
            h = g * grp + j
            s = s_all[j * tq:(j + 1) * tq, :] + bias_of_head(h)
            s = jnp.where(valid, s, NEG_INF)
            m_prev = m_ref[h]
            m_new = jnp.maximum(m_prev, jnp.max(s, axis=1, keepdims=True))
            m_safe = jnp.where(m_new == NEG_INF, 0.0, m_new)
            alpha = jnp.exp(m_prev - m_safe)
            p = jnp.exp(s - m_safe)
            l_ref[h] = alpha * l_ref[h] + jnp.sum(p, axis=1, keepdims=True)
            acc_ref[h] = alpha * acc_ref[h] + jnp.dot(p.astype(BF16), vg,
                                                      preferred_element_type=F32)
            m_ref[h] = m_new


def _flash_kernel(*refs, mode, n_a, n_b, tq, tk_a, tk_b, p_len, nq_per_batch, kv_heads, grp,
                  causal_skip):
    heads = kv_heads * grp
    it = iter(refs)
    q_ref = next(it)
    if n_a:
        ka_ref, va_ref = next(it), next(it)
    kb_ref, vb_ref = next(it), next(it)
    if mode == "dsa":
        if n_a:
            maska_ref, biasa_ref = next(it), next(it)
        maskb_ref, biasb_ref = next(it), next(it)
    else:
        fq_ref = next(it)
        if n_a:
            fka_ref = next(it)
        fkb_ref = next(it)
    o_ref, m_ref, l_ref, acc_ref = next(it), next(it), next(it), next(it)

    i = pl.program_id(0) % nq_per_batch
    j = pl.program_id(1)

    @pl.when(j == 0)
    def _():
        m_ref[...] = jnp.full(m_ref.shape, NEG_INF, F32)
        l_ref[...] = jnp.zeros(l_ref.shape, F32)
        acc_ref[...] = jnp.zeros(acc_ref.shape, F32)

    qpos = p_len + i * tq + lax.broadcasted_iota(jnp.int32, (tq, 1), 0)
    common = dict(tq=tq, kv_heads=kv_heads, grp=grp)

    if n_a:
        @pl.when(j < n_a)
        def _():
            if mode == "dsa":
                valid = maska_ref[...] > 0
                bias = lambda h: biasa_ref[0, h]
            else:
                valid = (j * tk_a + lax.broadcasted_iota(jnp.int32, (tq, tk_a), 1)) <= qpos
                bias = lambda h: fq_ref[:, h:h + 1] - fka_ref[0, h:h + 1, :]
            _attend(q_ref, ka_ref[0], va_ref[0], bias, valid, m_ref, l_ref, acc_ref, **common)

    jb = j - n_a
    run_b = jb >= 0
    if causal_skip:
        run_b = jnp.logical_and(run_b, jb * tk_b <= p_len + i * tq + tq - 1)

    @pl.when(run_b)
    def _():
        if mode == "dsa":
            valid = maskb_ref[...] > 0
            bias = lambda h: biasb_ref[0, h]
        else:
            kpos = p_len + jb * tk_b + lax.broadcasted_iota(jnp.int32, (tq, tk_b), 1)
            valid = kpos <= qpos
            bias = lambda h: fq_ref[:, h:h + 1] - fkb_ref[0, h:h + 1, :tk_b]
        _attend(q_ref, kb_ref[...], vb_ref[...], bias, valid, m_ref, l_ref, acc_ref, **common)

    @pl.when(j == n_a + n_b - 1)
    def _():
        for h in range(heads):
            o_ref[:, h * HEAD_DIM:(h + 1) * HEAD_DIM] = (acc_ref[h] / l_ref[h]).astype(o_ref.dtype)


def _flash(mode, q, k_new, v_new, *, k_past=None, v_past=None, rows_per_batch, tq, tk_a, tk_b,
           mask_a=None, bias_a=None, mask_b=None, bias_b=None, bias_b_map=None,
           fq=None, fk_t=None):
    n, width = q.shape
    heads = width // HEAD_DIM
    kv_w = k_new.shape[1]
    kv_heads = kv_w // HEAD_DIM
    grp = heads // kv_heads
    t = rows_per_batch
    nqb = t // tq
    p_len = 0 if k_past is None else k_past.shape[1]
    n_a = p_len // tk_a if p_len else 0
    n_b = t // tk_b
    causal_skip = n_b > 1

    def a_blk(j):
        return jnp.minimum(j, n_a - 1)

    def b_blk(g, j):
        jb = jnp.maximum(j - n_a, 0)
        if causal_skip:
            jb = jnp.minimum(jb, (p_len + (g % nqb) * tq + tq - 1) // tk_b)
        return jb

    in_specs = [pl.BlockSpec((tq, width), lambda g, j: (g, 0))]
    args = [q]
    if n_a:
        spec = pl.BlockSpec((1, tk_a, kv_w), lambda g, j: (g // nqb, a_blk(j), 0))
        in_specs += [spec, spec]
        args += [k_past, v_past]
    spec = pl.BlockSpec((tk_b, kv_w), lambda g, j: ((g // nqb) * n_b + b_blk(g, j), 0))
    in_specs += [spec, spec]
    args += [k_new, v_new]
    if mode == "dsa":
        if n_a:
            in_specs += [pl.BlockSpec((tq, tk_a), lambda g, j: (g, a_blk(j))),
                         pl.BlockSpec((1, heads, tq, tk_a), lambda g, j: (a_blk(j), 0, 0, 0))]
            args += [mask_a, bias_a]
        in_specs += [pl.BlockSpec((tq, tk_b), lambda g, j: (g, b_blk(g, j))),
                     pl.BlockSpec((1, heads, tq, tk_b),
                                  lambda g, j: (bias_b_map(g % nqb, b_blk(g, j)), 0, 0, 0))]
        args += [mask_b, bias_b]
    else:
        in_specs.append(pl.BlockSpec((tq, heads), lambda g, j: (g, 0)))
        args.append(fq)
        if n_a:
            in_specs.append(pl.BlockSpec((1, heads, tk_a), lambda g, j: (g // nqb, 0, a_blk(j))))
            args.append(fk_t)
        fkb_w = max(tk_b, LANES)
        fkb_off = p_len // fkb_w
        in_specs.append(pl.BlockSpec((1, heads, fkb_w),
                                     lambda g, j: (g // nqb, 0, fkb_off + b_blk(g, j))))
        args.append(fk_t)

    kern = functools.partial(_flash_kernel, mode=mode, n_a=n_a, n_b=n_b, tq=tq, tk_a=tk_a,
                             tk_b=tk_b, p_len=p_len, nq_per_batch=nqb, kv_heads=kv_heads,
                             grp=grp, causal_skip=causal_skip)
    return pl.pallas_call(
        kern,
        grid=(n // tq, n_a + n_b),
        in_specs=in_specs,
        out_specs=pl.BlockSpec((tq, width), lambda g, j: (g, 0)),
        out_shape=jax.ShapeDtypeStruct((n, width), BF16),
        scratch_shapes=[pltpu.VMEM((heads, tq, 1), F32),
                        pltpu.VMEM((heads, tq, 1), F32),
                        pltpu.VMEM((heads, tq, HEAD_DIM), F32)],
        compiler_params=_cparams("arbitrary", "arbitrary"),
    )(*args)


def _router_kernel(lt_ref, eidx_ref, gate_ref):
    row = lambda r: lt_ref[r:r + 1, :]
    g = [row(r) for r in range(N_GROUPS)]
    gmax = functools.reduce(jnp.maximum, g)
    gsel = jnp.full(gmax.shape, N_GROUPS - 1, jnp.int32)
    for r in reversed(range(N_GROUPS - 1)):
        gsel = jnp.where(g[r] == gmax, r, gsel)
    denom = functools.reduce(lambda a, b: a + b, [jnp.exp(x - gmax) for x in g])
    g_prob = 1.0 / denom
    e_in = []
    for kk in range(EXPERTS_PER_GROUP):
        v = row(N_GROUPS + (N_GROUPS - 1) * EXPERTS_PER_GROUP + kk)
        for r in reversed(range(N_GROUPS - 1)):
            v = jnp.where(gsel == r, row(N_GROUPS + r * EXPERTS_PER_GROUP + kk), v)
        e_in.append(v)
    v1 = functools.reduce(jnp.maximum, e_in)
    i1 = jnp.full(v1.shape, EXPERTS_PER_GROUP - 1, jnp.int32)
    for kk in reversed(range(EXPERTS_PER_GROUP - 1)):
        i1 = jnp.where(e_in[kk] == v1, kk, i1)
    rest = [jnp.where(i1 == kk, NEG_INF, e_in[kk]) for kk in range(EXPERTS_PER_GROUP)]
    v2 = functools.reduce(jnp.maximum, rest)
    i2 = jnp.full(v2.shape, EXPERTS_PER_GROUP - 1, jnp.int32)
    for kk in reversed(range(EXPERTS_PER_GROUP - 1)):
        i2 = jnp.where(jnp.logical_and(rest[kk] == v2, i1 != kk), kk, i2)
    e2 = jnp.exp(v2 - v1)
    inv = 1.0 / (1.0 + e2)
    eidx_ref[0:1, :] = gsel * EXPERTS_PER_GROUP + i1
    eidx_ref[1:2, :] = gsel * EXPERTS_PER_GROUP + i2
    gate_ref[0:1, :] = inv * g_prob
    gate_ref[1:2, :] = (e2 * inv) * g_prob


def _router(logits_t):
    n = logits_t.shape[1]
    full = lambda r: pl.BlockSpec((r, n), lambda: (0, 0))
    return pl.pallas_call(
        _router_kernel,
        in_specs=[full(logits_t.shape[0])],
        out_specs=[full(TOP_K_IN_GROUP), full(TOP_K_IN_GROUP)],
        out_shape=[jax.ShapeDtypeStruct((TOP_K_IN_GROUP, n), jnp.int32),
                   jax.ShapeDtypeStruct((TOP_K_IN_GROUP, n), F32)],
        compiler_params=pltpu.CompilerParams(vmem_limit_bytes=VMEM_LIMIT),
    )(logits_t)


EXPERT_ROWS = 1024
EXPERT_CHUNK = 256
EXPERT_F_TILE = 256
EXPERT_N_TILE = 512


def _experts_kernel(sb_e_ref, sb_rows_ref, sb_blk_ref, x_ref, w1_ref, w3_ref, w2_ref, o_ref,
                    act_ref, *, nf):
    del sb_e_ref, sb_blk_ref
    s = pl.program_id(0)
    t = pl.program_id(1)
    n_chunks = (sb_rows_ref[s] + EXPERT_CHUNK - 1) // EXPERT_CHUNK

    @pl.when(t < nf)
    def _():
        w1 = w1_ref[0].astype(BF16)
        w3 = w3_ref[0].astype(BF16)

        def body(c, carry):
            r0 = pl.multiple_of(c * EXPERT_CHUNK, EXPERT_CHUNK)
            xc = x_ref[pl.ds(r0, EXPERT_CHUNK), :]
            a = jnp.dot(xc, w1, preferred_element_type=F32)
            b = jnp.dot(xc, w3, preferred_element_type=F32)
            act_ref[t, pl.ds(r0, EXPERT_CHUNK), :] = (a * jax.nn.sigmoid(a) * b).astype(BF16)
            return carry

        lax.fori_loop(0, n_chunks, body, 0)

    @pl.when(t >= nf)
    def _():
        def body(c, carry):
            r0 = pl.multiple_of(c * EXPERT_CHUNK, EXPERT_CHUNK)
            y = None
            for f in range(nf):
                part = jnp.dot(act_ref[f, pl.ds(r0, EXPERT_CHUNK), :],
                               w2_ref[0, f * EXPERT_F_TILE:(f + 1) * EXPERT_F_TILE, :].astype(BF16),
                               preferred_element_type=F32)
                y = part if y is None else y + part
            o_ref[pl.ds(r0, EXPERT_CHUNK), :] = y
            return carry

        lax.fori_loop(0, n_chunks, body, 0)


def _experts(xb, sb_expert, sb_rows, sb_blk, w1, w3, w2, n_sb):
    d = xb.shape[1]
    d_e = w1.shape[2]
    nf = d_e // EXPERT_F_TILE
    nn = d // EXPERT_N_TILE
    r = EXPERT_ROWS
    grid_spec = pltpu.PrefetchScalarGridSpec(
        num_scalar_prefetch=3,
        grid=(n_sb, nf + nn),
        in_specs=[
            pl.BlockSpec((r, d), lambda s, t, e, rows, blk: (blk[s], 0)),
            pl.BlockSpec((1, d, EXPERT_F_TILE),
                         lambda s, t, e, rows, blk: (e[s], 0, jnp.minimum(t, nf - 1))),
            pl.BlockSpec((1, d, EXPERT_F_TILE),
                         lambda s, t, e, rows, blk: (e[s], 0, jnp.minimum(t, nf - 1))),
            pl.BlockSpec((1, d_e, EXPERT_N_TILE),
                         lambda s, t, e, rows, blk: (e[s], 0, jnp.maximum(t - nf, 0))),
        ],
        out_specs=pl.BlockSpec(
            (r, EXPERT_N_TILE),
            lambda s, t, e, rows, blk: (blk[s], jnp.where(rows[s] > 0, jnp.maximum(t - nf, 0), 0))),
        scratch_shapes=[pltpu.VMEM((nf, r, EXPERT_F_TILE), BF16)],
    )
    return pl.pallas_call(
        functools.partial(_experts_kernel, nf=nf),
        grid_spec=grid_spec,
        out_shape=jax.ShapeDtypeStruct(xb.shape, F32),
        compiler_params=_cparams("arbitrary", "arbitrary"),
    )(sb_expert, sb_rows, sb_blk, xb, w1, w3, w2)


def _dispatch(expert, n_sb):
    n = expert.shape[0]
    a = n * TOP_K_IN_GROUP
    r = EXPERT_ROWS
    flat_e = expert.reshape(a)
    flat_tok = jnp.arange(a, dtype=jnp.int32) // TOP_K_IN_GROUP
    order = jnp.argsort(flat_e)
    se = flat_e[order]
    counts = jnp.bincount(flat_e, length=N_EXPERTS).astype(jnp.int32)
    nsb_e = (counts + r - 1) // r
    sb_end = jnp.cumsum(nsb_e)
    sb_start = sb_end - nsb_e
    seg_start = jnp.cumsum(counts) - counts
    dest = sb_start[se] * r + jnp.arange(a, dtype=jnp.int32) - seg_start[se]
    row_tok = jnp.full(((n_sb + 1) * r,), n, jnp.int32).at[dest].set(flat_tok[order])
    dest_of_assign = jnp.zeros((a,), jnp.int32).at[order].set(dest).reshape(n, TOP_K_IN_GROUP)
    total = sb_end[-1]
    s_idx = jnp.arange(n_sb, dtype=jnp.int32)
    e_of_sb = jnp.minimum(jnp.searchsorted(sb_end, s_idx, side="right"), N_EXPERTS - 1).astype(jnp.int32)
    rows_of_sb = jnp.clip(counts[e_of_sb] - (s_idx - sb_start[e_of_sb]) * r, 0, r)
    live = s_idx < total
    last_e = e_of_sb[jnp.maximum(total - 1, 0)]
    sb_expert = jnp.where(live, e_of_sb, last_e).astype(jnp.int32)
    sb_rows = jnp.where(live, rows_of_sb, 0).astype(jnp.int32)
    sb_blk = jnp.where(live, s_idx, n_sb).astype(jnp.int32)
    return row_tok, dest_of_assign, sb_expert, sb_rows, sb_blk


def _t5_bucket(rel):
    half = N_BUCKETS // 2
    max_exact = half // 2
    bucket = jnp.where(rel > 0, half, 0).astype(jnp.int32)
    n = jnp.abs(rel)
    large = max_exact + (jnp.log(jnp.maximum(n, 1).astype(jnp.float32) / max_exact)
                         / math.log(MAX_DISTANCE / max_exact) * (half - max_exact)).astype(jnp.int32)
    large = jnp.minimum(large, half - 1)
    return bucket + jnp.where(n < max_exact, n, large).astype(jnp.int32)


def _bias_tiles(rel_bias, rel0_list, tq, tk):
    rel0 = jnp.asarray(rel0_list, jnp.int32)[:, None, None]
    rel = rel0 + jnp.arange(tk, dtype=jnp.int32)[None, None, :] - jnp.arange(tq, dtype=jnp.int32)[None, :, None]
    return jnp.transpose(rel_bias[_t5_bucket(rel)].astype(F32), (0, 3, 1, 2))


def _pick(n, candidates):
    for c in candidates:
        if n % c == 0:
            return c
    return n


def _mixer_half(x, mods, past, w_in_packed, b_forget, w_out, rel_bias, g_mix, g_ffn):
    b, t, d = x.shape
    n = b * t
    sh1, sc1, gt1, sh2, sc2, _ = mods
    xf = x.reshape(n, d)
    tt = _pick(t, (512, 256, 128, 64, 32, 16))
    h = _norm_mod(xf, g_mix, sc1, sh1, rows_per_batch=t, tt=tt)

    tm = _pick(n, (512, 256, 128))

    def proj(col_off, width, tn):
        return _matmul([h], w_in_packed, col_off=col_off, n=width, tm=tm, tn=tn)

    off = 0
    qa = proj(off, A_WIDTH, 1024); off += A_WIDTH
    ka = proj(off, A_KV_WIDTH, 512); off += A_KV_WIDTH
    va = proj(off, A_KV_WIDTH, 512); off += A_KV_WIDTH
    qi = proj(off, IDX_WIDTH, 1024); off += IDX_WIDTH
    qb = proj(off, B_WIDTH, 1024); off += B_WIDTH
    kb = proj(off, B_WIDTH, 1024); off += B_WIDTH
    vb = proj(off, B_WIDTH, 1024); off += B_WIDTH
    small = proj(off, SMALL_WIDTH, SMALL_WIDTH)

    ki = small[:, :IDX_DIM]
    fl = small[:, IDX_DIM + IDX_HEADS:IDX_DIM + IDX_HEADS + B_HEADS]

    p_len = 0 if past is None else past[0].shape[1]
    t_pad = -(-t // LANES) * LANES
    l_real = p_len + t
    l_pad = p_len + t_pad

    fl_t = jnp.transpose(fl.reshape(b, t, B_HEADS), (0, 2, 1))
    fl_t = jnp.pad(fl_t, ((0, 0), (0, 0), (0, t_pad - t)))
    past_logf_t = None if past is None else jnp.transpose(past[5], (0, 2, 1))
    logf_t, fcum_t = _logf_cumsum(fl_t, b_forget.reshape(B_HEADS, 1), past_logf_t)
    logf = jnp.transpose(logf_t[:, :, :t], (0, 2, 1))
    fq = jnp.transpose(fcum_t[:, :, p_len:p_len + t], (0, 2, 1)).reshape(n, B_HEADS)

    ki_b = ki.reshape(b, t, IDX_DIM)
    if past is not None:
        ki_b = jnp.concatenate([past[2], ki_b], axis=1)
    ki_t = jnp.transpose(jnp.pad(ki_b, ((0, 0), (0, l_pad - l_real), (0, 0))), (0, 2, 1)).astype(BF16)

    if past is None:
        tq = _pick(t, (256, 128, 64))
        tk = tq
        mask = _indexer_mask(qi, small, ki_t, rows_per_batch=t, tq=tq, tkc=_pick(l_pad, (256, 128)),
                             l_real=l_real, p_len=0)
        n_far = -(-(MAX_DISTANCE + tk - 1) // tk)
        bias_b = _bias_tiles(rel_bias, [-c * tk for c in range(n_far + 1)], tq, tk)
        out_a = _flash("dsa", qa, ka, va, rows_per_batch=t, tq=tq, tk_a=tk, tk_b=tk,
                       mask_b=mask, bias_b=bias_b,
                       bias_b_map=lambda i, jb: jnp.minimum(i - jb, n_far))
        out_b = _flash("fox", qb, kb, vb, rows_per_batch=t, tq=tq, tk_a=tk, tk_b=tk,
                       fq=fq, fk_t=fcum_t)
    else:
        tq = t
        tk_a = _pick(p_len, (512, 256, 128))
        mask = _indexer_mask(qi, small, ki_t, rows_per_batch=t, tq=tq, tkc=l_pad,
                             l_real=l_real, p_len=p_len)
        mask_a = mask[:, :p_len]
        mask_b = mask[:, p_len:p_len + t]
        n_a = p_len // tk_a
        bias_a = _bias_tiles(rel_bias, [c * tk_a - p_len for c in range(n_a)], tq, tk_a)
        bias_b = _bias_tiles(rel_bias, [0], tq, t)
        cache = lambda a: a.reshape(b, p_len, -1)
        out_a = _flash("dsa", qa, ka, va, k_past=cache(past[0]), v_past=cache(past[1]),
                       rows_per_batch=t, tq=tq, tk_a=tk_a, tk_b=t, mask_a=mask_a, bias_a=bias_a,
                       mask_b=mask_b, bias_b=bias_b, bias_b_map=lambda i, jb: 0)
        out_b = _flash("fox", qb, kb, vb, k_past=cache(past[3]), v_past=cache(past[4]),
                       rows_per_batch=t, tq=tq, tk_a=tk_a, tk_b=t, fq=fq, fk_t=fcum_t)

    gate1 = gt1.reshape(1, d) if b == 1 else jnp.repeat(gt1.reshape(b, d), t, axis=0)
    x1 = _matmul([out_a, out_b], w_out, tm=tm, tn=512, res=xf, gate=gate1)
    h2 = _norm_mod(x1, g_ffn, sc2, sh2, rows_per_batch=t, tt=tt)

    state = (ka.reshape(1, b, t, A_KV_HEADS, HEAD_DIM), va.reshape(1, b, t, A_KV_HEADS, HEAD_DIM),
             ki.reshape(1, b, t, IDX_DIM),
             kb.reshape(1, b, t, B_HEADS, HEAD_DIM), vb.reshape(1, b, t, B_HEADS, HEAD_DIM),
             logf.reshape(1, b, t, B_HEADS))
    return x1, h2, state


def kernel(x_prompt, x_sample, cache_a_k, cache_a_v, cache_idx_k, cache_b_k, cache_b_v, cache_b_logf, c_prompt, c_sample, w_in, b_forget, w_out, rel_bias, w_mod, b_mod, g_mix, g_ffn, w_router_group, b_router_group, w_router_expert, b_router_expert, w_gate, w_up, w_down, g_final):
    assert w_in.shape[0] == 1, "single-layer trunk"
    d = x_prompt.shape[-1]
    bp, tp, _ = x_prompt.shape
    bs, ts, _ = x_sample.shape
    n_p, n_s = bp * tp, bs * ts

    n_c = bp + bs
    c_rows = -(-n_c // 8) * 8
    c_all = jnp.pad(jnp.concatenate([c_prompt, c_sample], axis=0), ((0, c_rows - n_c), (0, 0)))
    mod = _matmul([c_all], w_mod[0], tm=c_rows, tn=512, silu_a=True, bias=b_mod[0].reshape(1, -1))

    def mods_of(lo, hi):
        m6 = mod[lo:hi].reshape(hi - lo, 6, 1, d)
        return [m6[:, i] for i in range(6)]

    mods_p, mods_s = mods_of(0, bp), mods_of(bp, n_c)

    w = w_in[0]
    o1 = A_WIDTH + 2 * A_KV_WIDTH + IDX_WIDTH
    o2 = o1 + IDX_DIM + IDX_HEADS
    o3 = o2 + 3 * B_WIDTH
    tail_pad = SMALL_WIDTH - (IDX_DIM + IDX_HEADS + B_HEADS)
    w_packed = jnp.concatenate(
        [w[:, :o1], w[:, o2:o3], w[:, o1:o2], w[:, o3:], jnp.zeros((d, tail_pad), w.dtype)],
        axis=1).astype(BF16)

    g_mix2, g_ffn2, g_fin2 = g_mix[0].reshape(1, d), g_ffn[0].reshape(1, d), g_final.reshape(1, d)
    past = tuple(c[0] for c in (cache_a_k, cache_a_v, cache_idx_k, cache_b_k, cache_b_v, cache_b_logf))
    x1_p, h2_p, st_p = _mixer_half(x_prompt, mods_p, None, w_packed, b_forget[0], w_out[0],
                                   rel_bias, g_mix2, g_ffn2)
    x1_s, h2_s, st_s = _mixer_half(x_sample, mods_s, past, w_packed, b_forget[0], w_out[0],
                                   rel_bias, g_mix2, g_ffn2)

    n_tok = n_p + n_s
    h2 = jnp.concatenate([h2_p, h2_s], axis=0)
    w_r = jnp.concatenate([w_router_group[0], w_router_expert[0]], axis=1)
    n_cls = w_r.shape[1]
    w_r = jnp.pad(w_r, ((0, 0), (0, LANES - n_cls)))
    b_r = jnp.pad(jnp.concatenate([b_router_group[0], b_router_expert[0]]), (0, LANES - n_cls))
    logits = _matmul([h2], w_r, tm=_pick(n_tok, (512, 256, 128)), tn=LANES, bias=b_r.reshape(1, LANES))
    eidx_t, gate_t = _router(jnp.transpose(logits))
    expert = jnp.transpose(eidx_t)
    gates = jnp.transpose(gate_t)

    n_sb = (n_tok * TOP_K_IN_GROUP) // EXPERT_ROWS + N_EXPERTS
    row_tok, dest_of_assign, sb_expert, sb_rows, sb_blk = _dispatch(expert, n_sb)
    h2_pad = jnp.concatenate([h2, jnp.zeros((1, d), h2.dtype)], axis=0)
    xb = h2_pad[row_tok]
    yb = _experts(xb, sb_expert, sb_rows, sb_blk, w_gate[0], w_up[0], w_down[0], n_sb)
    y0 = yb[dest_of_assign[:, 0]]
    y1 = yb[dest_of_assign[:, 1]]

    def finish(x1, lo, hi, gt2, t):
        tt = _pick(t, (128, 64, 32, 16))
        return _combine(x1, y0[lo:hi], y1[lo:hi], gates[lo:hi], gt2, g_fin2, rows_per_batch=t, tt=tt)

    y_p = finish(x1_p, 0, n_p, mods_p[5], tp).reshape(bp, tp, d)
    y_s = finish(x1_s, n_p, n_tok, mods_s[5], ts).reshape(bs, ts, d)
    return (y_p, y_s) + st_p + st_s
```

```python
import functools
import math

import numpy as np
import jax
import jax.numpy as jnp
from jax import lax
from jax.experimental import pallas as pl
from jax.experimental.pallas import tpu as pltpu

CHUNK = 64
HEAD_DIM = 128
A_HEADS = 16
A_KV_HEADS = 4
IDX_HEADS = 32
IDX_DIM = 64
TOPK_MAX = 256
B_HEADS = 16
N_BUCKETS = 32
MAX_DISTANCE = 128
N_GROUPS = 4
EXPERTS_PER_GROUP = 8
N_EXPERTS = N_GROUPS * EXPERTS_PER_GROUP
TOP_K_IN_GROUP = 2
EPS = 1e-6

A_WIDTH = A_HEADS * HEAD_DIM
A_KV_WIDTH = A_KV_HEADS * HEAD_DIM
B_WIDTH = B_HEADS * HEAD_DIM
IDX_WIDTH = IDX_HEADS * IDX_DIM
SMALL_WIDTH = 128

LANES = 128
V7X_VMEM_BYTES = 64 * 1024 * 1024
VMEM_LIMIT = 56 * 1024 * 1024

BF16 = jnp.bfloat16
F32 = jnp.float32
NEG_INF = float("-inf")
INT_MIN = -(2 ** 31)
KEY_NEG_INF = int(np.int32(np.float32(-np.inf).view(np.int32)) ^ np.int32(0x7FFFFFFF))
LOG2E = math.log2(math.e)
Q_SCALE = HEAD_DIM ** -0.5 * LOG2E


def _cparams(*sem):
    return pltpu.CompilerParams(dimension_semantics=sem, vmem_limit_bytes=VMEM_LIMIT)


def _matmul_kernel(*refs, n_a, silu_a, epilogue, out_scale, n_out):
    a_refs = refs[:n_a]
    w_ref = refs[n_a]
    rest = refs[n_a + 1:]
    o_refs = rest[len(rest) - n_out:]
    acc = None
    k0 = 0
    for a_ref in a_refs:
        a = a_ref[...]
        if silu_a:
            a = a * jax.nn.sigmoid(a)
        kk = a.shape[1]
        part = jnp.dot(a.astype(BF16), w_ref[k0:k0 + kk, :].astype(BF16),
                       preferred_element_type=F32)
        acc = part if acc is None else acc + part
        k0 += kk
    if epilogue == "bias":
        acc = acc + rest[0][...]
    elif epilogue == "gated_residual":
        acc = rest[0][...] + rest[1][...] * acc
    if out_scale is not None:
        acc = acc * out_scale
    for o_ref in o_refs:
        o_ref[...] = acc.astype(o_ref.dtype)


def _matmul(a_list, w, *, col_off=0, n=None, tm, tn, silu_a=False, bias=None, res=None,
            gate=None, out_dtypes=(F32,), out_scale=None):
    m = a_list[0].shape[0]
    k = w.shape[0]
    n = w.shape[1] - col_off if n is None else n
    assert m % tm == 0 and n % tn == 0 and col_off % tn == 0
    assert sum(a.shape[1] for a in a_list) == k
    cb = col_off // tn
    in_specs = [pl.BlockSpec((tm, a.shape[1]), lambda j, i: (i, 0)) for a in a_list]
    in_specs.append(pl.BlockSpec((k, tn), lambda j, i: (0, j + cb)))
    args = list(a_list) + [w]
    if bias is not None:
        epilogue = "bias"
        in_specs.append(pl.BlockSpec((1, tn), lambda j, i: (0, j)))
        args.append(bias)
    elif res is not None:
        epilogue = "gated_residual"
        in_specs.append(pl.BlockSpec((tm, tn), lambda j, i: (i, j)))
        args.append(res)
        if gate.shape[0] == 1:
            in_specs.append(pl.BlockSpec((1, tn), lambda j, i: (0, j)))
        else:
            in_specs.append(pl.BlockSpec((tm, tn), lambda j, i: (i, j)))
        args.append(gate)
    else:
        epilogue = None
    outs = pl.pallas_call(
        functools.partial(_matmul_kernel, n_a=len(a_list), silu_a=silu_a, epilogue=epilogue,
                          out_scale=out_scale, n_out=len(out_dtypes)),
        grid=(n // tn, m // tm),
        in_specs=in_specs,
        out_specs=[pl.BlockSpec((tm, tn), lambda j, i: (i, j)) for _ in out_dtypes],
        out_shape=[jax.ShapeDtypeStruct((m, n), dt) for dt in out_dtypes],
        compiler_params=_cparams("arbitrary", "arbitrary"),
    )(*args)
    return outs[0] if len(out_dtypes) == 1 else outs


def _rms(x, g):
    return x * lax.rsqrt(jnp.mean(x * x, axis=-1, keepdims=True) + EPS) * g


def _norm_mod_kernel(x_ref, g_ref, sc_ref, sh_ref, o_ref):
    y = _rms(x_ref[...], g_ref[...])
    o_ref[...] = (y * (1.0 + sc_ref[0]) + sh_ref[0]).astype(o_ref.dtype)


def _norm_mod(x, g, sc, sh, *, rows_per_batch, tt, out_dtype):
    n, d = x.shape
    bpb = rows_per_batch // tt
    return pl.pallas_call(
        _norm_mod_kernel,
        grid=(n // tt,),
        in_specs=[pl.BlockSpec((tt, d), lambda i: (i, 0)),
                  pl.BlockSpec((1, d), lambda i: (0, 0)),
                  pl.BlockSpec((1, 1, d), lambda i: (i // bpb, 0, 0)),
                  pl.BlockSpec((1, 1, d), lambda i: (i // bpb, 0, 0))],
        out_specs=pl.BlockSpec((tt, d), lambda i: (i, 0)),
        out_shape=jax.ShapeDtypeStruct((n, d), out_dtype),
        compiler_params=_cparams("arbitrary"),
    )(x, g, sc, sh)


def _combine_kernel(x_ref, y0_ref, y1_ref, gates_ref, gt_ref, g_ref, o_ref):
    g0 = gates_ref[:, 0:1]
    g1 = gates_ref[:, 1:2]
    moe = y0_ref[...] * g0 + y1_ref[...] * g1
    x = x_ref[...] + gt_ref[0] * moe
    o_ref[...] = _rms(x, g_ref[...])


def _combine(x, y01, gates, gt, g_final, *, row_off, n_tok, rows_per_batch, tt):
    n, d = x.shape
    bpb = rows_per_batch // tt
    b0 = row_off // tt
    b1 = (n_tok + row_off) // tt
    row = pl.BlockSpec((tt, d), lambda i: (i, 0))
    return pl.pallas_call(
        _combine_kernel,
        grid=(n // tt,),
        in_specs=[row,
                  pl.BlockSpec((tt, d), lambda i: (i + b0, 0)),
                  pl.BlockSpec((tt, d), lambda i: (i + b1, 0)),
                  pl.BlockSpec((tt, 2), lambda i: (i, 0)),
                  pl.BlockSpec((1, 1, d), lambda i: (i // bpb, 0, 0)),
                  pl.BlockSpec((1, d), lambda i: (0, 0))],
        out_specs=row,
        out_shape=jax.ShapeDtypeStruct((n, d), F32),
        compiler_params=_cparams("arbitrary"),
    )(x, y01, y01, gates, gt, g_final)


GATHER_ROWS = 256


def _gather_kernel(idx_ref, nlive_ref, src_ref, o_ref, buf_ref, sem):
    i = pl.program_id(0)

    @pl.when(i < nlive_ref[0])
    def _():
        base = i * GATHER_ROWS

        def row_copy(r, src_row):
            return pltpu.make_async_copy(src_ref.at[pl.ds(src_row, 1)], buf_ref.at[pl.ds(r, 1)], sem)

        def start(r, carry):
            row_copy(r, idx_ref[base + r]).start()
            return carry

        def wait(r, carry):
            row_copy(r, 0).wait()
            return carry

        lax.fori_loop(0, GATHER_ROWS, start, 0)
        lax.fori_loop(0, GATHER_ROWS, wait, 0)
        o_ref[...] = buf_ref[...].astype(o_ref.dtype)


def _gather_rows(src, idx, n_live_blocks, out_dtype):
    nr = idx.shape[0]
    d = src.shape[1]
    assert nr % GATHER_ROWS == 0
    grid_spec = pltpu.PrefetchScalarGridSpec(
        num_scalar_prefetch=2,
        grid=(nr // GATHER_ROWS,),
        in_specs=[pl.BlockSpec(memory_space=pl.ANY)],
        out_specs=pl.BlockSpec((GATHER_ROWS, d), lambda i, idx, nl: (jnp.minimum(i, nl[0] - 1), 0)),
        scratch_shapes=[pltpu.VMEM((GATHER_ROWS, d), src.dtype), pltpu.SemaphoreType.DMA(())],
    )
    return pl.pallas_call(
        _gather_kernel,
        grid_spec=grid_spec,
        out_shape=jax.ShapeDtypeStruct((nr, d), out_dtype),
        compiler_params=_cparams("arbitrary"),
    )(idx, n_live_blocks, src)


def _lane_prefix_sum(x, lane):
    s = 1
    while s < LANES:
        x = x + jnp.where(lane >= s, pltpu.roll(x, s, axis=1), 0.0)
        s *= 2
    return x


def _logf_cumsum_kernel(*refs, p_len, t_pad):
    if p_len:
        fl_ref, bf_ref, past_ref, logf_ref, cum_ref = refs
    else:
        fl_ref, bf_ref, logf_ref, cum_ref = refs
        past_ref = None
    lane = lax.broadcasted_iota(jnp.int32, (B_HEADS, LANES), 1)
    carry = jnp.zeros((B_HEADS, 1), F32)
    for c in range(p_len // LANES):
        sl = slice(c * LANES, (c + 1) * LANES)
        y = _lane_prefix_sum(past_ref[0, :, sl], lane) + carry
        cum_ref[0, :, sl] = y
        carry = y[:, LANES - 1:LANES]
    for c in range(t_pad // LANES):
        sl = slice(c * LANES, (c + 1) * LANES)
        z = fl_ref[0, :, sl] + bf_ref[...]
        logf = jnp.minimum(z, 0.0) - jnp.log1p(jnp.exp(-jnp.abs(z)))
        logf_ref[0, :, sl] = logf
        y = _lane_prefix_sum(logf, lane) + carry
        cum_ref[0, :, p_len + c * LANES:p_len + (c + 1) * LANES] = y
        carry = y[:, LANES - 1:LANES]


def _logf_cumsum(fl_t, b_forget, past_t):
    b, _, t_pad = fl_t.shape
    p_len = 0 if past_t is None else past_t.shape[2]
    in_specs = [pl.BlockSpec((1, B_HEADS, t_pad), lambda i: (i, 0, 0)),
                pl.BlockSpec((B_HEADS, 1), lambda i: (0, 0))]
    args = [fl_t, b_forget]
    if p_len:
        in_specs.append(pl.BlockSpec((1, B_HEADS, p_len), lambda i: (i, 0, 0)))
        args.append(past_t)
    return pl.pallas_call(
        functools.partial(_logf_cumsum_kernel, p_len=p_len, t_pad=t_pad),
        grid=(b,),
        in_specs=in_specs,
        out_specs=[pl.BlockSpec((1, B_HEADS, t_pad), lambda i: (i, 0, 0)),
                   pl.BlockSpec((1, B_HEADS, p_len + t_pad), lambda i: (i, 0, 0))],
        out_shape=[jax.ShapeDtypeStruct((b, B_HEADS, t_pad), F32),
                   jax.ShapeDtypeStruct((b, B_HEADS, p_len + t_pad), F32)],
        compiler_params=_cparams("arbitrary"),
    )(*args)


def _indexer_kernel(qi_ref, small_ref, klo_ref, khi_ref, mask_ref, qh_ref, wb_ref, keys_ref, *,
                    tq, tkc, n_chunks, l_real, p_len, nq_per_batch, n_sel):
    i = pl.program_id(0)
    row0 = (i % nq_per_batch) * tq
    n_pairs = IDX_HEADS // 2
    sub = tkc // LANES if tkc % LANES == 0 else 0
    for b in range(n_pairs):
        qh_ref[b * tq:(b + 1) * tq, :] = qi_ref[:, b * LANES:(b + 1) * LANES]
    w = small_ref[:, IDX_DIM:IDX_DIM + IDX_HEADS] * (IDX_HEADS ** -0.5)
    w = w * (IDX_DIM ** -0.5)
    wcols = LANES if sub else tkc
    for h in range(IDX_HEADS):
        wb_ref[h] = jnp.broadcast_to(w[:, h:h + 1], (tq, wcols))
    qpos = p_len + row0 + lax.broadcasted_iota(jnp.int32, (tq, 1), 0)
    q_chunk = qpos // CHUNK
    last_allowed = (p_len + row0 + tq - 1) // CHUNK * CHUNK + CHUNK - 1
    n_allowed = jnp.minimum(last_allowed // tkc + 1, n_chunks)

    def allowed_of(c0):
        kpos = c0 + lax.broadcasted_iota(jnp.int32, (tq, tkc), 1)
        return jnp.logical_and(kpos // CHUNK <= q_chunk, kpos < l_real)

    def weighted(s, h):
        r = jnp.maximum(s, 0.0)
        if not sub:
            return r * wb_ref[h]
        wv = wb_ref[h]
        return jnp.concatenate([r[:, u * LANES:(u + 1) * LANES] * wv for u in range(sub)], axis=1)

    for c in range(n_chunks):
        c0 = c * tkc

        @pl.when(c < n_allowed)
        def _():
            q_all = qh_ref[...]
            s_lo = jnp.dot(q_all, klo_ref[0, :, c0:c0 + tkc], preferred_element_type=F32)
            s_hi = jnp.dot(q_all, khi_ref[0, :, c0:c0 + tkc], preferred_element_type=F32)
            score = jnp.zeros((tq, tkc), F32)
            for b in range(n_pairs):
                score = score + weighted(s_lo[b * tq:(b + 1) * tq, :], 2 * b)
                score = score + weighted(s_hi[b * tq:(b + 1) * tq, :], 2 * b + 1)
            score = score + 0.0
            score = jnp.where(allowed_of(c0), score, NEG_INF)
            bits = lax.bitcast_convert_type(score, jnp.int32)
            keys_ref[c] = jnp.where(bits < 0, bits ^ 0x7FFFFFFF, bits)

    def count(pred_of_chunk):
        def body(c, acc):
            ind = jnp.where(pred_of_chunk(keys_ref[c], c), 1.0, 0.0)
            if not sub:
                return acc + ind
            for u in range(sub):
                acc = acc + ind[:, u * LANES:(u + 1) * LANES]
            return acc

        acc = lax.fori_loop(0, n_allowed, body, jnp.zeros((tq, wcols), F32))
        return jnp.sum(acc, axis=1, keepdims=True)

    k_f = float(n_sel)
    t0 = jnp.where(count(lambda kc, c: kc >= 0) >= k_f, 0, INT_MIN).astype(jnp.int32)

    def bisect(it, t):
        cand = t + jnp.left_shift(jnp.int32(1), 30 - it)
        return jnp.where(count(lambda kc, c: kc >= cand) >= k_f, cand, t)

    t = lax.fori_loop(0, 31, bisect, t0)
    cge = count(lambda kc, c: kc >= t)
    excess = jnp.max(jnp.where(t > KEY_NEG_INF, cge, 0.0)) > k_f

    def emit(sel_of_chunk):
        for c in range(n_chunks):
            @pl.when(c < n_allowed)
            def _():
                sel = jnp.logical_and(sel_of_chunk(keys_ref[c], c), allowed_of(c * tkc))
                mask_ref[c] = jnp.where(sel, 0.0, NEG_INF).astype(mask_ref.dtype)

            @pl.when(c >= n_allowed)
            def _():
                mask_ref[c] = jnp.full((tq, tkc), NEG_INF, mask_ref.dtype)

    @pl.when(jnp.logical_not(excess))
    def _():
        emit(lambda kc, c: kc >= t)

    @pl.when(excess)
    def _():
        need = k_f - count(lambda kc, c: kc > t)
        lane_idx = lax.broadcasted_iota(jnp.int32, (tq, tkc), 1)
        m = jnp.zeros((tq, 1), jnp.int32)
        for bit in reversed(range(max(1, (n_chunks * tkc - 1).bit_length()))):
            cand = m + (1 << bit)
            c_lt = count(lambda kc, c: jnp.logical_and(kc == t, c * tkc + lane_idx < cand))
            m = jnp.where(c_lt < need, cand, m)
        emit(lambda kc, c: jnp.logical_or(
            kc > t, jnp.logical_and(kc == t, c * tkc + lane_idx <= m)))


def _indexer_mask(qi, small, k_lo, k_hi, *, rows_per_batch, tq, tkc, l_real, p_len):
    n = qi.shape[0]
    l_pad = k_lo.shape[2]
    n_chunks = l_pad // tkc
    nqb = rows_per_batch // tq
    n_sel = min(TOPK_MAX, l_real // 4)
    wcols = LANES if tkc % LANES == 0 else tkc
    kern = functools.partial(_indexer_kernel, tq=tq, tkc=tkc, n_chunks=n_chunks, l_real=l_real,
                             p_len=p_len, nq_per_batch=nqb, n_sel=n_sel)
    kspec = pl.BlockSpec((1, LANES, l_pad), lambda i: (i // nqb, 0, 0))
    return pl.pallas_call(
        kern,
        grid=(n // tq,),
        in_specs=[pl.BlockSpec((tq, IDX_WIDTH), lambda i: (i, 0)),
                  pl.BlockSpec((tq, SMALL_WIDTH), lambda i: (i, 0)),
                  kspec, kspec],
        out_specs=pl.BlockSpec((n_chunks, tq, tkc), lambda i: (0, i, 0)),
        out_shape=jax.ShapeDtypeStruct((n_chunks, n, tkc), BF16),
        scratch_shapes=[pltpu.VMEM((IDX_HEADS // 2 * tq, LANES), BF16),
                        pltpu.VMEM((IDX_HEADS, tq, wcols), F32),
                        pltpu.VMEM((n_chunks, tq, tkc), jnp.int32)],
        compiler_params=_cparams("arbitrary"),
    )(qi, small, k_lo, k_hi)


def _softmax_step(s, carry, v_tile):
    m, l, acc = carry
    m_new = jnp.maximum(m, jnp.max(s, axis=1, keepdims=True))
    m_safe = jnp.where(m_new == NEG_INF, 0.0, m_new)
    alpha = jnp.exp2(m - m_safe)
    p = jnp.exp2(s - m_safe)
    l = alpha * l + jnp.sum(p, axis=1, keepdims=True)
    acc = alpha * acc + jnp.dot(p.astype(BF16), v_tile, preferred_element_type=F32)
    return m_new, l, acc


def _softmax_init(rows):
    return (jnp.full((rows, 1), NEG_INF, F32), jnp.zeros((rows, 1), F32),
            jnp.zeros((rows, HEAD_DIM), F32))


def _qk(q, k_tile):
    return lax.dot_general(q, k_tile, (((1,), (1,)), ((), ())), preferred_element_type=F32)


def _fox_prompt_kernel(q_ref, k_ref, v_ref, fk_ref, o_ref, *, t_tile):
    i = pl.program_id(1)
    q = q_ref[...]

    def tile(j, carry, diagonal):
        k0 = pl.multiple_of(j * t_tile, t_tile)
        s = _qk(q, k_ref[pl.ds(k0, t_tile), :]) - fk_ref[j] * LOG2E
        if diagonal:
            row = lax.broadcasted_iota(jnp.int32, (t_tile, t_tile), 0)
            col = lax.broadcasted_iota(jnp.int32, (t_tile, t_tile), 1)
            s = jnp.where(col <= row, s, NEG_INF)
        return _softmax_step(s, carry, v_ref[pl.ds(k0, t_tile), :])

    carry = lax.fori_loop(0, i, lambda j, c: tile(j, c, False), _softmax_init(t_tile))
    _, l, acc = tile(i, carry, True)
    o_ref[...] = (acc / l).astype(o_ref.dtype)


def _fox_prompt(q, k, v, fk_tiles, *, t_tile):
    t, width = q.shape
    heads = width // HEAD_DIM
    nt = t // t_tile
    return pl.pallas_call(
        functools.partial(_fox_prompt_kernel, t_tile=t_tile),
        grid=(heads, nt),
        in_specs=[pl.BlockSpec((t_tile, HEAD_DIM), lambda h, i: (i, h)),
                  pl.BlockSpec((t, HEAD_DIM), lambda h, i: (0, h)),
                  pl.BlockSpec((t, HEAD_DIM), lambda h, i: (0, h)),
                  pl.BlockSpec((nt, 1, t_tile), lambda h, i: (h, 0, 0))],
        out_specs=pl.BlockSpec((t_tile, HEAD_DIM), lambda h, i: (i, h)),
        out_shape=jax.ShapeDtypeStruct((t, width), BF16),
        compiler_params=_cparams("arbitrary", "arbitrary"),
    )(q, k, v, fk_tiles)


def _dsa_prompt_kernel(q_ref, k_ref, v_ref, mask_ref, bias_ref, o_ref, *, t_tile, grp):
    i = pl.program_id(1)
    q = jnp.concatenate([q_ref[:, j * HEAD_DIM:(j + 1) * HEAD_DIM] for j in range(grp)], axis=0)

    def tile(j, carry, bias_cls):
        k0 = pl.multiple_of(j * t_tile, t_tile)
        s = _qk(q, k_ref[pl.ds(k0, t_tile), :]).reshape(grp, t_tile, t_tile)
        s = s + mask_ref[j].astype(F32)[None]
        if bias_cls is not None:
            s = s + bias_ref[bias_cls]
        return _softmax_step(s.reshape(grp * t_tile, t_tile), carry, v_ref[pl.ds(k0, t_tile), :])

    carry = lax.fori_loop(0, jnp.maximum(i - 1, 0), lambda j, c: tile(j, c, None),
                          _softmax_init(grp * t_tile))
    carry = lax.fori_loop(jnp.maximum(i - 1, 0), i, lambda j, c: tile(j, c, 1), carry)
    _, l, acc = tile(i, carry, 0)
    out = acc / l
    for j in range(grp):
        o_ref[:, j * HEAD_DIM:(j + 1) * HEAD_DIM] = out[j * t_tile:(j + 1) * t_tile].astype(o_ref.dtype)


def _dsa_prompt(q, k, v, mask, bias, *, t_tile):
    t, width = q.shape
    kv_heads = k.shape[1] // HEAD_DIM
    grp = width // HEAD_DIM // kv_heads
    nt = t // t_tile
    return pl.pallas_call(
        functools.partial(_dsa_prompt_kernel, t_tile=t_tile, grp=grp),
        grid=(kv_heads, nt),
        in_specs=[pl.BlockSpec((t_tile, grp * HEAD_DIM), lambda g, i: (i, g)),
                  pl.BlockSpec((t, HEAD_DIM), lambda g, i: (0, g)),
                  pl.BlockSpec((t, HEAD_DIM), lambda g, i: (0, g)),
                  pl.BlockSpec((nt, t_tile, t_tile), lambda g, i: (0, i, 0)),
                  pl.BlockSpec((2, grp, t_tile, t_tile), lambda g, i: (0, g, 0, 0))],
        out_specs=pl.BlockSpec((t_tile, grp * HEAD_DIM), lambda g, i: (i, g)),
        out_shape=jax.ShapeDtypeStruct((t, width), BF16),
        compiler_params=_cparams("arbitrary", "arbitrary"),
    )(q, k, v, mask, bias)


def _attend(q_ref, k, v, bias_of_head, m_ref, l_ref, acc_ref, *, tq, kv_heads, grp):
    for g in range(kv_heads):
        kg = k[:, g * HEAD_DIM:(g + 1) * HEAD_DIM].astype(BF16)
        vg = v[:, g * HEAD_DIM:(g + 1) * HEAD_DIM].astype(BF16)
        qs = [q_ref[:, (g * grp + j) * HEAD_DIM:(g * grp + j + 1) * HEAD_DIM] for j in range(grp)]
        qg = qs[0] if grp == 1 else jnp.concatenate(qs, axis=0)
        s_all = _qk(qg, kg)
        for j in range(grp):
            h = g * grp + j
            s = s_all[j * tq:(j + 1) * tq, :] + bias_of_head(h)
            m_ref[h], l_ref[h], acc_ref[h] = _softmax_step(s, (m_ref[h], l_ref[h], acc_ref[h]), vg)


def _flash_kernel(*refs, mode, n_a, tq, tk_a, tk_b, p_len, kv_heads, grp):
    heads = kv_heads * grp
    it = iter(refs)
    q_ref = next(it)
    ka_ref, va_ref, kb_ref, vb_ref = next(it), next(it), next(it), next(it)
    if mode == "dsa":
        maska_ref, biasa_ref, maskb_ref, biasb_ref = next(it), next(it), next(it), next(it)
    else:
        fq_ref, fka_ref, fkb_ref = next(it), next(it), next(it)
    o_ref, m_ref, l_ref, acc_ref = next(it), next(it), next(it), next(it)
    j = pl.program_id(1)

    @pl.when(j == 0)
    def _():
        m_ref[...] = jnp.full(m_ref.shape, NEG_INF, F32)
        l_ref[...] = jnp.zeros(l_ref.shape, F32)
        acc_ref[...] = jnp.zeros(acc_ref.shape, F32)

    common = dict(tq=tq, kv_heads=kv_heads, grp=grp)

    @pl.when(j < n_a)
    def _():
        if mode == "dsa":
            madd = maska_ref[...].astype(F32)
            bias = lambda h: biasa_ref[0, h] + madd
        else:
            bias = lambda h: (fq_ref[:, h:h + 1] - fka_ref[0, h:h + 1, :]) * LOG2E
        _attend(q_ref, ka_ref[0], va_ref[0], bias, m_ref, l_ref, acc_ref, **common)

    @pl.when(j == n_a)
    def _():
        if mode == "dsa":
            madd = maskb_ref[...].astype(F32)
            bias = lambda h: biasb_ref[0, h] + madd
        else:
            row = lax.broadcasted_iota(jnp.int32, (tq, tk_b), 0)
            col = lax.broadcasted_iota(jnp.int32, (tq, tk_b), 1)
            causal = jnp.where(col <= row, 0.0, NEG_INF)
            bias = lambda h: (fq_ref[:, h:h + 1] - fkb_ref[0, h:h + 1, :tk_b]) * LOG2E + causal
        _attend(q_ref, kb_ref[...], vb_ref[...], bias, m_ref, l_ref, acc_ref, **common)
        for h in range(heads):
            o_ref[:, h * HEAD_DIM:(h + 1) * HEAD_DIM] = (acc_ref[h] / l_ref[h]).astype(o_ref.dtype)


def _flash_sample(mode, q, k_new, v_new, k_past, v_past, *, tk_a, mask_a=None, bias_a=None,
                  mask_b=None, bias_b=None, fq=None, fk_t=None):
    n, width = q.shape
    b, p_len, kv_w = k_past.shape
    tq = n // b
    heads = width // HEAD_DIM
    kv_heads = kv_w // HEAD_DIM
    grp = heads // kv_heads
    n_a = p_len // tk_a
    a_blk = lambda j: jnp.minimum(j, n_a - 1)

    in_specs = [pl.BlockSpec((tq, width), lambda g, j: (g, 0))]
    spec = pl.BlockSpec((1, tk_a, kv_w), lambda g, j: (g, a_blk(j), 0))
    in_specs += [spec, spec]
    spec = pl.BlockSpec((tq, kv_w), lambda g, j: (g, 0))
    in_specs += [spec, spec]
    args = [q, k_past, v_past, k_new, v_new]
    if mode == "dsa":
        in_specs += [pl.BlockSpec((tq, tk_a), lambda g, j: (g, a_blk(j))),
                     pl.BlockSpec((1, heads, tq, tk_a), lambda g, j: (a_blk(j), 0, 0, 0)),
                     pl.BlockSpec((tq, tq), lambda g, j: (g, 0)),
                     pl.BlockSpec((1, heads, tq, tq), lambda g, j: (0, 0, 0, 0))]
        args += [mask_a, bias_a, mask_b, bias_b]
    else:
        in_specs += [pl.BlockSpec((tq, heads), lambda g, j: (g, 0)),
                     pl.BlockSpec((1, heads, tk_a), lambda g, j: (g, 0, a_blk(j))),
                     pl.BlockSpec((1, heads, LANES), lambda g, j: (g, 0, p_len // LANES))]
        args += [fq, fk_t, fk_t]

    kern = functools.partial(_flash_kernel, mode=mode, n_a=n_a, tq=tq, tk_a=tk_a, tk_b=tq,
                             p_len=p_len, kv_heads=kv_heads, grp=grp)
    return pl.pallas_call(
        kern,
        grid=(b, n_a + 1),
        in_specs=in_specs,
        out_specs=pl.BlockSpec((tq, width), lambda g, j: (g, 0)),
        out_shape=jax.ShapeDtypeStruct((n, width), BF16),
        scratch_shapes=[pltpu.VMEM((heads, tq, 1), F32),
                        pltpu.VMEM((heads, tq, 1), F32),
                        pltpu.VMEM((heads, tq, HEAD_DIM), F32)],
        compiler_params=_cparams("arbitrary", "arbitrary"),
    )(*args)


def _router_kernel(lt_ref, eidx_ref, gate_ref):
    row = lambda r: lt_ref[r:r + 1, :]
    g = [row(r) for r in range(N_GROUPS)]
    gmax = functools.reduce(jnp.maximum, g)
    gsel = jnp.full(gmax.shape, N_GROUPS - 1, jnp.int32)
    for r in reversed(range(N_GROUPS - 1)):
        gsel = jnp.where(g[r] == gmax, r, gsel)
    denom = functools.reduce(lambda a, b: a + b, [jnp.exp(x - gmax) for x in g])
    g_prob = 1.0 / denom
    e_in = []
    for kk in range(EXPERTS_PER_GROUP):
        v = row(N_GROUPS + (N_GROUPS - 1) * EXPERTS_PER_GROUP + kk)
        for r in reversed(range(N_GROUPS - 1)):
            v = jnp.where(gsel == r, row(N_GROUPS + r * EXPERTS_PER_GROUP + kk), v)
        e_in.append(v)
    v1 = functools.reduce(jnp.maximum, e_in)
    i1 = jnp.full(v1.shape, EXPERTS_PER_GROUP - 1, jnp.int32)
    for kk in reversed(range(EXPERTS_PER_GROUP - 1)):
        i1 = jnp.where(e_in[kk] == v1, kk, i1)
    rest = [jnp.where(i1 == kk, NEG_INF, e_in[kk]) for kk in range(EXPERTS_PER_GROUP)]
    v2 = functools.reduce(jnp.maximum, rest)
    i2 = jnp.full(v2.shape, EXPERTS_PER_GROUP - 1, jnp.int32)
    for kk in reversed(range(EXPERTS_PER_GROUP - 1)):
        i2 = jnp.where(jnp.logical_and(rest[kk] == v2, i1 != kk), kk, i2)
    e2 = jnp.exp(v2 - v1)
    inv = 1.0 / (1.0 + e2)
    eidx_ref[0:1, :] = gsel * EXPERTS_PER_GROUP + i1
    eidx_ref[1:2, :] = gsel * EXPERTS_PER_GROUP + i2
    gate_ref[0:1, :] = inv * g_prob
    gate_ref[1:2, :] = (e2 * inv) * g_prob


def _router(logits_t):
    n = logits_t.shape[1]
    full = lambda r: pl.BlockSpec((r, n), lambda: (0, 0))
    return pl.pallas_call(
        _router_kernel,
        in_specs=[full(logits_t.shape[0])],
        out_specs=[full(TOP_K_IN_GROUP), full(TOP_K_IN_GROUP)],
        out_shape=[jax.ShapeDtypeStruct((TOP_K_IN_GROUP, n), jnp.int32),
                   jax.ShapeDtypeStruct((TOP_K_IN_GROUP, n), F32)],
        compiler_params=pltpu.CompilerParams(vmem_limit_bytes=VMEM_LIMIT),
    )(logits_t)


EXPERT_CHUNK = 256
EXPERT_CHUNKS_PER_BLOCK = 4
EXPERT_ROWS = EXPERT_CHUNK * EXPERT_CHUNKS_PER_BLOCK
EXPERT_F_TILE = 256
EXPERT_N_TILE = 512


def _experts_kernel(sb_e_ref, sb_row0_ref, sb_nch_ref, sb_blk_ref, x_hbm, w1_ref, w3_ref, w2_ref,
                    o_ref, x_ref, act_ref, sem, *, nf):
    del sb_e_ref, sb_blk_ref
    s = pl.program_id(0)
    t = pl.program_id(1)
    n_ch = sb_nch_ref[s]

    @pl.when(jnp.logical_and(t == 0, n_ch > 0))
    def _():
        row0 = pl.multiple_of(sb_row0_ref[s], EXPERT_CHUNK)
        cp = pltpu.make_async_copy(x_hbm.at[pl.ds(row0, EXPERT_ROWS)], x_ref, sem)
        cp.start()
        cp.wait()

    for nc in range(1, EXPERT_CHUNKS_PER_BLOCK + 1):
        rows = nc * EXPERT_CHUNK

        @pl.when(jnp.logical_and(t < nf, n_ch == nc))
        def _():
            xc = x_ref[0:rows, :]
            a = jnp.dot(xc, w1_ref[0].astype(BF16), preferred_element_type=F32)
            b = jnp.dot(xc, w3_ref[0].astype(BF16), preferred_element_type=F32)
            act_ref[t, 0:rows, :] = (a * jax.nn.sigmoid(a) * b).astype(BF16)

        @pl.when(jnp.logical_and(t >= nf, n_ch == nc))
        def _():
            y = None
            for f in range(nf):
                part = jnp.dot(act_ref[f, 0:rows, :],
                               w2_ref[0, f * EXPERT_F_TILE:(f + 1) * EXPERT_F_TILE, :].astype(BF16),
                               preferred_element_type=F32)
                y = part if y is None else y + part
            o_ref[0:rows, :] = y


def _experts(xs, sb_expert, sb_row0, sb_nch, sb_blk, w1, w3, w2, n_sb):
    d = xs.shape[1]
    d_e = w1.shape[2]
    nf = d_e // EXPERT_F_TILE
    nn = d // EXPERT_N_TILE
    r = EXPERT_ROWS
    grid_spec = pltpu.PrefetchScalarGridSpec(
        num_scalar_prefetch=4,
        grid=(n_sb, nf + nn),
        in_specs=[
            pl.BlockSpec(memory_space=pl.ANY),
            pl.BlockSpec((1, d, EXPERT_F_TILE),
                         lambda s, t, e, r0, nch, blk: (e[s], 0, jnp.minimum(t, nf - 1))),
            pl.BlockSpec((1, d, EXPERT_F_TILE),
                         lambda s, t, e, r0, nch, blk: (e[s], 0, jnp.minimum(t, nf - 1))),
            pl.BlockSpec((1, d_e, EXPERT_N_TILE),
                         lambda s, t, e, r0, nch, blk: (e[s], 0, jnp.maximum(t - nf, 0))),
        ],
        out_specs=pl.BlockSpec(
            (r, EXPERT_N_TILE),
            lambda s, t, e, r0, nch, blk: (blk[s], jnp.where(nch[s] > 0, jnp.maximum(t - nf, 0), 0))),
        scratch_shapes=[pltpu.VMEM((r, d), BF16),
                        pltpu.VMEM((nf, r, EXPERT_F_TILE), BF16),
                        pltpu.SemaphoreType.DMA(())],
    )
    return pl.pallas_call(
        functools.partial(_experts_kernel, nf=nf),
        grid_spec=grid_spec,
        out_shape=jax.ShapeDtypeStruct(((n_sb + 1) * r, d), F32),
        compiler_params=_cparams("arbitrary", "arbitrary"),
    )(sb_expert, sb_row0, sb_nch, sb_blk, xs, w1, w3, w2)


def _dispatch(expert, n_sb, n_x_rows):
    n = expert.shape[0]
    a = n * TOP_K_IN_GROUP
    ch, r = EXPERT_CHUNK, EXPERT_ROWS
    flat_e = expert.reshape(a)
    flat_tok = jnp.arange(a, dtype=jnp.int32) // TOP_K_IN_GROUP
    order = jnp.argsort(flat_e)
    se = flat_e[order]
    counts = jnp.bincount(flat_e, length=N_EXPERTS).astype(jnp.int32)
    seg_start = jnp.cumsum(counts) - counts
    rank = jnp.arange(a, dtype=jnp.int32) - seg_start[se]
    nch_e = (counts + ch - 1) // ch
    xch_end = jnp.cumsum(nch_e)
    x_start = (xch_end - nch_e) * ch
    nsb_e = (counts + r - 1) // r
    sb_end = jnp.cumsum(nsb_e)
    sb_start = sb_end - nsb_e
    x_row_tok = jnp.zeros((n_x_rows,), jnp.int32).at[x_start[se] + rank].set(flat_tok[order])
    n_live_x_blocks = (xch_end[-1] * ch + GATHER_ROWS - 1) // GATHER_ROWS
    y_row = jnp.zeros((a,), jnp.int32).at[order].set(sb_start[se] * r + rank)
    y_row = y_row.reshape(n, TOP_K_IN_GROUP)
    total = sb_end[-1]
    s_idx = jnp.arange(n_sb, dtype=jnp.int32)
    e_of_sb = jnp.minimum(jnp.searchsorted(sb_end, s_idx, side="right"), N_EXPERTS - 1).astype(jnp.int32)
    k_in_e = s_idx - sb_start[e_of_sb]
    rows_of_sb = jnp.clip(counts[e_of_sb] - k_in_e * r, 0, r)
    live = s_idx < total
    last_e = e_of_sb[jnp.maximum(total - 1, 0)]
    sb_expert = jnp.where(live, e_of_sb, last_e).astype(jnp.int32)
    sb_nch = jnp.where(live, (rows_of_sb + ch - 1) // ch, 0).astype(jnp.int32)
    sb_row0 = jnp.where(live, x_start[e_of_sb] + k_in_e * r, 0).astype(jnp.int32)
    sb_blk = jnp.where(live, s_idx, n_sb).astype(jnp.int32)
    return (x_row_tok, n_live_x_blocks.astype(jnp.int32).reshape(1), y_row, sb_expert, sb_row0,
            sb_nch, sb_blk)


def _t5_bucket(rel):
    half = N_BUCKETS // 2
    max_exact = half // 2
    bucket = jnp.where(rel > 0, half, 0).astype(jnp.int32)
    n = jnp.abs(rel)
    large = max_exact + (jnp.log(jnp.maximum(n, 1).astype(jnp.float32) / max_exact)
                         / math.log(MAX_DISTANCE / max_exact) * (half - max_exact)).astype(jnp.int32)
    large = jnp.minimum(large, half - 1)
    return bucket + jnp.where(n < max_exact, n, large).astype(jnp.int32)


def _bias_tiles(rel_bias, rel0_list, tq, tk):
    rel0 = jnp.asarray(rel0_list, jnp.int32)[:, None, None]
    rel = rel0 + jnp.arange(tk, dtype=jnp.int32)[None, None, :] - jnp.arange(tq, dtype=jnp.int32)[None, :, None]
    return jnp.transpose(rel_bias[_t5_bucket(rel)].astype(F32), (0, 3, 1, 2))


def _pick(n, candidates):
    for c in candidates:
        if n % c == 0:
            return c
    return n


PROMPT_TILE = 256


def _mixer_half(x, mods, past, w_in_packed, b_forget, w_out, rel_bias, g_mix, g_ffn):
    b, t, d = x.shape
    n = b * t
    sh1, sc1, gt1, sh2, sc2, _ = mods
    xf = x.reshape(n, d)
    tt = _pick(t, (256, 128, 64, 32, 16))
    h = _norm_mod(xf, g_mix, sc1, sh1, rows_per_batch=t, tt=tt, out_dtype=BF16)

    tm = _pick(n, (512, 256, 128))

    def proj(col_off, width, tn, **kw):
        return _matmul([h], w_in_packed, col_off=col_off, n=width, tm=tm, tn=tn, **kw)

    off = 0
    qa = proj(off, A_WIDTH, 1024, out_dtypes=(BF16,), out_scale=Q_SCALE); off += A_WIDTH
    ka, ka16 = proj(off, A_KV_WIDTH, 512, out_dtypes=(F32, BF16)); off += A_KV_WIDTH
    va, va16 = proj(off, A_KV_WIDTH, 512, out_dtypes=(F32, BF16)); off += A_KV_WIDTH
    qi = proj(off, IDX_WIDTH, 1024, out_dtypes=(BF16,)); off += IDX_WIDTH
    qb = proj(off, B_WIDTH, 1024, out_dtypes=(BF16,), out_scale=Q_SCALE); off += B_WIDTH
    kb, kb16 = proj(off, B_WIDTH, 1024, out_dtypes=(F32, BF16)); off += B_WIDTH
    vb, vb16 = proj(off, B_WIDTH, 1024, out_dtypes=(F32, BF16)); off += B_WIDTH
    small = proj(off, SMALL_WIDTH, SMALL_WIDTH)

    ki = small[:, :IDX_DIM]
    fl = small[:, IDX_DIM + IDX_HEADS:IDX_DIM + IDX_HEADS + B_HEADS]

    p_len = 0 if past is None else past[0].shape[1]
    t_pad = -(-t // LANES) * LANES
    l_real = p_len + t
    l_pad = p_len + t_pad

    fl_t = jnp.transpose(fl.reshape(b, t, B_HEADS), (0, 2, 1))
    fl_t = jnp.pad(fl_t, ((0, 0), (0, 0), (0, t_pad - t)))
    past_logf_t = None if past is None else jnp.transpose(past[5], (0, 2, 1))
    logf_t, fcum_t = _logf_cumsum(fl_t, b_forget.reshape(B_HEADS, 1), past_logf_t)
    logf = jnp.transpose(logf_t[:, :, :t], (0, 2, 1))

    ki_b = ki.reshape(b, t, IDX_DIM)
    if past is not None:
        ki_b = jnp.concatenate([past[2], ki_b], axis=1)
    ki_t = jnp.transpose(jnp.pad(ki_b, ((0, 0), (0, l_pad - l_real), (0, 0))), (0, 2, 1)).astype(BF16)
    zeros = jnp.zeros_like(ki_t)
    k_lo = jnp.concatenate([ki_t, zeros], axis=1)
    k_hi = jnp.concatenate([zeros, ki_t], axis=1)

    if past is None:
        tile = PROMPT_TILE
        assert b == 1 and t % tile == 0 and tile >= MAX_DISTANCE and tile % CHUNK == 0
        mask = _indexer_mask(qi, small, k_lo, k_hi, rows_per_batch=t, tq=tile, tkc=tile,
                             l_real=l_real, p_len=0)
        bias = _bias_tiles(rel_bias, [0, -tile], tile, tile)
        far = rel_bias[_t5_bucket(jnp.int32(-MAX_DISTANCE))].astype(F32)
        bias = (bias - far[None, :, None, None]) * LOG2E
        out_a = _dsa_prompt(qa, ka16, va16, mask, bias, t_tile=tile)
        fk_tiles = fcum_t.reshape(B_HEADS * (t // tile), 1, tile)
        out_b = _fox_prompt(qb, kb16, vb16, fk_tiles, t_tile=tile)
    else:
        tk_a = _pick(p_len, (512, 256, 128))
        mask = _indexer_mask(qi, small, k_lo, k_hi, rows_per_batch=t, tq=t, tkc=l_pad,
                             l_real=l_real, p_len=p_len).reshape(n, l_pad)
        n_a = p_len // tk_a
        bias_a = _bias_tiles(rel_bias, [c * tk_a - p_len for c in range(n_a)], t, tk_a) * LOG2E
        bias_b = _bias_tiles(rel_bias, [0], t, t) * LOG2E
        cache = lambda a: a.reshape(b, p_len, -1)
        out_a = _flash_sample("dsa", qa, ka16, va16, cache(past[0]), cache(past[1]), tk_a=tk_a,
                              mask_a=mask[:, :p_len], bias_a=bias_a,
                              mask_b=mask[:, p_len:p_len + t], bias_b=bias_b)
        fq = jnp.transpose(fcum_t[:, :, p_len:p_len + t], (0, 2, 1)).reshape(n, B_HEADS)
        out_b = _flash_sample("fox", qb, kb16, vb16, cache(past[3]), cache(past[4]), tk_a=tk_a,
                              fq=fq, fk_t=fcum_t)

    gate1 = gt1.reshape(1, d) if b == 1 else jnp.repeat(gt1.reshape(b, d), t, axis=0)
    x1 = _matmul([out_a, out_b], w_out, tm=tm, tn=512, res=xf, gate=gate1)
    h2 = _norm_mod(x1, g_ffn, sc2, sh2, rows_per_batch=t, tt=tt, out_dtype=F32)

    state = (ka.reshape(1, b, t, A_KV_HEADS, HEAD_DIM), va.reshape(1, b, t, A_KV_HEADS, HEAD_DIM),
             ki.reshape(1, b, t, IDX_DIM),
             kb.reshape(1, b, t, B_HEADS, HEAD_DIM), vb.reshape(1, b, t, B_HEADS, HEAD_DIM),
             logf.reshape(1, b, t, B_HEADS))
    return x1, h2, state


def kernel(x_prompt, x_sample, cache_a_k, cache_a_v, cache_idx_k, cache_b_k, cache_b_v, cache_b_logf, c_prompt, c_sample, w_in, b_forget, w_out, rel_bias, w_mod, b_mod, g_mix, g_ffn, w_router_group, b_router_group, w_router_expert, b_router_expert, w_gate, w_up, w_down, g_final):
    assert w_in.shape[0] == 1, "single-layer trunk"
    d = x_prompt.shape[-1]
    bp, tp, _ = x_prompt.shape
    bs, ts, _ = x_sample.shape
    n_p, n_s = bp * tp, bs * ts
    layer0 = lambda a: a.reshape(a.shape[1:])

    n_c = bp + bs
    c_rows = -(-n_c // 8) * 8
    c_all = jnp.pad(jnp.concatenate([c_prompt, c_sample], axis=0), ((0, c_rows - n_c), (0, 0)))
    mod = _matmul([c_all], layer0(w_mod), tm=c_rows, tn=512, silu_a=True, bias=b_mod.reshape(1, -1))

    def mods_of(lo, hi):
        m6 = mod[lo:hi].reshape(hi - lo, 6, 1, d)
        return [m6[:, i] for i in range(6)]

    mods_p, mods_s = mods_of(0, bp), mods_of(bp, n_c)

    w = layer0(w_in)
    o1 = A_WIDTH + 2 * A_KV_WIDTH + IDX_WIDTH
    o2 = o1 + IDX_DIM + IDX_HEADS
    o3 = o2 + 3 * B_WIDTH
    tail_pad = SMALL_WIDTH - (IDX_DIM + IDX_HEADS + B_HEADS)
    w_packed = jnp.concatenate(
        [w[:, :o1], w[:, o2:o3], w[:, o1:o2], w[:, o3:], jnp.zeros((d, tail_pad), w.dtype)],
        axis=1).astype(BF16)

    g_mix2, g_ffn2, g_fin2 = g_mix.reshape(1, d), g_ffn.reshape(1, d), g_final.reshape(1, d)
    past = tuple(layer0(c) for c in (cache_a_k, cache_a_v, cache_idx_k, cache_b_k, cache_b_v,
                                      cache_b_logf))
    b_f, w_o = layer0(b_forget), layer0(w_out)
    x1_p, h2_p, st_p = _mixer_half(x_prompt, mods_p, None, w_packed, b_f, w_o, rel_bias, g_mix2, g_ffn2)
    x1_s, h2_s, st_s = _mixer_half(x_sample, mods_s, past, w_packed, b_f, w_o, rel_bias, g_mix2, g_ffn2)

    n_tok = n_p + n_s
    h2 = jnp.concatenate([h2_p, h2_s], axis=0)
    w_r = jnp.concatenate([layer0(w_router_group), layer0(w_router_expert)], axis=1)
    n_cls = w_r.shape[1]
    w_r = jnp.pad(w_r, ((0, 0), (0, LANES - n_cls)))
    b_r = jnp.pad(jnp.concatenate([layer0(b_router_group), layer0(b_router_expert)]), (0, LANES - n_cls))
    logits = _matmul([h2], w_r, tm=_pick(n_tok, (512, 256, 128)), tn=LANES, bias=b_r.reshape(1, LANES))
    eidx_t, gate_t = _router(jnp.transpose(logits))
    expert = jnp.transpose(eidx_t)
    gates = jnp.transpose(gate_t)

    n_assign = n_tok * TOP_K_IN_GROUP
    assert n_assign % GATHER_ROWS == 0
    n_sb = n_assign // EXPERT_ROWS + N_EXPERTS
    n_x_rows = -(-n_assign // EXPERT_CHUNK) * EXPERT_CHUNK + N_EXPERTS * EXPERT_CHUNK + EXPERT_ROWS
    x_row_tok, n_live_x, y_row, sb_expert, sb_row0, sb_nch, sb_blk = _dispatch(expert, n_sb, n_x_rows)
    xs = _gather_rows(h2, x_row_tok, n_live_x, BF16)
    yb = _experts(xs, sb_expert, sb_row0, sb_nch, sb_blk, layer0(w_gate), layer0(w_up),
                  layer0(w_down), n_sb)
    y_idx = jnp.concatenate([y_row[:, 0], y_row[:, 1]])
    y01 = _gather_rows(yb, y_idx, jnp.full((1,), n_assign // GATHER_ROWS, jnp.int32), F32)

    def finish(x1, lo, gt2, t):
        tt = _pick(t, (128, 64, 32, 16))
        return _combine(x1, y01, gates[lo:lo + x1.shape[0]], gt2, g_fin2, row_off=lo, n_tok=n_tok,
                        rows_per_batch=t, tt=tt)

    y_p = finish(x1_p, 0, mods_p[5], tp).reshape(bp, tp, d)
    y_s = finish(x1_s, n_p, mods_s[5], ts).reshape(bs, ts, d)
    return (y_p, y_s) + st_p + st_s
```

```python
import functools
import math

import numpy as np
import jax
import jax.numpy as jnp
from jax import lax
from jax.experimental import pallas as pl
from jax.experimental.pallas import tpu as pltpu

CHUNK = 64
HEAD_DIM = 128
A_HEADS = 16
A_KV_HEADS = 4
IDX_HEADS = 32
IDX_DIM = 64
TOPK_MAX = 256
B_HEADS = 16
N_BUCKETS = 32
MAX_DISTANCE = 128
N_GROUPS = 4
EXPERTS_PER_GROUP = 8
N_EXPERTS = N_GROUPS * EXPERTS_PER_GROUP
TOP_K_IN_GROUP = 2
EPS = 1e-6

A_WIDTH = A_HEADS * HEAD_DIM
A_KV_WIDTH = A_KV_HEADS * HEAD_DIM
B_WIDTH = B_HEADS * HEAD_DIM
IDX_WIDTH = IDX_HEADS * IDX_DIM
SMALL_WIDTH = 128

LANES = 128
V7X_VMEM_BYTES = 64 * 1024 * 1024
VMEM_LIMIT = 56 * 1024 * 1024

BF16 = jnp.bfloat16
F32 = jnp.float32
NEG_INF = float("-inf")
INT_MIN = -(2 ** 31)
KEY_NEG_INF = int(np.int32(np.float32(-np.inf).view(np.int32)) ^ np.int32(0x7FFFFFFF))
LOG2E = math.log2(math.e)
Q_SCALE = HEAD_DIM ** -0.5 * LOG2E


def _cparams(*sem):
    return pltpu.CompilerParams(dimension_semantics=sem, vmem_limit_bytes=VMEM_LIMIT)


def _matmul_kernel(*refs, n_a, silu_a, epilogue, out_scale, n_out):
    a_refs = refs[:n_a]
    w_ref = refs[n_a]
    rest = refs[n_a + 1:]
    o_refs = rest[len(rest) - n_out:]
    acc = None
    k0 = 0
    for a_ref in a_refs:
        a = a_ref[...]
        if silu_a:
            a = a * jax.nn.sigmoid(a)
        kk = a.shape[1]
        part = jnp.dot(a.astype(BF16), w_ref[k0:k0 + kk, :].astype(BF16),
                       preferred_element_type=F32)
        acc = part if acc is None else acc + part
        k0 += kk
    if epilogue == "bias":
        acc = acc + rest[0][...]
    elif epilogue == "gated_residual":
        acc = rest[0][...] + rest[1][...] * acc
    if out_scale is not None:
        acc = acc * out_scale
    for o_ref in o_refs:
        o_ref[...] = acc.astype(o_ref.dtype)


def _matmul(a_list, w, *, col_off=0, n=None, tm, tn, silu_a=False, bias=None, res=None,
            gate=None, out_dtypes=(F32,), out_scale=None):
    m = a_list[0].shape[0]
    k = w.shape[0]
    n = w.shape[1] - col_off if n is None else n
    assert m % tm == 0 and n % tn == 0 and col_off % tn == 0
    assert sum(a.shape[1] for a in a_list) == k
    cb = col_off // tn
    in_specs = [pl.BlockSpec((tm, a.shape[1]), lambda j, i: (i, 0)) for a in a_list]
    in_specs.append(pl.BlockSpec((k, tn), lambda j, i: (0, j + cb)))
    args = list(a_list) + [w]
    if bias is not None:
        epilogue = "bias"
        in_specs.append(pl.BlockSpec((1, tn), lambda j, i: (0, j)))
        args.append(bias)
    elif res is not None:
        epilogue = "gated_residual"
        in_specs.append(pl.BlockSpec((tm, tn), lambda j, i: (i, j)))
        args.append(res)
        if gate.shape[0] == 1:
            in_specs.append(pl.BlockSpec((1, tn), lambda j, i: (0, j)))
        else:
            in_specs.append(pl.BlockSpec((tm, tn), lambda j, i: (i, j)))
        args.append(gate)
    else:
        epilogue = None
    outs = pl.pallas_call(
        functools.partial(_matmul_kernel, n_a=len(a_list), silu_a=silu_a, epilogue=epilogue,
                          out_scale=out_scale, n_out=len(out_dtypes)),
        grid=(n // tn, m // tm),
        in_specs=in_specs,
        out_specs=[pl.BlockSpec((tm, tn), lambda j, i: (i, j)) for _ in out_dtypes],
        out_shape=[jax.ShapeDtypeStruct((m, n), dt) for dt in out_dtypes],
        compiler_params=_cparams("arbitrary", "arbitrary"),
    )(*args)
    return outs[0] if len(out_dtypes) == 1 else outs


def _rms(x, g):
    return x * lax.rsqrt(jnp.mean(x * x, axis=-1, keepdims=True) + EPS) * g


def _norm_mod_kernel(x_ref, g_ref, sc_ref, sh_ref, o_ref):
    y = _rms(x_ref[...], g_ref[...])
    o_ref[...] = (y * (1.0 + sc_ref[0]) + sh_ref[0]).astype(o_ref.dtype)


def _norm_mod(x, g, sc, sh, *, rows_per_batch, tt, out_dtype):
    n, d = x.shape
    bpb = rows_per_batch // tt
    return pl.pallas_call(
        _norm_mod_kernel,
        grid=(n // tt,),
        in_specs=[pl.BlockSpec((tt, d), lambda i: (i, 0)),
                  pl.BlockSpec((1, d), lambda i: (0, 0)),
                  pl.BlockSpec((1, 1, d), lambda i: (i // bpb, 0, 0)),
                  pl.BlockSpec((1, 1, d), lambda i: (i // bpb, 0, 0))],
        out_specs=pl.BlockSpec((tt, d), lambda i: (i, 0)),
        out_shape=jax.ShapeDtypeStruct((n, d), out_dtype),
        compiler_params=_cparams("arbitrary"),
    )(x, g, sc, sh)


def _combine_kernel(x_ref, y0_ref, y1_ref, gates_ref, gt_ref, g_ref, o_ref):
    g0 = gates_ref[:, 0:1]
    g1 = gates_ref[:, 1:2]
    moe = y0_ref[...] * g0 + y1_ref[...] * g1
    x = x_ref[...] + gt_ref[0] * moe
    o_ref[...] = _rms(x, g_ref[...])


def _combine(x, y01, gates, gt, g_final, *, row_off, n_tok, rows_per_batch, tt):
    n, d = x.shape
    bpb = rows_per_batch // tt
    b0 = row_off // tt
    b1 = (n_tok + row_off) // tt
    row = pl.BlockSpec((tt, d), lambda i: (i, 0))
    return pl.pallas_call(
        _combine_kernel,
        grid=(n // tt,),
        in_specs=[row,
                  pl.BlockSpec((tt, d), lambda i: (i + b0, 0)),
                  pl.BlockSpec((tt, d), lambda i: (i + b1, 0)),
                  pl.BlockSpec((tt, 2), lambda i: (i, 0)),
                  pl.BlockSpec((1, 1, d), lambda i: (i // bpb, 0, 0)),
                  pl.BlockSpec((1, d), lambda i: (0, 0))],
        out_specs=row,
        out_shape=jax.ShapeDtypeStruct((n, d), F32),
        compiler_params=_cparams("arbitrary"),
    )(x, y01, y01, gates, gt, g_final)


GATHER_ROWS = 256


def _gather_kernel(idx_ref, nlive_ref, src_ref, o_ref, buf_ref, sem):
    i = pl.program_id(0)

    @pl.when(i < nlive_ref[0])
    def _():
        base = i * GATHER_ROWS

        def row_copy(r, src_row):
            return pltpu.make_async_copy(src_ref.at[pl.ds(src_row, 1)], buf_ref.at[pl.ds(r, 1)], sem)

        def start(r, carry):
            row_copy(r, idx_ref[base + r]).start()
            return carry

        def wait(r, carry):
            row_copy(r, 0).wait()
            return carry

        lax.fori_loop(0, GATHER_ROWS, start, 0)
        lax.fori_loop(0, GATHER_ROWS, wait, 0)
        if len(o_ref.shape) == 2:
            o_ref[...] = buf_ref[...].astype(o_ref.dtype)
        else:
            w = o_ref.shape[2]
            for k in range(o_ref.shape[0]):
                o_ref[k] = buf_ref[:, k * w:(k + 1) * w].astype(o_ref.dtype)


def _gather_rows(src, idx, n_live_blocks, out_dtype, n_slabs=None):
    nr = idx.shape[0]
    d = src.shape[1]
    assert nr % GATHER_ROWS == 0
    blk = lambda i, idx, nl: jnp.minimum(i, nl[0] - 1)
    if n_slabs is None:
        out_spec = pl.BlockSpec((GATHER_ROWS, d), lambda i, idx, nl: (blk(i, idx, nl), 0))
        out_shape = (nr, d)
    else:
        out_spec = pl.BlockSpec((n_slabs, GATHER_ROWS, d // n_slabs),
                                lambda i, idx, nl: (0, blk(i, idx, nl), 0))
        out_shape = (n_slabs, nr, d // n_slabs)
    grid_spec = pltpu.PrefetchScalarGridSpec(
        num_scalar_prefetch=2,
        grid=(nr // GATHER_ROWS,),
        in_specs=[pl.BlockSpec(memory_space=pl.ANY)],
        out_specs=out_spec,
        scratch_shapes=[pltpu.VMEM((GATHER_ROWS, d), src.dtype), pltpu.SemaphoreType.DMA(())],
    )
    return pl.pallas_call(
        _gather_kernel,
        grid_spec=grid_spec,
        out_shape=jax.ShapeDtypeStruct(out_shape, out_dtype),
        compiler_params=_cparams("arbitrary"),
    )(idx, n_live_blocks, src)


def _lane_prefix_sum(x, lane):
    s = 1
    while s < LANES:
        x = x + jnp.where(lane >= s, pltpu.roll(x, s, axis=1), 0.0)
        s *= 2
    return x


def _logf_cumsum_kernel(*refs, p_len, t_pad):
    if p_len:
        fl_ref, bf_ref, past_ref, logf_ref, cum_ref = refs
    else:
        fl_ref, bf_ref, logf_ref, cum_ref = refs
        past_ref = None
    lane = lax.broadcasted_iota(jnp.int32, (B_HEADS, LANES), 1)
    carry = jnp.zeros((B_HEADS, 1), F32)
    for c in range(p_len // LANES):
        sl = slice(c * LANES, (c + 1) * LANES)
        y = _lane_prefix_sum(past_ref[0, :, sl], lane) + carry
        cum_ref[0, :, sl] = y
        carry = y[:, LANES - 1:LANES]
    for c in range(t_pad // LANES):
        sl = slice(c * LANES, (c + 1) * LANES)
        z = fl_ref[0, :, sl] + bf_ref[...]
        logf = jnp.minimum(z, 0.0) - jnp.log1p(jnp.exp(-jnp.abs(z)))
        logf_ref[0, :, sl] = logf
        y = _lane_prefix_sum(logf, lane) + carry
        cum_ref[0, :, p_len + c * LANES:p_len + (c + 1) * LANES] = y
        carry = y[:, LANES - 1:LANES]


def _logf_cumsum(fl_t, b_forget, past_t):
    b, _, t_pad = fl_t.shape
    p_len = 0 if past_t is None else past_t.shape[2]
    in_specs = [pl.BlockSpec((1, B_HEADS, t_pad), lambda i: (i, 0, 0)),
                pl.BlockSpec((B_HEADS, 1), lambda i: (0, 0))]
    args = [fl_t, b_forget]
    if p_len:
        in_specs.append(pl.BlockSpec((1, B_HEADS, p_len), lambda i: (i, 0, 0)))
        args.append(past_t)
    return pl.pallas_call(
        functools.partial(_logf_cumsum_kernel, p_len=p_len, t_pad=t_pad),
        grid=(b,),
        in_specs=in_specs,
        out_specs=[pl.BlockSpec((1, B_HEADS, t_pad), lambda i: (i, 0, 0)),
                   pl.BlockSpec((1, B_HEADS, p_len + t_pad), lambda i: (i, 0, 0))],
        out_shape=[jax.ShapeDtypeStruct((b, B_HEADS, t_pad), F32),
                   jax.ShapeDtypeStruct((b, B_HEADS, p_len + t_pad), F32)],
        compiler_params=_cparams("arbitrary"),
    )(*args)


def _indexer_kernel(qi_ref, small_ref, klo_ref, khi_ref, mask_ref, qh_ref, wb_ref, keys_ref, *,
                    tq, tkc, n_chunks, l_real, p_len, nq_per_batch, n_sel):
    i = pl.program_id(0)
    row0 = (i % nq_per_batch) * tq
    n_pairs = IDX_HEADS // 2
    sub = tkc // LANES if tkc % LANES == 0 else 0
    for b in range(n_pairs):
        qh_ref[b * tq:(b + 1) * tq, :] = qi_ref[:, b * LANES:(b + 1) * LANES]
    w = small_ref[:, IDX_DIM:IDX_DIM + IDX_HEADS] * (IDX_HEADS ** -0.5)
    w = w * (IDX_DIM ** -0.5)
    wcols = LANES if sub else tkc
    for h in range(IDX_HEADS):
        wb_ref[h] = jnp.broadcast_to(w[:, h:h + 1], (tq, wcols))
    qpos = p_len + row0 + lax.broadcasted_iota(jnp.int32, (tq, 1), 0)
    q_chunk = qpos // CHUNK
    last_allowed = (p_len + row0 + tq - 1) // CHUNK * CHUNK + CHUNK - 1
    n_allowed = jnp.minimum(last_allowed // tkc + 1, n_chunks)

    def allowed_of(c0):
        kpos = c0 + lax.broadcasted_iota(jnp.int32, (tq, tkc), 1)
        return jnp.logical_and(kpos // CHUNK <= q_chunk, kpos < l_real)

    def weighted(s, h):
        r = jnp.maximum(s, 0.0)
        if not sub:
            return r * wb_ref[h]
        wv = wb_ref[h]
        return jnp.concatenate([r[:, u * LANES:(u + 1) * LANES] * wv for u in range(sub)], axis=1)

    for c in range(n_chunks):
        c0 = c * tkc

        @pl.when(c < n_allowed)
        def _():
            q_all = qh_ref[...]
            s_lo = jnp.dot(q_all, klo_ref[0, :, c0:c0 + tkc], preferred_element_type=F32)
            s_hi = jnp.dot(q_all, khi_ref[0, :, c0:c0 + tkc], preferred_element_type=F32)
            score = jnp.zeros((tq, tkc), F32)
            for b in range(n_pairs):
                score = score + weighted(s_lo[b * tq:(b + 1) * tq, :], 2 * b)
                score = score + weighted(s_hi[b * tq:(b + 1) * tq, :], 2 * b + 1)
            score = score + 0.0
            score = jnp.where(allowed_of(c0), score, NEG_INF)
            bits = lax.bitcast_convert_type(score, jnp.int32)
            keys_ref[c] = jnp.where(bits < 0, bits ^ 0x7FFFFFFF, bits)

    def count(pred_of_chunk):
        def body(c, acc):
            ind = jnp.where(pred_of_chunk(keys_ref[c], c), 1.0, 0.0)
            if not sub:
                return acc + ind
            for u in range(sub):
                acc = acc + ind[:, u * LANES:(u + 1) * LANES]
            return acc

        acc = lax.fori_loop(0, n_allowed, body, jnp.zeros((tq, wcols), F32))
        return jnp.sum(acc, axis=1, keepdims=True)

    k_f = float(n_sel)
    t0 = jnp.where(count(lambda kc, c: kc >= 0) >= k_f, 0, INT_MIN).astype(jnp.int32)

    def bisect(it, t):
        cand = t + jnp.left_shift(jnp.int32(1), 30 - it)
        return jnp.where(count(lambda kc, c: kc >= cand) >= k_f, cand, t)

    t = lax.fori_loop(0, 31, bisect, t0)
    cge = count(lambda kc, c: kc >= t)
    excess = jnp.max(jnp.where(t > KEY_NEG_INF, cge, 0.0)) > k_f

    def emit(sel_of_chunk):
        for c in range(n_chunks):
            @pl.when(c < n_allowed)
            def _():
                sel = jnp.logical_and(sel_of_chunk(keys_ref[c], c), allowed_of(c * tkc))
                mask_ref[c] = jnp.where(sel, 0.0, NEG_INF).astype(mask_ref.dtype)

            @pl.when(c >= n_allowed)
            def _():
                mask_ref[c] = jnp.full((tq, tkc), NEG_INF, mask_ref.dtype)

    @pl.when(jnp.logical_not(excess))
    def _():
        emit(lambda kc, c: kc >= t)

    @pl.when(excess)
    def _():
        need = k_f - count(lambda kc, c: kc > t)
        lane_idx = lax.broadcasted_iota(jnp.int32, (tq, tkc), 1)
        m = jnp.zeros((tq, 1), jnp.int32)
        for bit in reversed(range(max(1, (n_chunks * tkc - 1).bit_length()))):
            cand = m + (1 << bit)
            c_lt = count(lambda kc, c: jnp.logical_and(kc == t, c * tkc + lane_idx < cand))
            m = jnp.where(c_lt < need, cand, m)
        emit(lambda kc, c: jnp.logical_or(
            kc > t, jnp.logical_and(kc == t, c * tkc + lane_idx <= m)))


def _indexer_mask(qi, small, k_lo, k_hi, *, rows_per_batch, tq, tkc, l_real, p_len):
    n = qi.shape[0]
    l_pad = k_lo.shape[2]
    n_chunks = l_pad // tkc
    nqb = rows_per_batch // tq
    n_sel = min(TOPK_MAX, l_real // 4)
    wcols = LANES if tkc % LANES == 0 else tkc
    kern = functools.partial(_indexer_kernel, tq=tq, tkc=tkc, n_chunks=n_chunks, l_real=l_real,
                             p_len=p_len, nq_per_batch=nqb, n_sel=n_sel)
    kspec = pl.BlockSpec((1, LANES, l_pad), lambda i: (i // nqb, 0, 0))
    return pl.pallas_call(
        kern,
        grid=(n // tq,),
        in_specs=[pl.BlockSpec((tq, IDX_WIDTH), lambda i: (i, 0)),
                  pl.BlockSpec((tq, SMALL_WIDTH), lambda i: (i, 0)),
                  kspec, kspec],
        out_specs=pl.BlockSpec((n_chunks, tq, tkc), lambda i: (0, i, 0)),
        out_shape=jax.ShapeDtypeStruct((n_chunks, n, tkc), BF16),
        scratch_shapes=[pltpu.VMEM((IDX_HEADS // 2 * tq, LANES), BF16),
                        pltpu.VMEM((IDX_HEADS, tq, wcols), F32),
                        pltpu.VMEM((n_chunks, tq, tkc), jnp.int32)],
        compiler_params=_cparams("arbitrary"),
    )(qi, small, k_lo, k_hi)


def _softmax_step(s, carry, v_tile):
    m, l, acc = carry
    m_new = jnp.maximum(m, jnp.max(s, axis=1, keepdims=True))
    m_safe = jnp.where(m_new == NEG_INF, 0.0, m_new)
    alpha = jnp.exp2(m - m_safe)
    p = jnp.exp2(s - m_safe)
    l = alpha * l + jnp.sum(p, axis=1, keepdims=True)
    acc = alpha * acc + jnp.dot(p.astype(BF16), v_tile, preferred_element_type=F32)
    return m_new, l, acc


def _softmax_init(rows):
    return (jnp.full((rows, 1), NEG_INF, F32), jnp.zeros((rows, 1), F32),
            jnp.zeros((rows, HEAD_DIM), F32))


def _qk(q, k_tile):
    return lax.dot_general(q, k_tile, (((1,), (1,)), ((), ())), preferred_element_type=F32)


def _fox_prompt_kernel(q_ref, k_ref, v_ref, fk_ref, o_ref, *, t_tile, n_heads, nt):
    i = pl.program_id(1)
    lanes = lambda g: slice(g * HEAD_DIM, (g + 1) * HEAD_DIM)
    qs = [q_ref[:, lanes(g)] for g in range(n_heads)]

    def tile(j, carries, diagonal):
        k0 = pl.multiple_of(j * t_tile, t_tile)
        out = []
        for g in range(n_heads):
            s = _qk(qs[g], k_ref[pl.ds(k0, t_tile), lanes(g)]) - fk_ref[g * nt + j] * LOG2E
            if diagonal:
                row = lax.broadcasted_iota(jnp.int32, (t_tile, t_tile), 0)
                col = lax.broadcasted_iota(jnp.int32, (t_tile, t_tile), 1)
                s = jnp.where(col <= row, s, NEG_INF)
            out.append(_softmax_step(s, carries[g], v_ref[pl.ds(k0, t_tile), lanes(g)]))
        return tuple(out)

    init = tuple(_softmax_init(t_tile) for _ in range(n_heads))
    carries = lax.fori_loop(0, i, lambda j, c: tile(j, c, False), init)
    carries = tile(i, carries, True)
    for g in range(n_heads):
        _, l, acc = carries[g]
        o_ref[:, lanes(g)] = (acc / l).astype(o_ref.dtype)


def _fox_prompt(q, k, v, fk_tiles, *, t_tile, heads_per_step):
    t, width = q.shape
    heads = width // HEAD_DIM
    nt = t // t_tile
    gw = heads_per_step * HEAD_DIM
    resident = dict(pipeline_mode=pl.Buffered(1))
    return pl.pallas_call(
        functools.partial(_fox_prompt_kernel, t_tile=t_tile, n_heads=heads_per_step, nt=nt),
        grid=(heads // heads_per_step, nt),
        in_specs=[pl.BlockSpec((t_tile, gw), lambda h, i: (i, h)),
                  pl.BlockSpec((t, gw), lambda h, i: (0, h), **resident),
                  pl.BlockSpec((t, gw), lambda h, i: (0, h), **resident),
                  pl.BlockSpec((heads_per_step * nt, 1, t_tile), lambda h, i: (h, 0, 0))],
        out_specs=pl.BlockSpec((t_tile, gw), lambda h, i: (i, h)),
        out_shape=jax.ShapeDtypeStruct((t, width), BF16),
        compiler_params=_cparams("arbitrary", "arbitrary"),
    )(q, k, v, fk_tiles)


def _dsa_prompt_kernel(q_ref, k_ref, v_ref, mask_ref, bias_ref, o_ref, *, t_tile, grp, mask_w):
    i = pl.program_id(1)
    q = jnp.concatenate([q_ref[:, j * HEAD_DIM:(j + 1) * HEAD_DIM] for j in range(grp)], axis=0)
    n_sub = t_tile // mask_w

    def tile(j, carry, bias_cls):
        k0 = pl.multiple_of(j * t_tile, t_tile)
        s = _qk(q, k_ref[pl.ds(k0, t_tile), :]).reshape(grp, t_tile, t_tile)
        madd = jnp.concatenate([mask_ref[j * n_sub + u] for u in range(n_sub)], axis=1)
        s = s + madd.astype(F32)[None]
        if bias_cls is not None:
            s = s + bias_ref[bias_cls]
        return _softmax_step(s.reshape(grp * t_tile, t_tile), carry, v_ref[pl.ds(k0, t_tile), :])

    carry = lax.fori_loop(0, jnp.maximum(i - 1, 0), lambda j, c: tile(j, c, None),
                          _softmax_init(grp * t_tile))
    carry = lax.fori_loop(jnp.maximum(i - 1, 0), i, lambda j, c: tile(j, c, 1), carry)
    _, l, acc = tile(i, carry, 0)
    out = acc / l
    for j in range(grp):
        o_ref[:, j * HEAD_DIM:(j + 1) * HEAD_DIM] = out[j * t_tile:(j + 1) * t_tile].astype(o_ref.dtype)


def _dsa_prompt(q, k, v, mask, bias, *, t_tile):
    t, width = q.shape
    kv_heads = k.shape[1] // HEAD_DIM
    grp = width // HEAD_DIM // kv_heads
    nt = t // t_tile
    n_mask, _, mask_w = mask.shape
    resident = dict(pipeline_mode=pl.Buffered(1))
    return pl.pallas_call(
        functools.partial(_dsa_prompt_kernel, t_tile=t_tile, grp=grp, mask_w=mask_w),
        grid=(kv_heads, nt),
        in_specs=[pl.BlockSpec((t_tile, grp * HEAD_DIM), lambda g, i: (i, g)),
                  pl.BlockSpec((t, HEAD_DIM), lambda g, i: (0, g), **resident),
                  pl.BlockSpec((t, HEAD_DIM), lambda g, i: (0, g), **resident),
                  pl.BlockSpec((n_mask, t_tile, mask_w), lambda g, i: (0, i, 0)),
                  pl.BlockSpec((2, grp, t_tile, t_tile), lambda g, i: (0, g, 0, 0), **resident)],
        out_specs=pl.BlockSpec((t_tile, grp * HEAD_DIM), lambda g, i: (i, g)),
        out_shape=jax.ShapeDtypeStruct((t, width), BF16),
        compiler_params=_cparams("arbitrary", "arbitrary"),
    )(q, k, v, mask, bias)


def _attend(q_ref, k_of, v_of, bias_of_head, m_ref, l_ref, acc_ref, *, tq, kv_heads, grp):
    for g in range(kv_heads):
        kg = k_of(g).astype(BF16)
        vg = v_of(g).astype(BF16)
        qs = [q_ref[:, (g * grp + j) * HEAD_DIM:(g * grp + j + 1) * HEAD_DIM] for j in range(grp)]
        qg = qs[0] if grp == 1 else jnp.concatenate(qs, axis=0)
        s_all = _qk(qg, kg)
        for j in range(grp):
            h = g * grp + j
            s = s_all[j * tq:(j + 1) * tq, :] + bias_of_head(h)
            m_ref[h], l_ref[h], acc_ref[h] = _softmax_step(s, (m_ref[h], l_ref[h], acc_ref[h]), vg)


def _flash_kernel(*refs, mode, n_a, tq, tk_a, tk_b, p_len, kv_heads, grp):
    heads = kv_heads * grp
    it = iter(refs)
    q_ref = next(it)
    ka_ref, va_ref, kb_ref, vb_ref = next(it), next(it), next(it), next(it)
    if mode == "dsa":
        maska_ref, biasa_ref, maskb_ref, biasb_ref = next(it), next(it), next(it), next(it)
    else:
        fq_ref, fka_ref, fkb_ref = next(it), next(it), next(it)
    o_ref, m_ref, l_ref, acc_ref = next(it), next(it), next(it), next(it)
    j = pl.program_id(1)

    @pl.when(j == 0)
    def _():
        m_ref[...] = jnp.full(m_ref.shape, NEG_INF, F32)
        l_ref[...] = jnp.zeros(l_ref.shape, F32)
        acc_ref[...] = jnp.zeros(acc_ref.shape, F32)

    common = dict(tq=tq, kv_heads=kv_heads, grp=grp)

    @pl.when(j < n_a)
    def _():
        if mode == "dsa":
            madd = maska_ref[...].astype(F32)
            bias = lambda h: biasa_ref[0, h] + madd
        else:
            bias = lambda h: (fq_ref[:, h:h + 1] - fka_ref[0, h:h + 1, :]) * LOG2E
        k_of = lambda g: ka_ref[0, pl.ds(g, tk_a, stride=kv_heads), :]
        v_of = lambda g: va_ref[0, pl.ds(g, tk_a, stride=kv_heads), :]
        _attend(q_ref, k_of, v_of, bias, m_ref, l_ref, acc_ref, **common)

    @pl.when(j == n_a)
    def _():
        if mode == "dsa":
            madd = maskb_ref[...].astype(F32)
            bias = lambda h: biasb_ref[0, h] + madd
        else:
            row = lax.broadcasted_iota(jnp.int32, (tq, tk_b), 0)
            col = lax.broadcasted_iota(jnp.int32, (tq, tk_b), 1)
            causal = jnp.where(col <= row, 0.0, NEG_INF)
            bias = lambda h: (fq_ref[:, h:h + 1] - fkb_ref[0, h:h + 1, :tk_b]) * LOG2E + causal
        k_of = lambda g: kb_ref[:, g * HEAD_DIM:(g + 1) * HEAD_DIM]
        v_of = lambda g: vb_ref[:, g * HEAD_DIM:(g + 1) * HEAD_DIM]
        _attend(q_ref, k_of, v_of, bias, m_ref, l_ref, acc_ref, **common)
        for h in range(heads):
            o_ref[:, h * HEAD_DIM:(h + 1) * HEAD_DIM] = (acc_ref[h] / l_ref[h]).astype(o_ref.dtype)


def _flash_sample(mode, q, k_new, v_new, k_past, v_past, *, tk_a, mask_a=None, bias_a=None,
                  mask_b=None, bias_b=None, fq=None, fk_t=None):
    n, width = q.shape
    kv_w = k_new.shape[1]
    b = k_past.shape[0]
    tq = n // b
    heads = width // HEAD_DIM
    kv_heads = kv_w // HEAD_DIM
    p_len = k_past.shape[1] // kv_heads
    grp = heads // kv_heads
    n_a = p_len // tk_a
    a_blk = lambda j: jnp.minimum(j, n_a - 1)

    in_specs = [pl.BlockSpec((tq, width), lambda g, j: (g, 0))]
    spec = pl.BlockSpec((1, tk_a * kv_heads, HEAD_DIM), lambda g, j: (g, a_blk(j), 0))
    in_specs += [spec, spec]
    spec = pl.BlockSpec((tq, kv_w), lambda g, j: (g, 0))
    in_specs += [spec, spec]
    args = [q, k_past, v_past, k_new, v_new]
    if mode == "dsa":
        in_specs += [pl.BlockSpec((tq, tk_a), lambda g, j: (g, a_blk(j))),
                     pl.BlockSpec((1, heads, tq, tk_a), lambda g, j: (a_blk(j), 0, 0, 0)),
                     pl.BlockSpec((tq, tq), lambda g, j: (g, 0)),
                     pl.BlockSpec((1, heads, tq, tq), lambda g, j: (0, 0, 0, 0))]
        args += [mask_a, bias_a, mask_b, bias_b]
    else:
        in_specs += [pl.BlockSpec((tq, heads), lambda g, j: (g, 0)),
                     pl.BlockSpec((1, heads, tk_a), lambda g, j: (g, 0, a_blk(j))),
                     pl.BlockSpec((1, heads, LANES), lambda g, j: (g, 0, p_len // LANES))]
        args += [fq, fk_t, fk_t]

    kern = functools.partial(_flash_kernel, mode=mode, n_a=n_a, tq=tq, tk_a=tk_a, tk_b=tq,
                             p_len=p_len, kv_heads=kv_heads, grp=grp)
    return pl.pallas_call(
        kern,
        grid=(b, n_a + 1),
        in_specs=in_specs,
        out_specs=pl.BlockSpec((tq, width), lambda g, j: (g, 0)),
        out_shape=jax.ShapeDtypeStruct((n, width), BF16),
        scratch_shapes=[pltpu.VMEM((heads, tq, 1), F32),
                        pltpu.VMEM((heads, tq, 1), F32),
                        pltpu.VMEM((heads, tq, HEAD_DIM), F32)],
        compiler_params=_cparams("arbitrary", "arbitrary"),
    )(*args)


def _router_kernel(lt_ref, eidx_ref, gate_ref):
    row = lambda r: lt_ref[r:r + 1, :]
    g = [row(r) for r in range(N_GROUPS)]
    gmax = functools.reduce(jnp.maximum, g)
    gsel = jnp.full(gmax.shape, N_GROUPS - 1, jnp.int32)
    for r in reversed(range(N_GROUPS - 1)):
        gsel = jnp.where(g[r] == gmax, r, gsel)
    denom = functools.reduce(lambda a, b: a + b, [jnp.exp(x - gmax) for x in g])
    g_prob = 1.0 / denom
    e_in = []
    for kk in range(EXPERTS_PER_GROUP):
        v = row(N_GROUPS + (N_GROUPS - 1) * EXPERTS_PER_GROUP + kk)
        for r in reversed(range(N_GROUPS - 1)):
            v = jnp.where(gsel == r, row(N_GROUPS + r * EXPERTS_PER_GROUP + kk), v)
        e_in.append(v)
    v1 = functools.reduce(jnp.maximum, e_in)
    i1 = jnp.full(v1.shape, EXPERTS_PER_GROUP - 1, jnp.int32)
    for kk in reversed(range(EXPERTS_PER_GROUP - 1)):
        i1 = jnp.where(e_in[kk] == v1, kk, i1)
    rest = [jnp.where(i1 == kk, NEG_INF, e_in[kk]) for kk in range(EXPERTS_PER_GROUP)]
    v2 = functools.reduce(jnp.maximum, rest)
    i2 = jnp.full(v2.shape, EXPERTS_PER_GROUP - 1, jnp.int32)
    for kk in reversed(range(EXPERTS_PER_GROUP - 1)):
        i2 = jnp.where(jnp.logical_and(rest[kk] == v2, i1 != kk), kk, i2)
    e2 = jnp.exp(v2 - v1)
    inv = 1.0 / (1.0 + e2)
    eidx_ref[0:1, :] = gsel * EXPERTS_PER_GROUP + i1
    eidx_ref[1:2, :] = gsel * EXPERTS_PER_GROUP + i2
    gate_ref[0:1, :] = inv * g_prob
    gate_ref[1:2, :] = (e2 * inv) * g_prob


def _router(logits_t):
    n = logits_t.shape[1]
    full = lambda r: pl.BlockSpec((r, n), lambda: (0, 0))
    return pl.pallas_call(
        _router_kernel,
        in_specs=[full(logits_t.shape[0])],
        out_specs=[full(TOP_K_IN_GROUP), full(TOP_K_IN_GROUP)],
        out_shape=[jax.ShapeDtypeStruct((TOP_K_IN_GROUP, n), jnp.int32),
                   jax.ShapeDtypeStruct((TOP_K_IN_GROUP, n), F32)],
        compiler_params=pltpu.CompilerParams(vmem_limit_bytes=VMEM_LIMIT),
    )(logits_t)


EXPERT_CHUNK = 256
EXPERT_CHUNKS_PER_BLOCK = 4
EXPERT_ROWS = EXPERT_CHUNK * EXPERT_CHUNKS_PER_BLOCK
EXPERT_K_TILE = 512
EXPERT_N_TILE = 1024


def _experts_kernel(sb_e_ref, sb_row0_ref, sb_nch_ref, sb_blk_ref, x_hbm, w1_ref, w3_ref, w2_ref,
                    o_ref, x_ref, a_ref, b_ref, act_ref, sem, *, nk):
    del sb_e_ref, sb_blk_ref
    s = pl.program_id(0)
    t = pl.program_id(1)
    n_ch = sb_nch_ref[s]

    @pl.when(jnp.logical_and(t == 0, n_ch > 0))
    def _():
        row0 = pl.multiple_of(sb_row0_ref[s], EXPERT_CHUNK)
        copies = [pltpu.make_async_copy(x_hbm.at[k, pl.ds(row0, EXPERT_ROWS)], x_ref.at[k], sem)
                  for k in range(nk)]
        for cp in copies:
            cp.start()
        for cp in copies:
            cp.wait()

    for nc in range(1, EXPERT_CHUNKS_PER_BLOCK + 1):
        rows = nc * EXPERT_CHUNK

        @pl.when(jnp.logical_and(t < nk, n_ch == nc))
        def _():
            xc = x_ref[t, 0:rows, :]
            a = jnp.dot(xc, w1_ref[0].astype(BF16), preferred_element_type=F32)
            b = jnp.dot(xc, w3_ref[0].astype(BF16), preferred_element_type=F32)

            @pl.when(t == 0)
            def _():
                a_ref[0:rows, :] = a
                b_ref[0:rows, :] = b

            @pl.when(t > 0)
            def _():
                a_ref[0:rows, :] += a
                b_ref[0:rows, :] += b

            @pl.when(t == nk - 1)
            def _():
                g = a_ref[0:rows, :]
                act_ref[0:rows, :] = (g * jax.nn.sigmoid(g) * b_ref[0:rows, :]).astype(BF16)

        @pl.when(jnp.logical_and(t >= nk, n_ch == nc))
        def _():
            o_ref[0:rows, :] = jnp.dot(act_ref[0:rows, :], w2_ref[0].astype(BF16),
                                       preferred_element_type=F32)


def _experts(xs, sb_expert, sb_row0, sb_nch, sb_blk, w1, w3, w2, n_sb):
    nk = xs.shape[0]
    d = nk * EXPERT_K_TILE
    d_e = w1.shape[2]
    nn = d // EXPERT_N_TILE
    r = EXPERT_ROWS
    grid_spec = pltpu.PrefetchScalarGridSpec(
        num_scalar_prefetch=4,
        grid=(n_sb, nk + nn),
        in_specs=[
            pl.BlockSpec(memory_space=pl.ANY),
            pl.BlockSpec((1, EXPERT_K_TILE, d_e),
                         lambda s, t, e, r0, nch, blk: (e[s], jnp.minimum(t, nk - 1), 0)),
            pl.BlockSpec((1, EXPERT_K_TILE, d_e),
                         lambda s, t, e, r0, nch, blk: (e[s], jnp.minimum(t, nk - 1), 0)),
            pl.BlockSpec((1, d_e, EXPERT_N_TILE),
                         lambda s, t, e, r0, nch, blk: (e[s], 0, jnp.maximum(t - nk, 0))),
        ],
        out_specs=pl.BlockSpec(
            (r, EXPERT_N_TILE),
            lambda s, t, e, r0, nch, blk: (blk[s], jnp.where(nch[s] > 0, jnp.maximum(t - nk, 0), 0))),
        scratch_shapes=[pltpu.VMEM((nk, r, EXPERT_K_TILE), BF16),
                        pltpu.VMEM((r, d_e), F32),
                        pltpu.VMEM((r, d_e), F32),
                        pltpu.VMEM((r, d_e), BF16),
                        pltpu.SemaphoreType.DMA(())],
    )
    return pl.pallas_call(
        functools.partial(_experts_kernel, nk=nk),
        grid_spec=grid_spec,
        out_shape=jax.ShapeDtypeStruct(((n_sb + 1) * r, d), F32),
        compiler_params=_cparams("arbitrary", "arbitrary"),
    )(sb_expert, sb_row0, sb_nch, sb_blk, xs, w1, w3, w2)


def _dispatch(expert, n_sb, n_x_rows):
    n = expert.shape[0]
    a = n * TOP_K_IN_GROUP
    ch, r = EXPERT_CHUNK, EXPERT_ROWS
    flat_e = expert.reshape(a)
    flat_tok = jnp.arange(a, dtype=jnp.int32) // TOP_K_IN_GROUP
    order = jnp.argsort(flat_e)
    se = flat_e[order]
    counts = jnp.bincount(flat_e, length=N_EXPERTS).astype(jnp.int32)
    seg_start = jnp.cumsum(counts) - counts
    rank = jnp.arange(a, dtype=jnp.int32) - seg_start[se]
    nch_e = (counts + ch - 1) // ch
    xch_end = jnp.cumsum(nch_e)
    x_start = (xch_end - nch_e) * ch
    nsb_e = (counts + r - 1) // r
    sb_end = jnp.cumsum(nsb_e)
    sb_start = sb_end - nsb_e
    x_row_tok = jnp.zeros((n_x_rows,), jnp.int32).at[x_start[se] + rank].set(flat_tok[order])
    n_live_x_blocks = (xch_end[-1] * ch + GATHER_ROWS - 1) // GATHER_ROWS
    y_row = jnp.zeros((a,), jnp.int32).at[order].set(sb_start[se] * r + rank)
    y_row = y_row.reshape(n, TOP_K_IN_GROUP)
    total = sb_end[-1]
    s_idx = jnp.arange(n_sb, dtype=jnp.int32)
    e_of_sb = jnp.minimum(jnp.searchsorted(sb_end, s_idx, side="right"), N_EXPERTS - 1).astype(jnp.int32)
    k_in_e = s_idx - sb_start[e_of_sb]
    rows_of_sb = jnp.clip(counts[e_of_sb] - k_in_e * r, 0, r)
    live = s_idx < total
    last_e = e_of_sb[jnp.maximum(total - 1, 0)]
    sb_expert = jnp.where(live, e_of_sb, last_e).astype(jnp.int32)
    sb_nch = jnp.where(live, (rows_of_sb + ch - 1) // ch, 0).astype(jnp.int32)
    sb_row0 = jnp.where(live, x_start[e_of_sb] + k_in_e * r, 0).astype(jnp.int32)
    sb_blk = jnp.where(live, s_idx, n_sb).astype(jnp.int32)
    return (x_row_tok, n_live_x_blocks.astype(jnp.int32).reshape(1), y_row, sb_expert, sb_row0,
            sb_nch, sb_blk)


def _t5_bucket(rel):
    half = N_BUCKETS // 2
    max_exact = half // 2
    bucket = jnp.where(rel > 0, half, 0).astype(jnp.int32)
    n = jnp.abs(rel)
    large = max_exact + (jnp.log(jnp.maximum(n, 1).astype(jnp.float32) / max_exact)
                         / math.log(MAX_DISTANCE / max_exact) * (half - max_exact)).astype(jnp.int32)
    large = jnp.minimum(large, half - 1)
    return bucket + jnp.where(n < max_exact, n, large).astype(jnp.int32)


def _bias_tiles(rel_bias, rel0_list, tq, tk):
    rel0 = jnp.asarray(rel0_list, jnp.int32)[:, None, None]
    rel = rel0 + jnp.arange(tk, dtype=jnp.int32)[None, None, :] - jnp.arange(tq, dtype=jnp.int32)[None, :, None]
    return jnp.transpose(rel_bias[_t5_bucket(rel)].astype(F32), (0, 3, 1, 2))


def _pick(n, candidates):
    for c in candidates:
        if n % c == 0:
            return c
    return n


INDEXER_TILE = 256
ATTN_TILE = 512
FOX_HEADS_PER_STEP = 4


def _mixer_half(x, mods, past, w_in_packed, b_forget, w_out, rel_bias, g_mix, g_ffn):
    b, t, d = x.shape
    n = b * t
    sh1, sc1, gt1, sh2, sc2, _ = mods
    xf = x.reshape(n, d)
    tt = _pick(t, (256, 128, 64, 32, 16))
    h = _norm_mod(xf, g_mix, sc1, sh1, rows_per_batch=t, tt=tt, out_dtype=BF16)

    tm = _pick(n, (512, 256, 128))

    def proj(col_off, width, tn, **kw):
        return _matmul([h], w_in_packed, col_off=col_off, n=width, tm=tm, tn=tn, **kw)

    off = 0
    qa = proj(off, A_WIDTH, 1024, out_dtypes=(BF16,), out_scale=Q_SCALE); off += A_WIDTH
    ka, ka16 = proj(off, A_KV_WIDTH, 512, out_dtypes=(F32, BF16)); off += A_KV_WIDTH
    va, va16 = proj(off, A_KV_WIDTH, 512, out_dtypes=(F32, BF16)); off += A_KV_WIDTH
    qi = proj(off, IDX_WIDTH, 1024, out_dtypes=(BF16,)); off += IDX_WIDTH
    qb = proj(off, B_WIDTH, 1024, out_dtypes=(BF16,), out_scale=Q_SCALE); off += B_WIDTH
    kb, kb16 = proj(off, B_WIDTH, 1024, out_dtypes=(F32, BF16)); off += B_WIDTH
    vb, vb16 = proj(off, B_WIDTH, 1024, out_dtypes=(F32, BF16)); off += B_WIDTH
    small = proj(off, SMALL_WIDTH, SMALL_WIDTH)

    ki = small[:, :IDX_DIM]
    fl = small[:, IDX_DIM + IDX_HEADS:IDX_DIM + IDX_HEADS + B_HEADS]

    p_len = 0 if past is None else past[0].shape[1]
    t_pad = -(-t // LANES) * LANES
    l_real = p_len + t
    l_pad = p_len + t_pad

    fl_t = jnp.transpose(fl.reshape(b, t, B_HEADS), (0, 2, 1))
    fl_t = jnp.pad(fl_t, ((0, 0), (0, 0), (0, t_pad - t)))
    past_logf_t = None if past is None else jnp.transpose(past[5], (0, 2, 1))
    logf_t, fcum_t = _logf_cumsum(fl_t, b_forget.reshape(B_HEADS, 1), past_logf_t)
    logf = jnp.transpose(logf_t[:, :, :t], (0, 2, 1))

    ki_b = ki.reshape(b, t, IDX_DIM)
    if past is not None:
        ki_b = jnp.concatenate([past[2], ki_b], axis=1)
    ki_t = jnp.transpose(jnp.pad(ki_b, ((0, 0), (0, l_pad - l_real), (0, 0))), (0, 2, 1)).astype(BF16)
    zeros = jnp.zeros_like(ki_t)
    k_lo = jnp.concatenate([ki_t, zeros], axis=1)
    k_hi = jnp.concatenate([zeros, ki_t], axis=1)

    if past is None:
        tile = ATTN_TILE
        assert b == 1 and t % tile == 0 and tile >= MAX_DISTANCE and tile % INDEXER_TILE == 0
        mask = _indexer_mask(qi, small, k_lo, k_hi, rows_per_batch=t, tq=INDEXER_TILE,
                             tkc=INDEXER_TILE, l_real=l_real, p_len=0)
        bias = _bias_tiles(rel_bias, [0, -tile], tile, tile)
        far = rel_bias[_t5_bucket(jnp.int32(-MAX_DISTANCE))].astype(F32)
        bias = (bias - far[None, :, None, None]) * LOG2E
        out_a = _dsa_prompt(qa, ka16, va16, mask, bias, t_tile=tile)
        fk_tiles = fcum_t.reshape(B_HEADS * (t // tile), 1, tile)
        out_b = _fox_prompt(qb, kb16, vb16, fk_tiles, t_tile=tile, heads_per_step=FOX_HEADS_PER_STEP)
    else:
        tk_a = _pick(p_len, (512, 256, 128))
        mask = _indexer_mask(qi, small, k_lo, k_hi, rows_per_batch=t, tq=t, tkc=l_pad,
                             l_real=l_real, p_len=p_len).reshape(n, l_pad)
        n_a = p_len // tk_a
        bias_a = _bias_tiles(rel_bias, [c * tk_a - p_len for c in range(n_a)], t, tk_a) * LOG2E
        bias_b = _bias_tiles(rel_bias, [0], t, t) * LOG2E
        cache = lambda a: a.reshape(b, -1, HEAD_DIM)
        out_a = _flash_sample("dsa", qa, ka16, va16, cache(past[0]), cache(past[1]), tk_a=tk_a,
                              mask_a=mask[:, :p_len], bias_a=bias_a,
                              mask_b=mask[:, p_len:p_len + t], bias_b=bias_b)
        fq = jnp.transpose(fcum_t[:, :, p_len:p_len + t], (0, 2, 1)).reshape(n, B_HEADS)
        out_b = _flash_sample("fox", qb, kb16, vb16, cache(past[3]), cache(past[4]), tk_a=tk_a,
                              fq=fq, fk_t=fcum_t)

    gate1 = gt1.reshape(1, d) if b == 1 else jnp.repeat(gt1.reshape(b, d), t, axis=0)
    x1 = _matmul([out_a, out_b], w_out, tm=tm, tn=512, res=xf, gate=gate1)
    h2 = _norm_mod(x1, g_ffn, sc2, sh2, rows_per_batch=t, tt=tt, out_dtype=F32)

    state = (ka.reshape(1, b, t, A_KV_HEADS, HEAD_DIM), va.reshape(1, b, t, A_KV_HEADS, HEAD_DIM),
             ki.reshape(1, b, t, IDX_DIM),
             kb.reshape(1, b, t, B_HEADS, HEAD_DIM), vb.reshape(1, b, t, B_HEADS, HEAD_DIM),
             logf.reshape(1, b, t, B_HEADS))
    return x1, h2, state


def kernel(x_prompt, x_sample, cache_a_k, cache_a_v, cache_idx_k, cache_b_k, cache_b_v, cache_b_logf, c_prompt, c_sample, w_in, b_forget, w_out, rel_bias, w_mod, b_mod, g_mix, g_ffn, w_router_group, b_router_group, w_router_expert, b_router_expert, w_gate, w_up, w_down, g_final):
    assert w_in.shape[0] == 1, "single-layer trunk"
    d = x_prompt.shape[-1]
    bp, tp, _ = x_prompt.shape
    bs, ts, _ = x_sample.shape
    n_p, n_s = bp * tp, bs * ts
    layer0 = lambda a: a.reshape(a.shape[1:])

    n_c = bp + bs
    c_rows = -(-n_c // 8) * 8
    c_all = jnp.pad(jnp.concatenate([c_prompt, c_sample], axis=0), ((0, c_rows - n_c), (0, 0)))
    mod = _matmul([c_all], layer0(w_mod), tm=c_rows, tn=512, silu_a=True, bias=b_mod.reshape(1, -1))

    def mods_of(lo, hi):
        m6 = mod[lo:hi].reshape(hi - lo, 6, 1, d)
        return [m6[:, i] for i in range(6)]

    mods_p, mods_s = mods_of(0, bp), mods_of(bp, n_c)

    w = layer0(w_in)
    o1 = A_WIDTH + 2 * A_KV_WIDTH + IDX_WIDTH
    o2 = o1 + IDX_DIM + IDX_HEADS
    o3 = o2 + 3 * B_WIDTH
    tail_pad = SMALL_WIDTH - (IDX_DIM + IDX_HEADS + B_HEADS)
    w_packed = jnp.concatenate(
        [w[:, :o1], w[:, o2:o3], w[:, o1:o2], w[:, o3:], jnp.zeros((d, tail_pad), w.dtype)],
        axis=1).astype(BF16)

    g_mix2, g_ffn2, g_fin2 = g_mix.reshape(1, d), g_ffn.reshape(1, d), g_final.reshape(1, d)
    past = tuple(layer0(c) for c in (cache_a_k, cache_a_v, cache_idx_k, cache_b_k, cache_b_v,
                                      cache_b_logf))
    b_f, w_o = layer0(b_forget), layer0(w_out)
    x1_p, h2_p, st_p = _mixer_half(x_prompt, mods_p, None, w_packed, b_f, w_o, rel_bias, g_mix2, g_ffn2)
    x1_s, h2_s, st_s = _mixer_half(x_sample, mods_s, past, w_packed, b_f, w_o, rel_bias, g_mix2, g_ffn2)

    n_tok = n_p + n_s
    h2 = jnp.concatenate([h2_p, h2_s], axis=0)
    w_r = jnp.concatenate([layer0(w_router_group), layer0(w_router_expert)], axis=1)
    n_cls = w_r.shape[1]
    w_r = jnp.pad(w_r, ((0, 0), (0, LANES - n_cls)))
    b_r = jnp.pad(jnp.concatenate([layer0(b_router_group), layer0(b_router_expert)]), (0, LANES - n_cls))
    logits = _matmul([h2], w_r, tm=_pick(n_tok, (512, 256, 128)), tn=LANES, bias=b_r.reshape(1, LANES))
    eidx_t, gate_t = _router(jnp.transpose(logits))
    expert = jnp.transpose(eidx_t)
    gates = jnp.transpose(gate_t)

    n_assign = n_tok * TOP_K_IN_GROUP
    assert n_assign % GATHER_ROWS == 0
    n_sb = n_assign // EXPERT_ROWS + N_EXPERTS
    n_x_rows = -(-n_assign // EXPERT_CHUNK) * EXPERT_CHUNK + N_EXPERTS * EXPERT_CHUNK + EXPERT_ROWS
    x_row_tok, n_live_x, y_row, sb_expert, sb_row0, sb_nch, sb_blk = _dispatch(expert, n_sb, n_x_rows)
    assert d % EXPERT_K_TILE == 0 and d % EXPERT_N_TILE == 0
    xs = _gather_rows(h2, x_row_tok, n_live_x, BF16, n_slabs=d // EXPERT_K_TILE)
    yb = _experts(xs, sb_expert, sb_row0, sb_nch, sb_blk, layer0(w_gate), layer0(w_up),
                  layer0(w_down), n_sb)
    y_idx = jnp.concatenate([y_row[:, 0], y_row[:, 1]])
    y01 = _gather_rows(yb, y_idx, jnp.full((1,), n_assign // GATHER_ROWS, jnp.int32), F32)

    def finish(x1, lo, gt2, t):
        tt = _pick(t, (128, 64, 32, 16))
        return _combine(x1, y01, gates[lo:lo + x1.shape[0]], gt2, g_fin2, row_off=lo, n_tok=n_tok,
                        rows_per_batch=t, tt=tt)

    y_p = finish(x1_p, 0, mods_p[5], tp).reshape(bp, tp, d)
    y_s = finish(x1_s, n_p, mods_s[5], ts).reshape(bs, ts, d)
    return (y_p, y_s) + st_p + st_s
```

```python
import functools
import math

import numpy as np
import jax
import jax.numpy as jnp
from jax import lax
from jax.experimental import pallas as pl
from jax.experimental.pallas import tpu as pltpu

CHUNK = 64
HEAD_DIM = 128
A_HEADS = 16
A_KV_HEADS = 4
IDX_HEADS = 32
IDX_DIM = 64
TOPK_MAX = 256
B_HEADS = 16
N_BUCKETS = 32
MAX_DISTANCE = 128
N_GROUPS = 4
EXPERTS_PER_GROUP = 8
N_EXPERTS = N_GROUPS * EXPERTS_PER_GROUP
TOP_K_IN_GROUP = 2
EPS = 1e-6

A_WIDTH = A_HEADS * HEAD_DIM
A_KV_WIDTH = A_KV_HEADS * HEAD_DIM
B_WIDTH = B_HEADS * HEAD_DIM
IDX_WIDTH = IDX_HEADS * IDX_DIM
SMALL_WIDTH = 128

LANES = 128
V7X_VMEM_BYTES = 64 * 1024 * 1024
VMEM_LIMIT = 56 * 1024 * 1024

BF16 = jnp.bfloat16
F32 = jnp.float32
NEG_INF = float("-inf")
INT_MIN = -(2 ** 31)
KEY_NEG_INF = int(np.int32(np.float32(-np.inf).view(np.int32)) ^ np.int32(0x7FFFFFFF))
LOG2E = math.log2(math.e)
Q_SCALE = HEAD_DIM ** -0.5 * LOG2E


def _cparams(*sem):
    return pltpu.CompilerParams(dimension_semantics=sem, vmem_limit_bytes=VMEM_LIMIT)


def _matmul_kernel(*refs, n_a, silu_a, epilogue, out_scale, n_out):
    a_refs = refs[:n_a]
    w_ref = refs[n_a]
    rest = refs[n_a + 1:]
    o_refs = rest[len(rest) - n_out:]
    acc = None
    k0 = 0
    for a_ref in a_refs:
        a = a_ref[...]
        if silu_a:
            a = a * jax.nn.sigmoid(a)
        kk = a.shape[1]
        part = jnp.dot(a.astype(BF16), w_ref[k0:k0 + kk, :].astype(BF16),
                       preferred_element_type=F32)
        acc = part if acc is None else acc + part
        k0 += kk
    if epilogue == "bias":
        acc = acc + rest[0][...]
    elif epilogue == "gated_residual":
        acc = rest[0][...] + rest[1][...] * acc
    if out_scale is not None:
        acc = acc * out_scale
    for o_ref in o_refs:
        o_ref[...] = acc.astype(o_ref.dtype)


def _matmul(a_list, w, *, col_off=0, n=None, tm, tn, silu_a=False, bias=None, res=None,
            gate=None, out_dtypes=(F32,), out_scale=None):
    m = a_list[0].shape[0]
    k = w.shape[0]
    n = w.shape[1] - col_off if n is None else n
    assert m % tm == 0 and n % tn == 0 and col_off % tn == 0
    assert sum(a.shape[1] for a in a_list) == k
    cb = col_off // tn
    in_specs = [pl.BlockSpec((tm, a.shape[1]), lambda j, i: (i, 0)) for a in a_list]
    in_specs.append(pl.BlockSpec((k, tn), lambda j, i: (0, j + cb)))
    args = list(a_list) + [w]
    if bias is not None:
        epilogue = "bias"
        in_specs.append(pl.BlockSpec((1, tn), lambda j, i: (0, j)))
        args.append(bias)
    elif res is not None:
        epilogue = "gated_residual"
        in_specs.append(pl.BlockSpec((tm, tn), lambda j, i: (i, j)))
        args.append(res)
        if gate.shape[0] == 1:
            in_specs.append(pl.BlockSpec((1, tn), lambda j, i: (0, j)))
        else:
            in_specs.append(pl.BlockSpec((tm, tn), lambda j, i: (i, j)))
        args.append(gate)
    else:
        epilogue = None
    outs = pl.pallas_call(
        functools.partial(_matmul_kernel, n_a=len(a_list), silu_a=silu_a, epilogue=epilogue,
                          out_scale=out_scale, n_out=len(out_dtypes)),
        grid=(n // tn, m // tm),
        in_specs=in_specs,
        out_specs=[pl.BlockSpec((tm, tn), lambda j, i: (i, j)) for _ in out_dtypes],
        out_shape=[jax.ShapeDtypeStruct((m, n), dt) for dt in out_dtypes],
        compiler_params=_cparams("arbitrary", "arbitrary"),
    )(*args)
    return outs[0] if len(out_dtypes) == 1 else outs


def _rms(x, g):
    return x * lax.rsqrt(jnp.mean(x * x, axis=-1, keepdims=True) + EPS) * g


def _norm_mod_kernel(x_ref, g_ref, sc_ref, sh_ref, o_ref):
    y = _rms(x_ref[...], g_ref[...])
    o_ref[...] = (y * (1.0 + sc_ref[0]) + sh_ref[0]).astype(o_ref.dtype)


def _norm_mod(x, g, sc, sh, *, rows_per_batch, tt, out_dtype):
    n, d = x.shape
    bpb = rows_per_batch // tt
    return pl.pallas_call(
        _norm_mod_kernel,
        grid=(n // tt,),
        in_specs=[pl.BlockSpec((tt, d), lambda i: (i, 0)),
                  pl.BlockSpec((1, d), lambda i: (0, 0)),
                  pl.BlockSpec((1, 1, d), lambda i: (i // bpb, 0, 0)),
                  pl.BlockSpec((1, 1, d), lambda i: (i // bpb, 0, 0))],
        out_specs=pl.BlockSpec((tt, d), lambda i: (i, 0)),
        out_shape=jax.ShapeDtypeStruct((n, d), out_dtype),
        compiler_params=_cparams("arbitrary"),
    )(x, g, sc, sh)


def _combine_kernel(x_ref, y0_ref, y1_ref, gates_ref, gt_ref, g_ref, o_ref):
    g0 = gates_ref[:, 0:1]
    g1 = gates_ref[:, 1:2]
    moe = y0_ref[...] * g0 + y1_ref[...] * g1
    x = x_ref[...] + gt_ref[0] * moe
    o_ref[...] = _rms(x, g_ref[...])


def _combine(x, y01, gates, gt, g_final, *, row_off, n_tok, rows_per_batch, tt):
    n, d = x.shape
    bpb = rows_per_batch // tt
    b0 = row_off // tt
    b1 = (n_tok + row_off) // tt
    row = pl.BlockSpec((tt, d), lambda i: (i, 0))
    return pl.pallas_call(
        _combine_kernel,
        grid=(n // tt,),
        in_specs=[row,
                  pl.BlockSpec((tt, d), lambda i: (i + b0, 0)),
                  pl.BlockSpec((tt, d), lambda i: (i + b1, 0)),
                  pl.BlockSpec((tt, 2), lambda i: (i, 0)),
                  pl.BlockSpec((1, 1, d), lambda i: (i // bpb, 0, 0)),
                  pl.BlockSpec((1, d), lambda i: (0, 0))],
        out_specs=row,
        out_shape=jax.ShapeDtypeStruct((n, d), F32),
        compiler_params=_cparams("arbitrary"),
    )(x, y01, y01, gates, gt, g_final)


GATHER_ROWS = 256


def _gather_kernel(idx_ref, nlive_ref, *refs, src_rows):
    n_src = len(src_rows)
    src_refs = refs[:n_src]
    o_ref, buf_ref, sem = refs[n_src:]
    i = pl.program_id(0)

    @pl.when(i < nlive_ref[0])
    def _():
        base = i * GATHER_ROWS

        def row_copy(src_ref, r, src_row):
            return pltpu.make_async_copy(src_ref.at[pl.ds(src_row, 1)], buf_ref.at[pl.ds(r, 1)], sem)

        def start(r, carry):
            row = idx_ref[base + r]
            lo = 0
            for k, src_ref in enumerate(src_refs):
                hi = lo + src_rows[k]
                in_range = row >= lo if k == n_src - 1 else jnp.logical_and(row >= lo, row < hi)
                if n_src == 1:
                    row_copy(src_ref, r, row).start()
                else:
                    @pl.when(in_range)
                    def _():
                        row_copy(src_ref, r, row - lo).start()
                lo = hi
            return carry

        def wait(r, carry):
            row_copy(src_refs[0], r, 0).wait()
            return carry

        lax.fori_loop(0, GATHER_ROWS, start, 0)
        lax.fori_loop(0, GATHER_ROWS, wait, 0)
        if len(o_ref.shape) == 2:
            o_ref[...] = buf_ref[...].astype(o_ref.dtype)
        else:
            w = o_ref.shape[2]
            for k in range(o_ref.shape[0]):
                o_ref[k] = buf_ref[:, k * w:(k + 1) * w].astype(o_ref.dtype)


def _gather_rows(srcs, idx, n_live_blocks, out_dtype, n_slabs=None):
    nr = idx.shape[0]
    src = srcs[0]
    d = src.shape[1]
    assert nr % GATHER_ROWS == 0
    blk = lambda i, idx, nl: jnp.minimum(i, nl[0] - 1)
    if n_slabs is None:
        out_spec = pl.BlockSpec((GATHER_ROWS, d), lambda i, idx, nl: (blk(i, idx, nl), 0))
        out_shape = (nr, d)
    else:
        out_spec = pl.BlockSpec((n_slabs, GATHER_ROWS, d // n_slabs),
                                lambda i, idx, nl: (0, blk(i, idx, nl), 0))
        out_shape = (n_slabs, nr, d // n_slabs)
    grid_spec = pltpu.PrefetchScalarGridSpec(
        num_scalar_prefetch=2,
        grid=(nr // GATHER_ROWS,),
        in_specs=[pl.BlockSpec(memory_space=pl.ANY) for _ in srcs],
        out_specs=out_spec,
        scratch_shapes=[pltpu.VMEM((GATHER_ROWS, d), src.dtype), pltpu.SemaphoreType.DMA(())],
    )
    return pl.pallas_call(
        functools.partial(_gather_kernel, src_rows=tuple(s.shape[0] for s in srcs)),
        grid_spec=grid_spec,
        out_shape=jax.ShapeDtypeStruct(out_shape, out_dtype),
        compiler_params=_cparams("arbitrary"),
    )(idx, n_live_blocks, *srcs)


def _lane_prefix_sum(x, lane):
    s = 1
    while s < LANES:
        x = x + jnp.where(lane >= s, pltpu.roll(x, s, axis=1), 0.0)
        s *= 2
    return x


def _logf_cumsum_kernel(*refs, p_len, t_pad):
    if p_len:
        fl_ref, bf_ref, past_ref, logf_ref, cum_ref = refs
    else:
        fl_ref, bf_ref, logf_ref, cum_ref = refs
        past_ref = None
    lane = lax.broadcasted_iota(jnp.int32, (B_HEADS, LANES), 1)
    carry = jnp.zeros((B_HEADS, 1), F32)
    for c in range(p_len // LANES):
        sl = slice(c * LANES, (c + 1) * LANES)
        y = _lane_prefix_sum(past_ref[0, :, sl], lane) + carry
        cum_ref[0, :, sl] = y
        carry = y[:, LANES - 1:LANES]
    for c in range(t_pad // LANES):
        sl = slice(c * LANES, (c + 1) * LANES)
        z = fl_ref[0, :, sl] + bf_ref[...]
        logf = jnp.minimum(z, 0.0) - jnp.log1p(jnp.exp(-jnp.abs(z)))
        logf_ref[0, :, sl] = logf
        y = _lane_prefix_sum(logf, lane) + carry
        cum_ref[0, :, p_len + c * LANES:p_len + (c + 1) * LANES] = y
        carry = y[:, LANES - 1:LANES]


def _logf_cumsum(fl_t, b_forget, past_t):
    b, _, t_pad = fl_t.shape
    p_len = 0 if past_t is None else past_t.shape[2]
    in_specs = [pl.BlockSpec((1, B_HEADS, t_pad), lambda i: (i, 0, 0)),
                pl.BlockSpec((B_HEADS, 1), lambda i: (0, 0))]
    args = [fl_t, b_forget]
    if p_len:
        in_specs.append(pl.BlockSpec((1, B_HEADS, p_len), lambda i: (i, 0, 0)))
        args.append(past_t)
    return pl.pallas_call(
        functools.partial(_logf_cumsum_kernel, p_len=p_len, t_pad=t_pad),
        grid=(b,),
        in_specs=in_specs,
        out_specs=[pl.BlockSpec((1, B_HEADS, t_pad), lambda i: (i, 0, 0)),
                   pl.BlockSpec((1, B_HEADS, p_len + t_pad), lambda i: (i, 0, 0))],
        out_shape=[jax.ShapeDtypeStruct((b, B_HEADS, t_pad), F32),
                   jax.ShapeDtypeStruct((b, B_HEADS, p_len + t_pad), F32)],
        compiler_params=_cparams("arbitrary"),
    )(*args)


BISECT_ROWS = 128


def _indexer_kernel(qi_ref, small_ref, klo_ref, khi_ref, mask_ref, qh_ref, wb_ref, keys_ref, *,
                    tq, tkc, n_chunks, l_real, p_len, nq_per_batch, n_sel):
    i = pl.program_id(0)
    row0 = (i % nq_per_batch) * tq
    n_pairs = IDX_HEADS // 2
    sub = tkc // LANES if tkc % LANES == 0 else 0
    for b in range(n_pairs):
        qh_ref[b * tq:(b + 1) * tq, :] = qi_ref[:, b * LANES:(b + 1) * LANES]
    w = small_ref[:, IDX_DIM:IDX_DIM + IDX_HEADS] * (IDX_HEADS ** -0.5)
    w = w * (IDX_DIM ** -0.5)
    wcols = LANES if sub else tkc
    for h in range(IDX_HEADS):
        wb_ref[h] = jnp.broadcast_to(w[:, h:h + 1], (tq, wcols))
    qpos = p_len + row0 + lax.broadcasted_iota(jnp.int32, (tq, 1), 0)
    q_chunk = qpos // CHUNK
    last_allowed = (p_len + row0 + tq - 1) // CHUNK * CHUNK + CHUNK - 1
    n_allowed = jnp.minimum(last_allowed // tkc + 1, n_chunks)

    def allowed_of(c0):
        kpos = c0 + lax.broadcasted_iota(jnp.int32, (tq, tkc), 1)
        return jnp.logical_and(kpos // CHUNK <= q_chunk, kpos < l_real)

    def weighted(s, h):
        r = jnp.maximum(s, 0.0)
        if not sub:
            return r * wb_ref[h]
        wv = wb_ref[h]
        return jnp.concatenate([r[:, u * LANES:(u + 1) * LANES] * wv for u in range(sub)], axis=1)

    for c in range(n_chunks):
        c0 = c * tkc

        @pl.when(c < n_allowed)
        def _():
            q_all = qh_ref[...]
            s_lo = jnp.dot(q_all, klo_ref[0, :, c0:c0 + tkc], preferred_element_type=F32)
            s_hi = jnp.dot(q_all, khi_ref[0, :, c0:c0 + tkc], preferred_element_type=F32)
            score = jnp.zeros((tq, tkc), F32)
            for b in range(n_pairs):
                score = score + weighted(s_lo[b * tq:(b + 1) * tq, :], 2 * b)
                score = score + weighted(s_hi[b * tq:(b + 1) * tq, :], 2 * b + 1)
            score = score + 0.0
            score = jnp.where(allowed_of(c0), score, NEG_INF)
            bits = lax.bitcast_convert_type(score, jnp.int32)
            keys_ref[c] = jnp.where(bits < 0, bits ^ 0x7FFFFFFF, bits)

    def count(pred_of_chunk, r0=0, nr=tq):
        def body(c, acc):
            ind = jnp.where(pred_of_chunk(keys_ref[c, r0:r0 + nr, :], c), 1.0, 0.0)
            if not sub:
                return acc + ind
            for u in range(sub):
                acc = acc + ind[:, u * LANES:(u + 1) * LANES]
            return acc

        acc = lax.fori_loop(0, n_allowed, body, jnp.zeros((nr, wcols), F32))
        return jnp.sum(acc, axis=1, keepdims=True)

    k_f = float(n_sel)
    slab = min(tq, BISECT_ROWS)
    t_parts = []
    for r0 in range(0, tq, slab):
        cnt = functools.partial(count, r0=r0, nr=slab)
        t0 = jnp.where(cnt(lambda kc, c: kc >= 0) >= k_f, 0, INT_MIN).astype(jnp.int32)

        def bisect(it, t, cnt=cnt):
            cand = t + jnp.left_shift(jnp.int32(1), 30 - it)
            return jnp.where(cnt(lambda kc, c: kc >= cand) >= k_f, cand, t)

        t_parts.append(lax.fori_loop(0, 31, bisect, t0))
    t = t_parts[0] if len(t_parts) == 1 else jnp.concatenate(t_parts, axis=0)
    cge = count(lambda kc, c: kc >= t)
    excess = jnp.max(jnp.where(t > KEY_NEG_INF, cge, 0.0)) > k_f

    def emit(sel_of_chunk):
        for c in range(n_chunks):
            @pl.when(c < n_allowed)
            def _():
                sel = jnp.logical_and(sel_of_chunk(keys_ref[c], c), allowed_of(c * tkc))
                mask_ref[c] = jnp.where(sel, 0.0, NEG_INF).astype(mask_ref.dtype)

            @pl.when(c >= n_allowed)
            def _():
                mask_ref[c] = jnp.full((tq, tkc), NEG_INF, mask_ref.dtype)

    @pl.when(jnp.logical_not(excess))
    def _():
        emit(lambda kc, c: kc >= t)

    @pl.when(excess)
    def _():
        need = k_f - count(lambda kc, c: kc > t)
        lane_idx = lax.broadcasted_iota(jnp.int32, (tq, tkc), 1)
        m = jnp.zeros((tq, 1), jnp.int32)
        for bit in reversed(range(max(1, (n_chunks * tkc - 1).bit_length()))):
            cand = m + (1 << bit)
            c_lt = count(lambda kc, c: jnp.logical_and(kc == t, c * tkc + lane_idx < cand))
            m = jnp.where(c_lt < need, cand, m)
        emit(lambda kc, c: jnp.logical_or(
            kc > t, jnp.logical_and(kc == t, c * tkc + lane_idx <= m)))


def _indexer_mask(qi, small, k_lo, k_hi, *, rows_per_batch, tq, tkc, l_real, p_len):
    n = qi.shape[0]
    l_pad = k_lo.shape[2]
    n_chunks = l_pad // tkc
    nqb = rows_per_batch // tq
    n_sel = min(TOPK_MAX, l_real // 4)
    wcols = LANES if tkc % LANES == 0 else tkc
    kern = functools.partial(_indexer_kernel, tq=tq, tkc=tkc, n_chunks=n_chunks, l_real=l_real,
                             p_len=p_len, nq_per_batch=nqb, n_sel=n_sel)
    kspec = pl.BlockSpec((1, LANES, l_pad), lambda i: (i // nqb, 0, 0))
    return pl.pallas_call(
        kern,
        grid=(n // tq,),
        in_specs=[pl.BlockSpec((tq, IDX_WIDTH), lambda i: (i, 0)),
                  pl.BlockSpec((tq, SMALL_WIDTH), lambda i: (i, 0)),
                  kspec, kspec],
        out_specs=pl.BlockSpec((n_chunks, tq, tkc), lambda i: (0, i, 0)),
        out_shape=jax.ShapeDtypeStruct((n_chunks, n, tkc), BF16),
        scratch_shapes=[pltpu.VMEM((IDX_HEADS // 2 * tq, LANES), BF16),
                        pltpu.VMEM((IDX_HEADS, tq, wcols), F32),
                        pltpu.VMEM((n_chunks, tq, tkc), jnp.int32)],
        compiler_params=_cparams("arbitrary"),
    )(qi, small, k_lo, k_hi)


def _softmax_step(s, carry, v_tile):
    m, l, acc = carry
    m_new = jnp.maximum(m, jnp.max(s, axis=1, keepdims=True))
    m_safe = jnp.where(m_new == NEG_INF, 0.0, m_new)
    alpha = jnp.exp2(m - m_safe)
    p = jnp.exp2(s - m_safe)
    l = alpha * l + jnp.sum(p, axis=1, keepdims=True)
    acc = alpha * acc + jnp.dot(p.astype(BF16), v_tile, preferred_element_type=F32)
    return m_new, l, acc


def _softmax_init(rows):
    return (jnp.full((rows, 1), NEG_INF, F32), jnp.zeros((rows, 1), F32),
            jnp.zeros((rows, HEAD_DIM), F32))


def _qk(q, k_tile):
    return lax.dot_general(q, k_tile, (((1,), (1,)), ((), ())), preferred_element_type=F32)


def _fox_prompt_kernel(q_ref, k_ref, v_ref, fk_ref, o_ref, *, t_tile, n_heads, nt):
    i = pl.program_id(1)
    lanes = lambda g: slice(g * HEAD_DIM, (g + 1) * HEAD_DIM)
    qs = [q_ref[:, lanes(g)] for g in range(n_heads)]

    def tile(j, carries, diagonal):
        k0 = pl.multiple_of(j * t_tile, t_tile)
        out = []
        for g in range(n_heads):
            s = _qk(qs[g], k_ref[pl.ds(k0, t_tile), lanes(g)]) - fk_ref[g * nt + j] * LOG2E
            if diagonal:
                row = lax.broadcasted_iota(jnp.int32, (t_tile, t_tile), 0)
                col = lax.broadcasted_iota(jnp.int32, (t_tile, t_tile), 1)
                s = jnp.where(col <= row, s, NEG_INF)
            out.append(_softmax_step(s, carries[g], v_ref[pl.ds(k0, t_tile), lanes(g)]))
        return tuple(out)

    init = tuple(_softmax_init(t_tile) for _ in range(n_heads))
    carries = lax.fori_loop(0, i, lambda j, c: tile(j, c, False), init)
    carries = tile(i, carries, True)
    for g in range(n_heads):
        _, l, acc = carries[g]
        o_ref[:, lanes(g)] = (acc / l).astype(o_ref.dtype)


def _fox_prompt(q, k, v, fk_tiles, *, t_tile, heads_per_step):
    t, width = q.shape
    heads = width // HEAD_DIM
    nt = t // t_tile
    gw = heads_per_step * HEAD_DIM
    resident = dict(pipeline_mode=pl.Buffered(1))
    return pl.pallas_call(
        functools.partial(_fox_prompt_kernel, t_tile=t_tile, n_heads=heads_per_step, nt=nt),
        grid=(heads // heads_per_step, nt),
        in_specs=[pl.BlockSpec((t_tile, gw), lambda h, i: (i, h)),
                  pl.BlockSpec((t, gw), lambda h, i: (0, h), **resident),
                  pl.BlockSpec((t, gw), lambda h, i: (0, h), **resident),
                  pl.BlockSpec((heads_per_step * nt, 1, t_tile), lambda h, i: (h, 0, 0))],
        out_specs=pl.BlockSpec((t_tile, gw), lambda h, i: (i, h)),
        out_shape=jax.ShapeDtypeStruct((t, width), BF16),
        compiler_params=_cparams("arbitrary", "arbitrary"),
    )(q, k, v, fk_tiles)


def _dsa_prompt_kernel(q_ref, k_ref, v_ref, mask_ref, bias_ref, o_ref, *, t_tile, grp, mask_w):
    i = pl.program_id(1)
    q = jnp.concatenate([q_ref[:, j * HEAD_DIM:(j + 1) * HEAD_DIM] for j in range(grp)], axis=0)
    n_sub = t_tile // mask_w

    def tile(j, carry, bias_cls):
        k0 = pl.multiple_of(j * t_tile, t_tile)
        s = _qk(q, k_ref[pl.ds(k0, t_tile), :]).reshape(grp, t_tile, t_tile)
        madd = jnp.concatenate([mask_ref[j * n_sub + u] for u in range(n_sub)], axis=1)
        s = s + madd.astype(F32)[None]
        if bias_cls is not None:
            s = s + bias_ref[bias_cls]
        return _softmax_step(s.reshape(grp * t_tile, t_tile), carry, v_ref[pl.ds(k0, t_tile), :])

    carry = lax.fori_loop(0, jnp.maximum(i - 1, 0), lambda j, c: tile(j, c, None),
                          _softmax_init(grp * t_tile))
    carry = lax.fori_loop(jnp.maximum(i - 1, 0), i, lambda j, c: tile(j, c, 1), carry)
    _, l, acc = tile(i, carry, 0)
    out = acc / l
    for j in range(grp):
        o_ref[:, j * HEAD_DIM:(j + 1) * HEAD_DIM] = out[j * t_tile:(j + 1) * t_tile].astype(o_ref.dtype)


def _dsa_prompt(q, k, v, mask, bias, *, t_tile):
    t, width = q.shape
    kv_heads = k.shape[1] // HEAD_DIM
    grp = width // HEAD_DIM // kv_heads
    nt = t // t_tile
    n_mask, _, mask_w = mask.shape
    resident = dict(pipeline_mode=pl.Buffered(1))
    return pl.pallas_call(
        functools.partial(_dsa_prompt_kernel, t_tile=t_tile, grp=grp, mask_w=mask_w),
        grid=(kv_heads, nt),
        in_specs=[pl.BlockSpec((t_tile, grp * HEAD_DIM), lambda g, i: (i, g)),
                  pl.BlockSpec((t, HEAD_DIM), lambda g, i: (0, g), **resident),
                  pl.BlockSpec((t, HEAD_DIM), lambda g, i: (0, g), **resident),
                  pl.BlockSpec((n_mask, t_tile, mask_w), lambda g, i: (0, i, 0)),
                  pl.BlockSpec((2, grp, t_tile, t_tile), lambda g, i: (0, g, 0, 0), **resident)],
        out_specs=pl.BlockSpec((t_tile, grp * HEAD_DIM), lambda g, i: (i, g)),
        out_shape=jax.ShapeDtypeStruct((t, width), BF16),
        compiler_params=_cparams("arbitrary", "arbitrary"),
    )(q, k, v, mask, bias)


def _attend(q_ref, k_of, v_of, bias_of_head, m_ref, l_ref, acc_ref, *, tq, kv_heads, grp):
    for g in range(kv_heads):
        kg = k_of(g).astype(BF16)
        vg = v_of(g).astype(BF16)
        qs = [q_ref[:, (g * grp + j) * HEAD_DIM:(g * grp + j + 1) * HEAD_DIM] for j in range(grp)]
        qg = qs[0] if grp == 1 else jnp.concatenate(qs, axis=0)
        s_all = _qk(qg, kg)
        for j in range(grp):
            h = g * grp + j
            s = s_all[j * tq:(j + 1) * tq, :] + bias_of_head(h)
            m_ref[h], l_ref[h], acc_ref[h] = _softmax_step(s, (m_ref[h], l_ref[h], acc_ref[h]), vg)


def _flash_kernel(*refs, mode, n_a, tq, tk_a, tk_b, p_len, kv_heads, grp):
    heads = kv_heads * grp
    it = iter(refs)
    q_ref = next(it)
    ka_ref, va_ref, kb_ref, vb_ref = next(it), next(it), next(it), next(it)
    if mode == "dsa":
        maska_ref, biasa_ref, maskb_ref, biasb_ref = next(it), next(it), next(it), next(it)
    else:
        fq_ref, fka_ref, fkb_ref = next(it), next(it), next(it)
    o_ref, m_ref, l_ref, acc_ref = next(it), next(it), next(it), next(it)
    j = pl.program_id(1)

    @pl.when(j == 0)
    def _():
        m_ref[...] = jnp.full(m_ref.shape, NEG_INF, F32)
        l_ref[...] = jnp.zeros(l_ref.shape, F32)
        acc_ref[...] = jnp.zeros(acc_ref.shape, F32)

    common = dict(tq=tq, kv_heads=kv_heads, grp=grp)

    @pl.when(j < n_a)
    def _():
        if mode == "dsa":
            madd = maska_ref[...].astype(F32)
            bias = lambda h: biasa_ref[0, h] + madd
        else:
            bias = lambda h: (fq_ref[:, h:h + 1] - fka_ref[0, h:h + 1, :]) * LOG2E
        k_of = lambda g: ka_ref[0, pl.ds(g, tk_a, stride=kv_heads), :]
        v_of = lambda g: va_ref[0, pl.ds(g, tk_a, stride=kv_heads), :]
        _attend(q_ref, k_of, v_of, bias, m_ref, l_ref, acc_ref, **common)

    @pl.when(j == n_a)
    def _():
        if mode == "dsa":
            madd = maskb_ref[...].astype(F32)
            bias = lambda h: biasb_ref[0, h] + madd
        else:
            row = lax.broadcasted_iota(jnp.int32, (tq, tk_b), 0)
            col = lax.broadcasted_iota(jnp.int32, (tq, tk_b), 1)
            causal = jnp.where(col <= row, 0.0, NEG_INF)
            bias = lambda h: (fq_ref[:, h:h + 1] - fkb_ref[0, h:h + 1, :tk_b]) * LOG2E + causal
        k_of = lambda g: kb_ref[:, g * HEAD_DIM:(g + 1) * HEAD_DIM]
        v_of = lambda g: vb_ref[:, g * HEAD_DIM:(g + 1) * HEAD_DIM]
        _attend(q_ref, k_of, v_of, bias, m_ref, l_ref, acc_ref, **common)
        for h in range(heads):
            o_ref[:, h * HEAD_DIM:(h + 1) * HEAD_DIM] = (acc_ref[h] / l_ref[h]).astype(o_ref.dtype)


def _flash_sample(mode, q, k_new, v_new, k_past, v_past, *, tk_a, mask_a=None, bias_a=None,
                  mask_b=None, bias_b=None, fq=None, fk_t=None):
    n, width = q.shape
    kv_w = k_new.shape[1]
    b = k_past.shape[0]
    tq = n // b
    heads = width // HEAD_DIM
    kv_heads = kv_w // HEAD_DIM
    p_len = k_past.shape[1] // kv_heads
    grp = heads // kv_heads
    n_a = p_len // tk_a
    a_blk = lambda j: jnp.minimum(j, n_a - 1)

    in_specs = [pl.BlockSpec((tq, width), lambda g, j: (g, 0))]
    spec = pl.BlockSpec((1, tk_a * kv_heads, HEAD_DIM), lambda g, j: (g, a_blk(j), 0))
    in_specs += [spec, spec]
    spec = pl.BlockSpec((tq, kv_w), lambda g, j: (g, 0))
    in_specs += [spec, spec]
    args = [q, k_past, v_past, k_new, v_new]
    if mode == "dsa":
        in_specs += [pl.BlockSpec((tq, tk_a), lambda g, j: (g, a_blk(j))),
                     pl.BlockSpec((1, heads, tq, tk_a), lambda g, j: (a_blk(j), 0, 0, 0)),
                     pl.BlockSpec((tq, tq), lambda g, j: (g, 0)),
                     pl.BlockSpec((1, heads, tq, tq), lambda g, j: (0, 0, 0, 0))]
        args += [mask_a, bias_a, mask_b, bias_b]
    else:
        in_specs += [pl.BlockSpec((tq, heads), lambda g, j: (g, 0)),
                     pl.BlockSpec((1, heads, tk_a), lambda g, j: (g, 0, a_blk(j))),
                     pl.BlockSpec((1, heads, LANES), lambda g, j: (g, 0, p_len // LANES))]
        args += [fq, fk_t, fk_t]

    kern = functools.partial(_flash_kernel, mode=mode, n_a=n_a, tq=tq, tk_a=tk_a, tk_b=tq,
                             p_len=p_len, kv_heads=kv_heads, grp=grp)
    return pl.pallas_call(
        kern,
        grid=(b, n_a + 1),
        in_specs=in_specs,
        out_specs=pl.BlockSpec((tq, width), lambda g, j: (g, 0)),
        out_shape=jax.ShapeDtypeStruct((n, width), BF16),
        scratch_shapes=[pltpu.VMEM((heads, tq, 1), F32),
                        pltpu.VMEM((heads, tq, 1), F32),
                        pltpu.VMEM((heads, tq, HEAD_DIM), F32)],
        compiler_params=_cparams("arbitrary", "arbitrary"),
    )(*args)


def _router_kernel(lt_ref, eidx_ref, gate_ref):
    row = lambda r: lt_ref[r:r + 1, :]
    g = [row(r) for r in range(N_GROUPS)]
    gmax = functools.reduce(jnp.maximum, g)
    gsel = jnp.full(gmax.shape, N_GROUPS - 1, jnp.int32)
    for r in reversed(range(N_GROUPS - 1)):
        gsel = jnp.where(g[r] == gmax, r, gsel)
    denom = functools.reduce(lambda a, b: a + b, [jnp.exp(x - gmax) for x in g])
    g_prob = 1.0 / denom
    e_in = []
    for kk in range(EXPERTS_PER_GROUP):
        v = row(N_GROUPS + (N_GROUPS - 1) * EXPERTS_PER_GROUP + kk)
        for r in reversed(range(N_GROUPS - 1)):
            v = jnp.where(gsel == r, row(N_GROUPS + r * EXPERTS_PER_GROUP + kk), v)
        e_in.append(v)
    v1 = functools.reduce(jnp.maximum, e_in)
    i1 = jnp.full(v1.shape, EXPERTS_PER_GROUP - 1, jnp.int32)
    for kk in reversed(range(EXPERTS_PER_GROUP - 1)):
        i1 = jnp.where(e_in[kk] == v1, kk, i1)
    rest = [jnp.where(i1 == kk, NEG_INF, e_in[kk]) for kk in range(EXPERTS_PER_GROUP)]
    v2 = functools.reduce(jnp.maximum, rest)
    i2 = jnp.full(v2.shape, EXPERTS_PER_GROUP - 1, jnp.int32)
    for kk in reversed(range(EXPERTS_PER_GROUP - 1)):
        i2 = jnp.where(jnp.logical_and(rest[kk] == v2, i1 != kk), kk, i2)
    e2 = jnp.exp(v2 - v1)
    inv = 1.0 / (1.0 + e2)
    eidx_ref[0:1, :] = gsel * EXPERTS_PER_GROUP + i1
    eidx_ref[1:2, :] = gsel * EXPERTS_PER_GROUP + i2
    gate_ref[0:1, :] = inv * g_prob
    gate_ref[1:2, :] = (e2 * inv) * g_prob


def _router(logits_t):
    n = logits_t.shape[1]
    full = lambda r: pl.BlockSpec((r, n), lambda: (0, 0))
    return pl.pallas_call(
        _router_kernel,
        in_specs=[full(logits_t.shape[0])],
        out_specs=[full(TOP_K_IN_GROUP), full(TOP_K_IN_GROUP)],
        out_shape=[jax.ShapeDtypeStruct((TOP_K_IN_GROUP, n), jnp.int32),
                   jax.ShapeDtypeStruct((TOP_K_IN_GROUP, n), F32)],
        compiler_params=pltpu.CompilerParams(vmem_limit_bytes=VMEM_LIMIT),
    )(logits_t)


EXPERT_CHUNK = 256
EXPERT_CHUNKS_PER_BLOCK = 4
EXPERT_ROWS = EXPERT_CHUNK * EXPERT_CHUNKS_PER_BLOCK
EXPERT_K_TILE = 512
EXPERT_N_TILE = 1024


def _experts_kernel(sb_e_ref, sb_row0_ref, sb_nch_ref, sb_blk_ref, x_hbm, w1_ref, w3_ref, w2_ref,
                    o_ref, x_ref, a_ref, b_ref, act_ref, sem, *, nk):
    del sb_e_ref, sb_blk_ref
    s = pl.program_id(0)
    t = pl.program_id(1)
    n_ch = sb_nch_ref[s]

    @pl.when(jnp.logical_and(t == 0, n_ch > 0))
    def _():
        row0 = pl.multiple_of(sb_row0_ref[s], EXPERT_CHUNK)
        copies = [pltpu.make_async_copy(x_hbm.at[k, pl.ds(row0, EXPERT_ROWS)], x_ref.at[k], sem)
                  for k in range(nk)]
        for cp in copies:
            cp.start()
        for cp in copies:
            cp.wait()

    for nc in range(1, EXPERT_CHUNKS_PER_BLOCK + 1):
        rows = nc * EXPERT_CHUNK

        @pl.when(jnp.logical_and(t < nk, n_ch == nc))
        def _():
            xc = x_ref[t, 0:rows, :]
            a = jnp.dot(xc, w1_ref[0].astype(BF16), preferred_element_type=F32)
            b = jnp.dot(xc, w3_ref[0].astype(BF16), preferred_element_type=F32)

            @pl.when(t == 0)
            def _():
                a_ref[0:rows, :] = a
                b_ref[0:rows, :] = b

            @pl.when(t > 0)
            def _():
                a_ref[0:rows, :] += a
                b_ref[0:rows, :] += b

            @pl.when(t == nk - 1)
            def _():
                g = a_ref[0:rows, :]
                act_ref[0:rows, :] = (g * jax.nn.sigmoid(g) * b_ref[0:rows, :]).astype(BF16)

        @pl.when(jnp.logical_and(t >= nk, n_ch == nc))
        def _():
            o_ref[0:rows, :] = jnp.dot(act_ref[0:rows, :], w2_ref[0].astype(BF16),
                                       preferred_element_type=F32)


def _experts(xs, sb_expert, sb_row0, sb_nch, sb_blk, w1, w3, w2, n_sb):
    nk = xs.shape[0]
    d = nk * EXPERT_K_TILE
    d_e = w1.shape[2]
    nn = d // EXPERT_N_TILE
    r = EXPERT_ROWS
    grid_spec = pltpu.PrefetchScalarGridSpec(
        num_scalar_prefetch=4,
        grid=(n_sb, nk + nn),
        in_specs=[
            pl.BlockSpec(memory_space=pl.ANY),
            pl.BlockSpec((1, EXPERT_K_TILE, d_e),
                         lambda s, t, e, r0, nch, blk: (e[s], jnp.minimum(t, nk - 1), 0)),
            pl.BlockSpec((1, EXPERT_K_TILE, d_e),
                         lambda s, t, e, r0, nch, blk: (e[s], jnp.minimum(t, nk - 1), 0)),
            pl.BlockSpec((1, d_e, EXPERT_N_TILE),
                         lambda s, t, e, r0, nch, blk: (e[s], 0, jnp.maximum(t - nk, 0))),
        ],
        out_specs=pl.BlockSpec(
            (r, EXPERT_N_TILE),
            lambda s, t, e, r0, nch, blk: (blk[s], jnp.where(nch[s] > 0, jnp.maximum(t - nk, 0), 0))),
        scratch_shapes=[pltpu.VMEM((nk, r, EXPERT_K_TILE), BF16),
                        pltpu.VMEM((r, d_e), F32),
                        pltpu.VMEM((r, d_e), F32),
                        pltpu.VMEM((r, d_e), BF16),
                        pltpu.SemaphoreType.DMA(())],
    )
    return pl.pallas_call(
        functools.partial(_experts_kernel, nk=nk),
        grid_spec=grid_spec,
        out_shape=jax.ShapeDtypeStruct(((n_sb + 1) * r, d), F32),
        compiler_params=_cparams("arbitrary", "arbitrary"),
    )(sb_expert, sb_row0, sb_nch, sb_blk, xs, w1, w3, w2)


def _dispatch(expert, n_sb, n_x_rows):
    n = expert.shape[0]
    a = n * TOP_K_IN_GROUP
    ch, r = EXPERT_CHUNK, EXPERT_ROWS
    flat_e = expert.reshape(a)
    flat_tok = jnp.arange(a, dtype=jnp.int32) // TOP_K_IN_GROUP
    order = jnp.argsort(flat_e)
    se = flat_e[order]
    counts = jnp.bincount(flat_e, length=N_EXPERTS).astype(jnp.int32)
    seg_start = jnp.cumsum(counts) - counts
    rank = jnp.arange(a, dtype=jnp.int32) - seg_start[se]
    nch_e = (counts + ch - 1) // ch
    xch_end = jnp.cumsum(nch_e)
    x_start = (xch_end - nch_e) * ch
    nsb_e = (counts + r - 1) // r
    sb_end = jnp.cumsum(nsb_e)
    sb_start = sb_end - nsb_e
    x_row_tok = jnp.zeros((n_x_rows,), jnp.int32).at[x_start[se] + rank].set(flat_tok[order])
    n_live_x_blocks = (xch_end[-1] * ch + GATHER_ROWS - 1) // GATHER_ROWS
    y_row = jnp.zeros((a,), jnp.int32).at[order].set(sb_start[se] * r + rank)
    y_row = y_row.reshape(n, TOP_K_IN_GROUP)
    total = sb_end[-1]
    s_idx = jnp.arange(n_sb, dtype=jnp.int32)
    e_of_sb = jnp.minimum(jnp.searchsorted(sb_end, s_idx, side="right"), N_EXPERTS - 1).astype(jnp.int32)
    k_in_e = s_idx - sb_start[e_of_sb]
    rows_of_sb = jnp.clip(counts[e_of_sb] - k_in_e * r, 0, r)
    live = s_idx < total
    last_e = e_of_sb[jnp.maximum(total - 1, 0)]
    sb_expert = jnp.where(live, e_of_sb, last_e).astype(jnp.int32)
    sb_nch = jnp.where(live, (rows_of_sb + ch - 1) // ch, 0).astype(jnp.int32)
    sb_row0 = jnp.where(live, x_start[e_of_sb] + k_in_e * r, 0).astype(jnp.int32)
    sb_blk = jnp.where(live, s_idx, n_sb).astype(jnp.int32)
    return (x_row_tok, n_live_x_blocks.astype(jnp.int32).reshape(1), y_row, sb_expert, sb_row0,
            sb_nch, sb_blk)


def _t5_bucket(rel):
    half = N_BUCKETS // 2
    max_exact = half // 2
    bucket = jnp.where(rel > 0, half, 0).astype(jnp.int32)
    n = jnp.abs(rel)
    large = max_exact + (jnp.log(jnp.maximum(n, 1).astype(jnp.float32) / max_exact)
                         / math.log(MAX_DISTANCE / max_exact) * (half - max_exact)).astype(jnp.int32)
    large = jnp.minimum(large, half - 1)
    return bucket + jnp.where(n < max_exact, n, large).astype(jnp.int32)


def _bias_tiles(rel_bias, rel0_list, tq, tk):
    span = tq + tk
    tiles = []
    for rel0 in rel0_list:
        rel = rel0 - (tq - 1) + jnp.arange(span, dtype=jnp.int32)
        table = jnp.transpose(rel_bias[_t5_bucket(rel)].astype(F32))
        heads = table.shape[0]
        skew = jnp.tile(table, (1, tq + 1))[:, :tq * (span + 1)].reshape(heads, tq, span + 1)
        tiles.append(skew[:, ::-1, :tk])
    return jnp.stack(tiles, axis=0)


def _pick(n, candidates):
    for c in candidates:
        if n % c == 0:
            return c
    return n


INDEXER_TILE = 256
ATTN_TILE = 512
FOX_HEADS_PER_STEP = 4


def _mixer_half(x, mods, past, w_in_packed, b_forget, w_out, rel_bias, g_mix, g_ffn):
    b, t, d = x.shape
    n = b * t
    sh1, sc1, gt1, sh2, sc2, _ = mods
    xf = x.reshape(n, d)
    tt = _pick(t, (256, 128, 64, 32, 16))
    h = _norm_mod(xf, g_mix, sc1, sh1, rows_per_batch=t, tt=tt, out_dtype=BF16)

    tm = _pick(n, (1024, 512, 256, 128))

    def proj(col_off, width, tn, **kw):
        return _matmul([h], w_in_packed, col_off=col_off, n=width, tm=tm, tn=tn, **kw)

    off = 0
    qa = proj(off, A_WIDTH, 1024, out_dtypes=(BF16,), out_scale=Q_SCALE); off += A_WIDTH
    ka, ka16 = proj(off, A_KV_WIDTH, 512, out_dtypes=(F32, BF16)); off += A_KV_WIDTH
    va, va16 = proj(off, A_KV_WIDTH, 512, out_dtypes=(F32, BF16)); off += A_KV_WIDTH
    qi = proj(off, IDX_WIDTH, 1024, out_dtypes=(BF16,)); off += IDX_WIDTH
    qb = proj(off, B_WIDTH, 1024, out_dtypes=(BF16,), out_scale=Q_SCALE); off += B_WIDTH
    kb, kb16 = proj(off, B_WIDTH, 1024, out_dtypes=(F32, BF16)); off += B_WIDTH
    vb, vb16 = proj(off, B_WIDTH, 1024, out_dtypes=(F32, BF16)); off += B_WIDTH
    small = proj(off, SMALL_WIDTH, SMALL_WIDTH)

    ki = small[:, :IDX_DIM]
    fl = small[:, IDX_DIM + IDX_HEADS:IDX_DIM + IDX_HEADS + B_HEADS]

    p_len = 0 if past is None else past[0].shape[1]
    t_pad = -(-t // LANES) * LANES
    l_real = p_len + t
    l_pad = p_len + t_pad

    fl_t = jnp.transpose(fl.reshape(b, t, B_HEADS), (0, 2, 1))
    fl_t = jnp.pad(fl_t, ((0, 0), (0, 0), (0, t_pad - t)))
    past_logf_t = None if past is None else jnp.transpose(past[5], (0, 2, 1))
    logf_t, fcum_t = _logf_cumsum(fl_t, b_forget.reshape(B_HEADS, 1), past_logf_t)
    logf = jnp.transpose(logf_t[:, :, :t], (0, 2, 1))

    ki_b = ki.reshape(b, t, IDX_DIM)
    if past is not None:
        ki_b = jnp.concatenate([past[2], ki_b], axis=1)
    ki_t = jnp.transpose(jnp.pad(ki_b, ((0, 0), (0, l_pad - l_real), (0, 0))), (0, 2, 1)).astype(BF16)
    zeros = jnp.zeros_like(ki_t)
    k_lo = jnp.concatenate([ki_t, zeros], axis=1)
    k_hi = jnp.concatenate([zeros, ki_t], axis=1)

    if past is None:
        tile = ATTN_TILE
        assert b == 1 and t % tile == 0 and tile >= MAX_DISTANCE and tile % INDEXER_TILE == 0
        mask = _indexer_mask(qi, small, k_lo, k_hi, rows_per_batch=t, tq=INDEXER_TILE,
                             tkc=INDEXER_TILE, l_real=l_real, p_len=0)
        bias = _bias_tiles(rel_bias, [0, -tile], tile, tile)
        far = rel_bias[_t5_bucket(jnp.int32(-MAX_DISTANCE))].astype(F32)
        bias = (bias - far[None, :, None, None]) * LOG2E
        out_a = _dsa_prompt(qa, ka16, va16, mask, bias, t_tile=tile)
        fk_tiles = fcum_t.reshape(B_HEADS * (t // tile), 1, tile)
        out_b = _fox_prompt(qb, kb16, vb16, fk_tiles, t_tile=tile, heads_per_step=FOX_HEADS_PER_STEP)
    else:
        tk_a = _pick(p_len, (512, 256, 128))
        mask = _indexer_mask(qi, small, k_lo, k_hi, rows_per_batch=t, tq=t, tkc=l_pad,
                             l_real=l_real, p_len=p_len).reshape(n, l_pad)
        n_a = p_len // tk_a
        bias_a = _bias_tiles(rel_bias, [c * tk_a - p_len for c in range(n_a)], t, tk_a) * LOG2E
        bias_b = _bias_tiles(rel_bias, [0], t, t) * LOG2E
        cache = lambda a: a.reshape(b, -1, HEAD_DIM)
        out_a = _flash_sample("dsa", qa, ka16, va16, cache(past[0]), cache(past[1]), tk_a=tk_a,
                              mask_a=mask[:, :p_len], bias_a=bias_a,
                              mask_b=mask[:, p_len:p_len + t], bias_b=bias_b)
        fq = jnp.transpose(fcum_t[:, :, p_len:p_len + t], (0, 2, 1)).reshape(n, B_HEADS)
        out_b = _flash_sample("fox", qb, kb16, vb16, cache(past[3]), cache(past[4]), tk_a=tk_a,
                              fq=fq, fk_t=fcum_t)

    gate1 = gt1.reshape(1, d) if b == 1 else jnp.repeat(gt1.reshape(b, d), t, axis=0)
    x1 = _matmul([out_a, out_b], w_out, tm=tm, tn=512, res=xf, gate=gate1)
    h2 = _norm_mod(x1, g_ffn, sc2, sh2, rows_per_batch=t, tt=tt, out_dtype=F32)

    state = (ka.reshape(1, b, t, A_KV_HEADS, HEAD_DIM), va.reshape(1, b, t, A_KV_HEADS, HEAD_DIM),
             ki.reshape(1, b, t, IDX_DIM),
             kb.reshape(1, b, t, B_HEADS, HEAD_DIM), vb.reshape(1, b, t, B_HEADS, HEAD_DIM),
             logf.reshape(1, b, t, B_HEADS))
    return x1, h2, state


def kernel(x_prompt, x_sample, cache_a_k, cache_a_v, cache_idx_k, cache_b_k, cache_b_v, cache_b_logf, c_prompt, c_sample, w_in, b_forget, w_out, rel_bias, w_mod, b_mod, g_mix, g_ffn, w_router_group, b_router_group, w_router_expert, b_router_expert, w_gate, w_up, w_down, g_final):
    assert w_in.shape[0] == 1, "single-layer trunk"
    d = x_prompt.shape[-1]
    bp, tp, _ = x_prompt.shape
    bs, ts, _ = x_sample.shape
    n_p, n_s = bp * tp, bs * ts
    layer0 = lambda a: a.reshape(a.shape[1:])

    n_c = bp + bs
    c_rows = -(-n_c // 8) * 8
    c_all = jnp.pad(jnp.concatenate([c_prompt, c_sample], axis=0), ((0, c_rows - n_c), (0, 0)))
    mod = _matmul([c_all], layer0(w_mod), tm=c_rows, tn=512, silu_a=True, bias=b_mod.reshape(1, -1))

    def mods_of(lo, hi):
        m6 = mod[lo:hi].reshape(hi - lo, 6, 1, d)
        return [m6[:, i] for i in range(6)]

    mods_p, mods_s = mods_of(0, bp), mods_of(bp, n_c)

    w = layer0(w_in)
    o1 = A_WIDTH + 2 * A_KV_WIDTH + IDX_WIDTH
    o2 = o1 + IDX_DIM + IDX_HEADS
    o3 = o2 + 3 * B_WIDTH
    tail_pad = SMALL_WIDTH - (IDX_DIM + IDX_HEADS + B_HEADS)
    w_packed = jnp.concatenate(
        [w[:, :o1], w[:, o2:o3], w[:, o1:o2], w[:, o3:], jnp.zeros((d, tail_pad), w.dtype)],
        axis=1).astype(BF16)

    g_mix2, g_ffn2, g_fin2 = g_mix.reshape(1, d), g_ffn.reshape(1, d), g_final.reshape(1, d)
    past = tuple(layer0(c) for c in (cache_a_k, cache_a_v, cache_idx_k, cache_b_k, cache_b_v,
                                      cache_b_logf))
    b_f, w_o = layer0(b_forget), layer0(w_out)
    x1_p, h2_p, st_p = _mixer_half(x_prompt, mods_p, None, w_packed, b_f, w_o, rel_bias, g_mix2, g_ffn2)
    x1_s, h2_s, st_s = _mixer_half(x_sample, mods_s, past, w_packed, b_f, w_o, rel_bias, g_mix2, g_ffn2)

    n_tok = n_p + n_s
    w_r = jnp.concatenate([layer0(w_router_group), layer0(w_router_expert)], axis=1)
    n_cls = w_r.shape[1]
    w_r = jnp.pad(w_r, ((0, 0), (0, LANES - n_cls)))
    b_r = jnp.pad(jnp.concatenate([layer0(b_router_group), layer0(b_router_expert)]), (0, LANES - n_cls))
    logits = jnp.concatenate(
        [_matmul([hh], w_r, tm=_pick(hh.shape[0], (512, 256, 128)), tn=LANES,
                 bias=b_r.reshape(1, LANES)) for hh in (h2_p, h2_s)], axis=0)
    eidx_t, gate_t = _router(jnp.transpose(logits))
    expert = jnp.transpose(eidx_t)
    gates = jnp.transpose(gate_t)

    n_assign = n_tok * TOP_K_IN_GROUP
    assert n_assign % GATHER_ROWS == 0
    n_sb = n_assign // EXPERT_ROWS + N_EXPERTS
    n_x_rows = -(-n_assign // EXPERT_CHUNK) * EXPERT_CHUNK + N_EXPERTS * EXPERT_CHUNK + EXPERT_ROWS
    x_row_tok, n_live_x, y_row, sb_expert, sb_row0, sb_nch, sb_blk = _dispatch(expert, n_sb, n_x_rows)
    assert d % EXPERT_K_TILE == 0 and d % EXPERT_N_TILE == 0
    xs = _gather_rows([h2_p, h2_s], x_row_tok, n_live_x, BF16, n_slabs=d // EXPERT_K_TILE)
    yb = _experts(xs, sb_expert, sb_row0, sb_nch, sb_blk, layer0(w_gate), layer0(w_up),
                  layer0(w_down), n_sb)
    y_idx = jnp.concatenate([y_row[:, 0], y_row[:, 1]])
    y01 = _gather_rows([yb], y_idx, jnp.full((1,), n_assign // GATHER_ROWS, jnp.int32), F32)

    def finish(x1, lo, gt2, t):
        tt = _pick(t, (128, 64, 32, 16))
        return _combine(x1, y01, gates[lo:lo + x1.shape[0]], gt2, g_fin2, row_off=lo, n_tok=n_tok,
                        rows_per_batch=t, tt=tt)

    y_p = finish(x1_p, 0, mods_p[5], tp).reshape(bp, tp, d)
    y_s = finish(x1_s, n_p, mods_s[5], ts).reshape(bs, ts, d)
    return (y_p, y_s) + st_p + st_s
```

```python
import functools
import math

import numpy as np
import jax
import jax.numpy as jnp
from jax import lax
from jax.experimental import pallas as pl
from jax.experimental.pallas import tpu as pltpu

CHUNK = 64
HEAD_DIM = 128
A_HEADS = 16
A_KV_HEADS = 4
IDX_HEADS = 32
IDX_DIM = 64
TOPK_MAX = 256
B_HEADS = 16
N_BUCKETS = 32
MAX_DISTANCE = 128
N_GROUPS = 4
EXPERTS_PER_GROUP = 8
N_EXPERTS = N_GROUPS * EXPERTS_PER_GROUP
TOP_K_IN_GROUP = 2
EPS = 1e-6

A_WIDTH = A_HEADS * HEAD_DIM
A_KV_WIDTH = A_KV_HEADS * HEAD_DIM
B_WIDTH = B_HEADS * HEAD_DIM
IDX_WIDTH = IDX_HEADS * IDX_DIM
SMALL_WIDTH = 128

LANES = 128
V7X_VMEM_BYTES = 64 * 1024 * 1024
VMEM_LIMIT = 56 * 1024 * 1024

BF16 = jnp.bfloat16
F32 = jnp.float32
NEG_INF = float("-inf")
INT_MIN = -(2 ** 31)
KEY_NEG_INF = int(np.int32(np.float32(-np.inf).view(np.int32)) ^ np.int32(0x7FFFFFFF))
LOG2E = math.log2(math.e)
Q_SCALE = HEAD_DIM ** -0.5 * LOG2E


def _cparams(*sem):
    return pltpu.CompilerParams(dimension_semantics=sem, vmem_limit_bytes=VMEM_LIMIT)


def _matmul_kernel(*refs, n_a, silu_a, epilogue, out_scale, n_out):
    a_refs = refs[:n_a]
    w_ref = refs[n_a]
    rest = refs[n_a + 1:]
    o_refs = rest[len(rest) - n_out:]
    acc = None
    k0 = 0
    for a_ref in a_refs:
        a = a_ref[...]
        if silu_a:
            a = a * jax.nn.sigmoid(a)
        kk = a.shape[1]
        part = jnp.dot(a.astype(BF16), w_ref[k0:k0 + kk, :].astype(BF16),
                       preferred_element_type=F32)
        acc = part if acc is None else acc + part
        k0 += kk
    if epilogue == "bias":
        acc = acc + rest[0][...]
    elif epilogue == "gated_residual":
        acc = rest[0][...] + rest[1][...] * acc
    if out_scale is not None:
        acc = acc * out_scale
    for o_ref in o_refs:
        o_ref[...] = acc.astype(o_ref.dtype)


def _matmul(a_list, w, *, col_off=0, n=None, tm, tn, silu_a=False, bias=None, res=None,
            gate=None, out_dtypes=(F32,), out_scale=None):
    m = a_list[0].shape[0]
    k = w.shape[0]
    n = w.shape[1] - col_off if n is None else n
    assert m % tm == 0 and n % tn == 0 and col_off % tn == 0
    assert sum(a.shape[1] for a in a_list) == k
    cb = col_off // tn
    in_specs = [pl.BlockSpec((tm, a.shape[1]), lambda j, i: (i, 0)) for a in a_list]
    in_specs.append(pl.BlockSpec((k, tn), lambda j, i: (0, j + cb)))
    args = list(a_list) + [w]
    if bias is not None:
        epilogue = "bias"
        in_specs.append(pl.BlockSpec((1, tn), lambda j, i: (0, j)))
        args.append(bias)
    elif res is not None:
        epilogue = "gated_residual"
        in_specs.append(pl.BlockSpec((tm, tn), lambda j, i: (i, j)))
        args.append(res)
        if gate.shape[0] == 1:
            in_specs.append(pl.BlockSpec((1, tn), lambda j, i: (0, j)))
        else:
            in_specs.append(pl.BlockSpec((tm, tn), lambda j, i: (i, j)))
        args.append(gate)
    else:
        epilogue = None
    outs = pl.pallas_call(
        functools.partial(_matmul_kernel, n_a=len(a_list), silu_a=silu_a, epilogue=epilogue,
                          out_scale=out_scale, n_out=len(out_dtypes)),
        grid=(n // tn, m // tm),
        in_specs=in_specs,
        out_specs=[pl.BlockSpec((tm, tn), lambda j, i: (i, j)) for _ in out_dtypes],
        out_shape=[jax.ShapeDtypeStruct((m, n), dt) for dt in out_dtypes],
        compiler_params=_cparams("arbitrary", "arbitrary"),
    )(*args)
    return outs[0] if len(out_dtypes) == 1 else outs


def _rms(x, g):
    return x * lax.rsqrt(jnp.mean(x * x, axis=-1, keepdims=True) + EPS) * g


def _norm_mod_kernel(x_ref, g_ref, sc_ref, sh_ref, o_ref):
    y = _rms(x_ref[...], g_ref[...])
    o_ref[...] = (y * (1.0 + sc_ref[0]) + sh_ref[0]).astype(o_ref.dtype)


def _norm_mod(x, g, sc, sh, *, rows_per_batch, tt, out_dtype):
    n, d = x.shape
    bpb = rows_per_batch // tt
    return pl.pallas_call(
        _norm_mod_kernel,
        grid=(n // tt,),
        in_specs=[pl.BlockSpec((tt, d), lambda i: (i, 0)),
                  pl.BlockSpec((1, d), lambda i: (0, 0)),
                  pl.BlockSpec((1, 1, d), lambda i: (i // bpb, 0, 0)),
                  pl.BlockSpec((1, 1, d), lambda i: (i // bpb, 0, 0))],
        out_specs=pl.BlockSpec((tt, d), lambda i: (i, 0)),
        out_shape=jax.ShapeDtypeStruct((n, d), out_dtype),
        compiler_params=_cparams("arbitrary"),
    )(x, g, sc, sh)


def _combine_kernel(x_ref, y0_ref, y1_ref, gates_ref, gt_ref, g_ref, o_ref):
    g0 = gates_ref[:, 0:1]
    g1 = gates_ref[:, 1:2]
    moe = y0_ref[...] * g0 + y1_ref[...] * g1
    x = x_ref[...] + gt_ref[0] * moe
    o_ref[...] = _rms(x, g_ref[...])


def _combine(x, y01, gates, gt, g_final, *, row_off, n_tok, rows_per_batch, tt):
    n, d = x.shape
    bpb = rows_per_batch // tt
    b0 = row_off // tt
    b1 = (n_tok + row_off) // tt
    row = pl.BlockSpec((tt, d), lambda i: (i, 0))
    return pl.pallas_call(
        _combine_kernel,
        grid=(n // tt,),
        in_specs=[row,
                  pl.BlockSpec((tt, d), lambda i: (i + b0, 0)),
                  pl.BlockSpec((tt, d), lambda i: (i + b1, 0)),
                  pl.BlockSpec((tt, 2), lambda i: (i, 0)),
                  pl.BlockSpec((1, 1, d), lambda i: (i // bpb, 0, 0)),
                  pl.BlockSpec((1, d), lambda i: (0, 0))],
        out_specs=row,
        out_shape=jax.ShapeDtypeStruct((n, d), F32),
        compiler_params=_cparams("arbitrary"),
    )(x, y01, y01, gates, gt, g_final)


GATHER_ROWS = 256


def _gather_kernel(idx_ref, nlive_ref, *refs, src_rows):
    n_src = len(src_rows)
    src_refs = refs[:n_src]
    o_ref, buf_ref, sem = refs[n_src:]
    i = pl.program_id(0)

    @pl.when(i < nlive_ref[0])
    def _():
        base = i * GATHER_ROWS

        def row_copy(src_ref, r, src_row):
            return pltpu.make_async_copy(src_ref.at[pl.ds(src_row, 1)], buf_ref.at[pl.ds(r, 1)], sem)

        def issue(r, priority):
            row = idx_ref[base + r]
            lo = 0
            for k, src_ref in enumerate(src_refs):
                hi = lo + src_rows[k]
                in_range = row >= lo if k == n_src - 1 else jnp.logical_and(row >= lo, row < hi)
                if n_src == 1:
                    row_copy(src_ref, r, row).start(priority=priority)
                else:
                    @pl.when(in_range)
                    def _():
                        row_copy(src_ref, r, row - lo).start(priority=priority)
                lo = hi

        def start(r2, carry):
            issue(2 * r2, 0)
            issue(2 * r2 + 1, 1)
            return carry

        def wait(r, carry):
            row_copy(src_refs[0], r, 0).wait()
            return carry

        lax.fori_loop(0, GATHER_ROWS // 2, start, 0)
        lax.fori_loop(0, GATHER_ROWS, wait, 0)
        if len(o_ref.shape) == 2:
            o_ref[...] = buf_ref[...].astype(o_ref.dtype)
        else:
            w = o_ref.shape[2]
            for k in range(o_ref.shape[0]):
                o_ref[k] = buf_ref[:, k * w:(k + 1) * w].astype(o_ref.dtype)


def _gather_rows(srcs, idx, n_live_blocks, out_dtype, n_slabs=None):
    nr = idx.shape[0]
    src = srcs[0]
    d = src.shape[1]
    assert nr % GATHER_ROWS == 0
    blk = lambda i, idx, nl: jnp.minimum(i, nl[0] - 1)
    if n_slabs is None:
        out_spec = pl.BlockSpec((GATHER_ROWS, d), lambda i, idx, nl: (blk(i, idx, nl), 0))
        out_shape = (nr, d)
    else:
        out_spec = pl.BlockSpec((n_slabs, GATHER_ROWS, d // n_slabs),
                                lambda i, idx, nl: (0, blk(i, idx, nl), 0))
        out_shape = (n_slabs, nr, d // n_slabs)
    grid_spec = pltpu.PrefetchScalarGridSpec(
        num_scalar_prefetch=2,
        grid=(nr // GATHER_ROWS,),
        in_specs=[pl.BlockSpec(memory_space=pl.ANY) for _ in srcs],
        out_specs=out_spec,
        scratch_shapes=[pltpu.VMEM((GATHER_ROWS, d), src.dtype), pltpu.SemaphoreType.DMA(())],
    )
    return pl.pallas_call(
        functools.partial(_gather_kernel, src_rows=tuple(s.shape[0] for s in srcs)),
        grid_spec=grid_spec,
        out_shape=jax.ShapeDtypeStruct(out_shape, out_dtype),
        compiler_params=_cparams("arbitrary"),
    )(idx, n_live_blocks, *srcs)


def _lane_prefix_sum(x, lane):
    s = 1
    while s < LANES:
        x = x + jnp.where(lane >= s, pltpu.roll(x, s, axis=1), 0.0)
        s *= 2
    return x


def _logf_cumsum_kernel(*refs, p_len, t_pad):
    if p_len:
        fl_ref, bf_ref, past_ref, logf_ref, cum_ref = refs
    else:
        fl_ref, bf_ref, logf_ref, cum_ref = refs
        past_ref = None
    lane = lax.broadcasted_iota(jnp.int32, (B_HEADS, LANES), 1)
    carry = jnp.zeros((B_HEADS, 1), F32)
    for c in range(p_len // LANES):
        sl = slice(c * LANES, (c + 1) * LANES)
        y = _lane_prefix_sum(past_ref[0, :, sl], lane) + carry
        cum_ref[0, :, sl] = y
        carry = y[:, LANES - 1:LANES]
    for c in range(t_pad // LANES):
        sl = slice(c * LANES, (c + 1) * LANES)
        z = fl_ref[0, :, sl] + bf_ref[...]
        logf = jnp.minimum(z, 0.0) - jnp.log1p(jnp.exp(-jnp.abs(z)))
        logf_ref[0, :, sl] = logf
        y = _lane_prefix_sum(logf, lane) + carry
        cum_ref[0, :, p_len + c * LANES:p_len + (c + 1) * LANES] = y
        carry = y[:, LANES - 1:LANES]


def _logf_cumsum(fl_t, b_forget, past_t):
    b, _, t_pad = fl_t.shape
    p_len = 0 if past_t is None else past_t.shape[2]
    in_specs = [pl.BlockSpec((1, B_HEADS, t_pad), lambda i: (i, 0, 0)),
                pl.BlockSpec((B_HEADS, 1), lambda i: (0, 0))]
    args = [fl_t, b_forget]
    if p_len:
        in_specs.append(pl.BlockSpec((1, B_HEADS, p_len), lambda i: (i, 0, 0)))
        args.append(past_t)
    return pl.pallas_call(
        functools.partial(_logf_cumsum_kernel, p_len=p_len, t_pad=t_pad),
        grid=(b,),
        in_specs=in_specs,
        out_specs=[pl.BlockSpec((1, B_HEADS, t_pad), lambda i: (i, 0, 0)),
                   pl.BlockSpec((1, B_HEADS, p_len + t_pad), lambda i: (i, 0, 0))],
        out_shape=[jax.ShapeDtypeStruct((b, B_HEADS, t_pad), F32),
                   jax.ShapeDtypeStruct((b, B_HEADS, p_len + t_pad), F32)],
        compiler_params=_cparams("arbitrary"),
    )(*args)


BISECT_ROWS = 128


def _indexer_kernel(qi_ref, small_ref, klo_ref, khi_ref, mask_ref, qh_ref, wb_ref, keys_ref, *,
                    tq, tkc, n_chunks, l_real, p_len, nq_per_batch, n_sel):
    i = pl.program_id(0)
    row0 = (i % nq_per_batch) * tq
    n_pairs = IDX_HEADS // 2
    sub = tkc // LANES if tkc % LANES == 0 else 0
    for b in range(n_pairs):
        qh_ref[b * tq:(b + 1) * tq, :] = qi_ref[:, b * LANES:(b + 1) * LANES]
    w = small_ref[:, IDX_DIM:IDX_DIM + IDX_HEADS] * (IDX_HEADS ** -0.5)
    w = w * (IDX_DIM ** -0.5)
    wcols = LANES if sub else tkc
    for h in range(IDX_HEADS):
        wb_ref[h] = jnp.broadcast_to(w[:, h:h + 1], (tq, wcols))
    qpos = p_len + row0 + lax.broadcasted_iota(jnp.int32, (tq, 1), 0)
    k_limit = jnp.minimum((qpos // CHUNK + 1) * CHUNK, l_real)
    last_allowed = (p_len + row0 + tq - 1) // CHUNK * CHUNK + CHUNK - 1
    n_allowed = jnp.minimum(last_allowed // tkc + 1, n_chunks)

    def allowed_of(c0):
        kpos = c0 + lax.broadcasted_iota(jnp.int32, (tq, tkc), 1)
        return kpos < k_limit

    def weighted(s, h):
        r = jnp.maximum(s, 0.0)
        if not sub:
            return r * wb_ref[h]
        wv = wb_ref[h]
        return jnp.concatenate([r[:, u * LANES:(u + 1) * LANES] * wv for u in range(sub)], axis=1)

    for c in range(n_chunks):
        c0 = c * tkc

        @pl.when(c < n_allowed)
        def _():
            q_all = qh_ref[...]
            s_lo = jnp.dot(q_all, klo_ref[0, :, c0:c0 + tkc], preferred_element_type=F32)
            s_hi = jnp.dot(q_all, khi_ref[0, :, c0:c0 + tkc], preferred_element_type=F32)
            score = jnp.zeros((tq, tkc), F32)
            for b in range(n_pairs):
                score = score + weighted(s_lo[b * tq:(b + 1) * tq, :], 2 * b)
                score = score + weighted(s_hi[b * tq:(b + 1) * tq, :], 2 * b + 1)
            score = score + 0.0
            score = jnp.where(allowed_of(c0), score, NEG_INF)
            bits = lax.bitcast_convert_type(score, jnp.int32)
            keys_ref[c] = jnp.where(bits < 0, bits ^ 0x7FFFFFFF, bits)

    def count(pred_of_chunk, r0=0, nr=tq):
        def body(c, acc):
            ind = jnp.where(pred_of_chunk(keys_ref[c, r0:r0 + nr, :], c), 1.0, 0.0)
            if not sub:
                return acc + ind
            for u in range(sub):
                acc = acc + ind[:, u * LANES:(u + 1) * LANES]
            return acc

        acc = lax.fori_loop(0, n_allowed, body, jnp.zeros((nr, wcols), F32))
        return jnp.sum(acc, axis=1, keepdims=True)

    k_f = float(n_sel)
    slab = min(tq, BISECT_ROWS)
    t_parts = []
    for r0 in range(0, tq, slab):
        cnt = functools.partial(count, r0=r0, nr=slab)
        t0 = jnp.where(cnt(lambda kc, c: kc >= 0) >= k_f, 0, INT_MIN).astype(jnp.int32)

        def bisect(it, t, cnt=cnt):
            cand = t + jnp.left_shift(jnp.int32(1), 30 - it)
            return jnp.where(cnt(lambda kc, c: kc >= cand) >= k_f, cand, t)

        t_parts.append(lax.fori_loop(0, 31, bisect, t0))
    t = t_parts[0] if len(t_parts) == 1 else jnp.concatenate(t_parts, axis=0)
    cge = count(lambda kc, c: kc >= t)
    excess = jnp.max(jnp.where(t > KEY_NEG_INF, cge, 0.0)) > k_f

    def emit(sel_of_chunk):
        for c in range(n_chunks):
            @pl.when(c < n_allowed)
            def _():
                sel = jnp.logical_and(sel_of_chunk(keys_ref[c], c), allowed_of(c * tkc))
                mask_ref[c] = jnp.where(sel, 0.0, NEG_INF).astype(mask_ref.dtype)

            @pl.when(c >= n_allowed)
            def _():
                mask_ref[c] = jnp.full((tq, tkc), NEG_INF, mask_ref.dtype)

    @pl.when(jnp.logical_not(excess))
    def _():
        emit(lambda kc, c: kc >= t)

    @pl.when(excess)
    def _():
        need = k_f - count(lambda kc, c: kc > t)
        lane_idx = lax.broadcasted_iota(jnp.int32, (tq, tkc), 1)
        m = jnp.zeros((tq, 1), jnp.int32)
        for bit in reversed(range(max(1, (n_chunks * tkc - 1).bit_length()))):
            cand = m + (1 << bit)
            c_lt = count(lambda kc, c: jnp.logical_and(kc == t, c * tkc + lane_idx < cand))
            m = jnp.where(c_lt < need, cand, m)
        emit(lambda kc, c: jnp.logical_or(
            kc > t, jnp.logical_and(kc == t, c * tkc + lane_idx <= m)))


def _indexer_mask(qi, small, k_lo, k_hi, *, rows_per_batch, tq, tkc, l_real, p_len):
    n = qi.shape[0]
    l_pad = k_lo.shape[2]
    n_chunks = l_pad // tkc
    nqb = rows_per_batch // tq
    n_sel = min(TOPK_MAX, l_real // 4)
    wcols = LANES if tkc % LANES == 0 else tkc
    kern = functools.partial(_indexer_kernel, tq=tq, tkc=tkc, n_chunks=n_chunks, l_real=l_real,
                             p_len=p_len, nq_per_batch=nqb, n_sel=n_sel)
    kspec = pl.BlockSpec((1, LANES, l_pad), lambda i: (i // nqb, 0, 0))
    return pl.pallas_call(
        kern,
        grid=(n // tq,),
        in_specs=[pl.BlockSpec((tq, IDX_WIDTH), lambda i: (i, 0)),
                  pl.BlockSpec((tq, SMALL_WIDTH), lambda i: (i, 0)),
                  kspec, kspec],
        out_specs=pl.BlockSpec((n_chunks, tq, tkc), lambda i: (0, i, 0)),
        out_shape=jax.ShapeDtypeStruct((n_chunks, n, tkc), BF16),
        scratch_shapes=[pltpu.VMEM((IDX_HEADS // 2 * tq, LANES), BF16),
                        pltpu.VMEM((IDX_HEADS, tq, wcols), F32),
                        pltpu.VMEM((n_chunks, tq, tkc), jnp.int32)],
        compiler_params=_cparams("arbitrary"),
    )(qi, small, k_lo, k_hi)


def _softmax_step(s, carry, v_tile):
    m, l, acc = carry
    m_new = jnp.maximum(m, jnp.max(s, axis=1, keepdims=True))
    m_safe = jnp.where(m_new == NEG_INF, 0.0, m_new)
    alpha = jnp.exp2(m - m_safe)
    p = jnp.exp2(s - m_safe)
    l = alpha * l + jnp.sum(p, axis=1, keepdims=True)
    acc = alpha * acc + jnp.dot(p.astype(BF16), v_tile, preferred_element_type=F32)
    return m_new, l, acc


def _softmax_init(rows):
    return (jnp.full((rows, 1), NEG_INF, F32), jnp.zeros((rows, 1), F32),
            jnp.zeros((rows, HEAD_DIM), F32))


def _qk(q, k_tile):
    return lax.dot_general(q, k_tile, (((1,), (1,)), ((), ())), preferred_element_type=F32)


def _fox_prompt_kernel(q_ref, k_ref, v_ref, fk_ref, o_ref, *, t_tile, n_heads, nt):
    i = pl.program_id(1)
    lanes = lambda g: slice(g * HEAD_DIM, (g + 1) * HEAD_DIM)
    qs = [q_ref[:, lanes(g)] for g in range(n_heads)]

    def tile(j, carries, diagonal):
        k0 = pl.multiple_of(j * t_tile, t_tile)
        out = []
        for g in range(n_heads):
            s = _qk(qs[g], k_ref[pl.ds(k0, t_tile), lanes(g)]) - fk_ref[g * nt + j] * LOG2E
            if diagonal:
                row = lax.broadcasted_iota(jnp.int32, (t_tile, t_tile), 0)
                col = lax.broadcasted_iota(jnp.int32, (t_tile, t_tile), 1)
                s = jnp.where(col <= row, s, NEG_INF)
            out.append(_softmax_step(s, carries[g], v_ref[pl.ds(k0, t_tile), lanes(g)]))
        return tuple(out)

    init = tuple(_softmax_init(t_tile) for _ in range(n_heads))
    carries = lax.fori_loop(0, i, lambda j, c: tile(j, c, False), init)
    carries = tile(i, carries, True)
    for g in range(n_heads):
        _, l, acc = carries[g]
        o_ref[:, lanes(g)] = (acc / l).astype(o_ref.dtype)


def _fox_prompt(q, k, v, fk_tiles, *, t_tile, heads_per_step):
    t, width = q.shape
    heads = width // HEAD_DIM
    nt = t // t_tile
    gw = heads_per_step * HEAD_DIM
    resident = dict(pipeline_mode=pl.Buffered(1))
    return pl.pallas_call(
        functools.partial(_fox_prompt_kernel, t_tile=t_tile, n_heads=heads_per_step, nt=nt),
        grid=(heads // heads_per_step, nt),
        in_specs=[pl.BlockSpec((t_tile, gw), lambda h, i: (i, h)),
                  pl.BlockSpec((t, gw), lambda h, i: (0, h), **resident),
                  pl.BlockSpec((t, gw), lambda h, i: (0, h), **resident),
                  pl.BlockSpec((heads_per_step * nt, 1, t_tile), lambda h, i: (h, 0, 0))],
        out_specs=pl.BlockSpec((t_tile, gw), lambda h, i: (i, h)),
        out_shape=jax.ShapeDtypeStruct((t, width), BF16),
        compiler_params=_cparams("arbitrary", "arbitrary"),
    )(q, k, v, fk_tiles)


def _dsa_prompt_kernel(q_ref, k_ref, v_ref, mask_ref, bias_ref, o_ref, *, t_tile, grp, mask_w):
    i = pl.program_id(1)
    q = jnp.concatenate([q_ref[:, j * HEAD_DIM:(j + 1) * HEAD_DIM] for j in range(grp)], axis=0)
    n_sub = t_tile // mask_w

    def tile(j, carry, bias_cls):
        k0 = pl.multiple_of(j * t_tile, t_tile)
        s = _qk(q, k_ref[pl.ds(k0, t_tile), :]).reshape(grp, t_tile, t_tile)
        madd = jnp.concatenate([mask_ref[j * n_sub + u] for u in range(n_sub)], axis=1)
        s = s + madd.astype(F32)[None]
        if bias_cls is not None:
            s = s + bias_ref[bias_cls]
        return _softmax_step(s.reshape(grp * t_tile, t_tile), carry, v_ref[pl.ds(k0, t_tile), :])

    carry = lax.fori_loop(0, jnp.maximum(i - 1, 0), lambda j, c: tile(j, c, None),
                          _softmax_init(grp * t_tile))
    carry = lax.fori_loop(jnp.maximum(i - 1, 0), i, lambda j, c: tile(j, c, 1), carry)
    _, l, acc = tile(i, carry, 0)
    out = acc / l
    for j in range(grp):
        o_ref[:, j * HEAD_DIM:(j + 1) * HEAD_DIM] = out[j * t_tile:(j + 1) * t_tile].astype(o_ref.dtype)


def _dsa_prompt(q, k, v, mask, bias, *, t_tile):
    t, width = q.shape
    kv_heads = k.shape[1] // HEAD_DIM
    grp = width // HEAD_DIM // kv_heads
    nt = t // t_tile
    n_mask, _, mask_w = mask.shape
    resident = dict(pipeline_mode=pl.Buffered(1))
    return pl.pallas_call(
        functools.partial(_dsa_prompt_kernel, t_tile=t_tile, grp=grp, mask_w=mask_w),
        grid=(kv_heads, nt),
        in_specs=[pl.BlockSpec((t_tile, grp * HEAD_DIM), lambda g, i: (i, g)),
                  pl.BlockSpec((t, HEAD_DIM), lambda g, i: (0, g), **resident),
                  pl.BlockSpec((t, HEAD_DIM), lambda g, i: (0, g), **resident),
                  pl.BlockSpec((n_mask, t_tile, mask_w), lambda g, i: (0, i, 0)),
                  pl.BlockSpec((2, grp, t_tile, t_tile), lambda g, i: (0, g, 0, 0), **resident)],
        out_specs=pl.BlockSpec((t_tile, grp * HEAD_DIM), lambda g, i: (i, g)),
        out_shape=jax.ShapeDtypeStruct((t, width), BF16),
        compiler_params=_cparams("arbitrary", "arbitrary"),
    )(q, k, v, mask, bias)


def _attend(q_ref, k_of, v_of, bias_of_head, m_ref, l_ref, acc_ref, *, tq, kv_heads, grp):
    for g in range(kv_heads):
        kg = k_of(g).astype(BF16)
        vg = v_of(g).astype(BF16)
        qs = [q_ref[:, (g * grp + j) * HEAD_DIM:(g * grp + j + 1) * HEAD_DIM] for j in range(grp)]
        qg = qs[0] if grp == 1 else jnp.concatenate(qs, axis=0)
        s_all = _qk(qg, kg)
        for j in range(grp):
            h = g * grp + j
            s = s_all[j * tq:(j + 1) * tq, :] + bias_of_head(h)
            m_ref[h], l_ref[h], acc_ref[h] = _softmax_step(s, (m_ref[h], l_ref[h], acc_ref[h]), vg)


def _flash_kernel(*refs, mode, n_a, tq, tk_a, tk_b, p_len, kv_heads, grp):
    heads = kv_heads * grp
    it = iter(refs)
    q_ref = next(it)
    ka_ref, va_ref, kb_ref, vb_ref = next(it), next(it), next(it), next(it)
    if mode == "dsa":
        maska_ref, biasa_ref, maskb_ref, biasb_ref = next(it), next(it), next(it), next(it)
    else:
        fq_ref, fka_ref, fkb_ref = next(it), next(it), next(it)
    o_ref, m_ref, l_ref, acc_ref = next(it), next(it), next(it), next(it)
    j = pl.program_id(1)

    @pl.when(j == 0)
    def _():
        m_ref[...] = jnp.full(m_ref.shape, NEG_INF, F32)
        l_ref[...] = jnp.zeros(l_ref.shape, F32)
        acc_ref[...] = jnp.zeros(acc_ref.shape, F32)

    common = dict(tq=tq, kv_heads=kv_heads, grp=grp)

    @pl.when(j < n_a)
    def _():
        if mode == "dsa":
            madd = maska_ref[...].astype(F32)
            bias = lambda h: biasa_ref[0, h] + madd
        else:
            bias = lambda h: (fq_ref[:, h:h + 1] - fka_ref[0, h:h + 1, :]) * LOG2E
        k_of = lambda g: ka_ref[0, pl.ds(g, tk_a, stride=kv_heads), :]
        v_of = lambda g: va_ref[0, pl.ds(g, tk_a, stride=kv_heads), :]
        _attend(q_ref, k_of, v_of, bias, m_ref, l_ref, acc_ref, **common)

    @pl.when(j == n_a)
    def _():
        if mode == "dsa":
            madd = maskb_ref[...].astype(F32)
            bias = lambda h: biasb_ref[0, h] + madd
        else:
            row = lax.broadcasted_iota(jnp.int32, (tq, tk_b), 0)
            col = lax.broadcasted_iota(jnp.int32, (tq, tk_b), 1)
            causal = jnp.where(col <= row, 0.0, NEG_INF)
            bias = lambda h: (fq_ref[:, h:h + 1] - fkb_ref[0, h:h + 1, :tk_b]) * LOG2E + causal
        k_of = lambda g: kb_ref[:, g * HEAD_DIM:(g + 1) * HEAD_DIM]
        v_of = lambda g: vb_ref[:, g * HEAD_DIM:(g + 1) * HEAD_DIM]
        _attend(q_ref, k_of, v_of, bias, m_ref, l_ref, acc_ref, **common)
        for h in range(heads):
            o_ref[:, h * HEAD_DIM:(h + 1) * HEAD_DIM] = (acc_ref[h] / l_ref[h]).astype(o_ref.dtype)


def _flash_sample(mode, q, k_new, v_new, k_past, v_past, *, tk_a, mask_a=None, bias_a=None,
                  mask_b=None, bias_b=None, fq=None, fk_t=None):
    n, width = q.shape
    kv_w = k_new.shape[1]
    b = k_past.shape[0]
    tq = n // b
    heads = width // HEAD_DIM
    kv_heads = kv_w // HEAD_DIM
    p_len = k_past.shape[1] // kv_heads
    grp = heads // kv_heads
    n_a = p_len // tk_a
    a_blk = lambda j: jnp.minimum(j, n_a - 1)

    in_specs = [pl.BlockSpec((tq, width), lambda g, j: (g, 0))]
    spec = pl.BlockSpec((1, tk_a * kv_heads, HEAD_DIM), lambda g, j: (g, a_blk(j), 0))
    in_specs += [spec, spec]
    spec = pl.BlockSpec((tq, kv_w), lambda g, j: (g, 0))
    in_specs += [spec, spec]
    args = [q, k_past, v_past, k_new, v_new]
    if mode == "dsa":
        in_specs += [pl.BlockSpec((tq, tk_a), lambda g, j: (g, a_blk(j))),
                     pl.BlockSpec((1, heads, tq, tk_a), lambda g, j: (a_blk(j), 0, 0, 0)),
                     pl.BlockSpec((tq, tq), lambda g, j: (g, 0)),
                     pl.BlockSpec((1, heads, tq, tq), lambda g, j: (0, 0, 0, 0))]
        args += [mask_a, bias_a, mask_b, bias_b]
    else:
        in_specs += [pl.BlockSpec((tq, heads), lambda g, j: (g, 0)),
                     pl.BlockSpec((1, heads, tk_a), lambda g, j: (g, 0, a_blk(j))),
                     pl.BlockSpec((1, heads, LANES), lambda g, j: (g, 0, p_len // LANES))]
        args += [fq, fk_t, fk_t]

    kern = functools.partial(_flash_kernel, mode=mode, n_a=n_a, tq=tq, tk_a=tk_a, tk_b=tq,
                             p_len=p_len, kv_heads=kv_heads, grp=grp)
    return pl.pallas_call(
        kern,
        grid=(b, n_a + 1),
        in_specs=in_specs,
        out_specs=pl.BlockSpec((tq, width), lambda g, j: (g, 0)),
        out_shape=jax.ShapeDtypeStruct((n, width), BF16),
        scratch_shapes=[pltpu.VMEM((heads, tq, 1), F32),
                        pltpu.VMEM((heads, tq, 1), F32),
                        pltpu.VMEM((heads, tq, HEAD_DIM), F32)],
        compiler_params=_cparams("arbitrary", "arbitrary"),
    )(*args)


def _router_kernel(lt_ref, eidx_ref, gate_ref):
    row = lambda r: lt_ref[r:r + 1, :]
    g = [row(r) for r in range(N_GROUPS)]
    gmax = functools.reduce(jnp.maximum, g)
    gsel = jnp.full(gmax.shape, N_GROUPS - 1, jnp.int32)
    for r in reversed(range(N_GROUPS - 1)):
        gsel = jnp.where(g[r] == gmax, r, gsel)
    denom = functools.reduce(lambda a, b: a + b, [jnp.exp(x - gmax) for x in g])
    g_prob = 1.0 / denom
    e_in = []
    for kk in range(EXPERTS_PER_GROUP):
        v = row(N_GROUPS + (N_GROUPS - 1) * EXPERTS_PER_GROUP + kk)
        for r in reversed(range(N_GROUPS - 1)):
            v = jnp.where(gsel == r, row(N_GROUPS + r * EXPERTS_PER_GROUP + kk), v)
        e_in.append(v)
    v1 = functools.reduce(jnp.maximum, e_in)
    i1 = jnp.full(v1.shape, EXPERTS_PER_GROUP - 1, jnp.int32)
    for kk in reversed(range(EXPERTS_PER_GROUP - 1)):
        i1 = jnp.where(e_in[kk] == v1, kk, i1)
    rest = [jnp.where(i1 == kk, NEG_INF, e_in[kk]) for kk in range(EXPERTS_PER_GROUP)]
    v2 = functools.reduce(jnp.maximum, rest)
    i2 = jnp.full(v2.shape, EXPERTS_PER_GROUP - 1, jnp.int32)
    for kk in reversed(range(EXPERTS_PER_GROUP - 1)):
        i2 = jnp.where(jnp.logical_and(rest[kk] == v2, i1 != kk), kk, i2)
    e2 = jnp.exp(v2 - v1)
    inv = 1.0 / (1.0 + e2)
    eidx_ref[0:1, :] = gsel * EXPERTS_PER_GROUP + i1
    eidx_ref[1:2, :] = gsel * EXPERTS_PER_GROUP + i2
    gate_ref[0:1, :] = inv * g_prob
    gate_ref[1:2, :] = (e2 * inv) * g_prob


def _router(logits_t):
    n = logits_t.shape[1]
    full = lambda r: pl.BlockSpec((r, n), lambda: (0, 0))
    return pl.pallas_call(
        _router_kernel,
        in_specs=[full(logits_t.shape[0])],
        out_specs=[full(TOP_K_IN_GROUP), full(TOP_K_IN_GROUP)],
        out_shape=[jax.ShapeDtypeStruct((TOP_K_IN_GROUP, n), jnp.int32),
                   jax.ShapeDtypeStruct((TOP_K_IN_GROUP, n), F32)],
        compiler_params=pltpu.CompilerParams(vmem_limit_bytes=VMEM_LIMIT),
    )(logits_t)


EXPERT_CHUNK = 256
EXPERT_CHUNKS_PER_BLOCK = 4
EXPERT_ROWS = EXPERT_CHUNK * EXPERT_CHUNKS_PER_BLOCK
EXPERT_K_TILE = 512
EXPERT_N_TILE = 1024


def _experts_kernel(sb_e_ref, sb_row0_ref, sb_nch_ref, sb_blk_ref, x_hbm, w1_ref, w3_ref, w2_ref,
                    o_ref, x_ref, a_ref, b_ref, act_ref, sem, *, nk):
    del sb_e_ref, sb_blk_ref
    s = pl.program_id(0)
    t = pl.program_id(1)
    n_ch = sb_nch_ref[s]

    @pl.when(jnp.logical_and(t == 0, n_ch > 0))
    def _():
        row0 = pl.multiple_of(sb_row0_ref[s], EXPERT_CHUNK)
        copies = [pltpu.make_async_copy(x_hbm.at[k, pl.ds(row0, EXPERT_ROWS)], x_ref.at[k], sem)
                  for k in range(nk)]
        for cp in copies:
            cp.start()
        for cp in copies:
            cp.wait()

    for nc in range(1, EXPERT_CHUNKS_PER_BLOCK + 1):
        rows = nc * EXPERT_CHUNK

        @pl.when(jnp.logical_and(t < nk, n_ch == nc))
        def _():
            xc = x_ref[t, 0:rows, :]
            a = jnp.dot(xc, w1_ref[0].astype(BF16), preferred_element_type=F32)
            b = jnp.dot(xc, w3_ref[0].astype(BF16), preferred_element_type=F32)

            @pl.when(t == 0)
            def _():
                a_ref[0:rows, :] = a
                b_ref[0:rows, :] = b

            @pl.when(t > 0)
            def _():
                a_ref[0:rows, :] += a
                b_ref[0:rows, :] += b

            @pl.when(t == nk - 1)
            def _():
                g = a_ref[0:rows, :]
                act_ref[0:rows, :] = (g * jax.nn.sigmoid(g) * b_ref[0:rows, :]).astype(BF16)

        @pl.when(jnp.logical_and(t >= nk, n_ch == nc))
        def _():
            o_ref[0:rows, :] = jnp.dot(act_ref[0:rows, :], w2_ref[0].astype(BF16),
                                       preferred_element_type=F32)


def _experts(xs, sb_expert, sb_row0, sb_nch, sb_blk, w1, w3, w2, n_sb):
    nk = xs.shape[0]
    d = nk * EXPERT_K_TILE
    d_e = w1.shape[2]
    nn = d // EXPERT_N_TILE
    r = EXPERT_ROWS
    grid_spec = pltpu.PrefetchScalarGridSpec(
        num_scalar_prefetch=4,
        grid=(n_sb, nk + nn),
        in_specs=[
            pl.BlockSpec(memory_space=pl.ANY),
            pl.BlockSpec((1, EXPERT_K_TILE, d_e),
                         lambda s, t, e, r0, nch, blk: (e[s], jnp.minimum(t, nk - 1), 0)),
            pl.BlockSpec((1, EXPERT_K_TILE, d_e),
                         lambda s, t, e, r0, nch, blk: (e[s], jnp.minimum(t, nk - 1), 0)),
            pl.BlockSpec((1, d_e, EXPERT_N_TILE),
                         lambda s, t, e, r0, nch, blk: (e[s], 0, jnp.maximum(t - nk, 0))),
        ],
        out_specs=pl.BlockSpec(
            (r, EXPERT_N_TILE),
            lambda s, t, e, r0, nch, blk: (blk[s], jnp.where(nch[s] > 0, jnp.maximum(t - nk, 0), 0))),
        scratch_shapes=[pltpu.VMEM((nk, r, EXPERT_K_TILE), BF16),
                        pltpu.VMEM((r, d_e), F32),
                        pltpu.VMEM((r, d_e), F32),
                        pltpu.VMEM((r, d_e), BF16),
                        pltpu.SemaphoreType.DMA(())],
    )
    return pl.pallas_call(
        functools.partial(_experts_kernel, nk=nk),
        grid_spec=grid_spec,
        out_shape=jax.ShapeDtypeStruct(((n_sb + 1) * r, d), F32),
        compiler_params=_cparams("arbitrary", "arbitrary"),
    )(sb_expert, sb_row0, sb_nch, sb_blk, xs, w1, w3, w2)


def _dispatch(expert, n_sb, n_x_rows):
    n = expert.shape[0]
    a = n * TOP_K_IN_GROUP
    ch, r = EXPERT_CHUNK, EXPERT_ROWS
    flat_e = expert.reshape(a)
    flat_tok = jnp.arange(a, dtype=jnp.int32) // TOP_K_IN_GROUP
    order = jnp.argsort(flat_e)
    se = flat_e[order]
    counts = jnp.bincount(flat_e, length=N_EXPERTS).astype(jnp.int32)
    seg_start = jnp.cumsum(counts) - counts
    rank = jnp.arange(a, dtype=jnp.int32) - seg_start[se]
    nch_e = (counts + ch - 1) // ch
    xch_end = jnp.cumsum(nch_e)
    x_start = (xch_end - nch_e) * ch
    nsb_e = (counts + r - 1) // r
    sb_end = jnp.cumsum(nsb_e)
    sb_start = sb_end - nsb_e
    x_row_tok = jnp.zeros((n_x_rows,), jnp.int32).at[x_start[se] + rank].set(flat_tok[order])
    n_live_x_blocks = (xch_end[-1] * ch + GATHER_ROWS - 1) // GATHER_ROWS
    y_row = jnp.zeros((a,), jnp.int32).at[order].set(sb_start[se] * r + rank)
    y_row = y_row.reshape(n, TOP_K_IN_GROUP)
    total = sb_end[-1]
    s_idx = jnp.arange(n_sb, dtype=jnp.int32)
    e_of_sb = jnp.minimum(jnp.searchsorted(sb_end, s_idx, side="right"), N_EXPERTS - 1).astype(jnp.int32)
    k_in_e = s_idx - sb_start[e_of_sb]
    rows_of_sb = jnp.clip(counts[e_of_sb] - k_in_e * r, 0, r)
    live = s_idx < total
    last_e = e_of_sb[jnp.maximum(total - 1, 0)]
    sb_expert = jnp.where(live, e_of_sb, last_e).astype(jnp.int32)
    sb_nch = jnp.where(live, (rows_of_sb + ch - 1) // ch, 0).astype(jnp.int32)
    sb_row0 = jnp.where(live, x_start[e_of_sb] + k_in_e * r, 0).astype(jnp.int32)
    sb_blk = jnp.where(live, s_idx, n_sb).astype(jnp.int32)
    return (x_row_tok, n_live_x_blocks.astype(jnp.int32).reshape(1), y_row, sb_expert, sb_row0,
            sb_nch, sb_blk)


def _t5_bucket(rel):
    half = N_BUCKETS // 2
    max_exact = half // 2
    bucket = jnp.where(rel > 0, half, 0).astype(jnp.int32)
    n = jnp.abs(rel)
    large = max_exact + (jnp.log(jnp.maximum(n, 1).astype(jnp.float32) / max_exact)
                         / math.log(MAX_DISTANCE / max_exact) * (half - max_exact)).astype(jnp.int32)
    large = jnp.minimum(large, half - 1)
    return bucket + jnp.where(n < max_exact, n, large).astype(jnp.int32)


def _bias_tiles(rel_bias, rel0_list, tq, tk):
    span = tq + tk
    tiles = []
    for rel0 in rel0_list:
        u = jnp.arange(span, dtype=jnp.int32)
        rel = rel0 + jnp.where(u < tk, u, u - span)
        table = jnp.transpose(rel_bias[_t5_bucket(rel)].astype(F32))
        heads = table.shape[0]
        skew = jnp.tile(table, (1, tq))[:, :tq * (span - 1)].reshape(heads, tq, span - 1)
        tiles.append(skew[:, :, :tk])
    return jnp.stack(tiles, axis=0)


def _pick(n, candidates):
    for c in candidates:
        if n % c == 0:
            return c
    return n


INDEXER_TILE = 256
ATTN_TILE = 512
FOX_TILE = 1024
FOX_HEADS_PER_STEP = 2


def _mixer_half(x, mods, past, w_in_packed, b_forget, w_out, rel_bias, g_mix, g_ffn):
    b, t, d = x.shape
    n = b * t
    sh1, sc1, gt1, sh2, sc2, _ = mods
    xf = x.reshape(n, d)
    tt = _pick(t, (256, 128, 64, 32, 16))
    h = _norm_mod(xf, g_mix, sc1, sh1, rows_per_batch=t, tt=tt, out_dtype=BF16)

    tm = _pick(n, (1024, 512, 256, 128))

    def proj(col_off, width, tn, **kw):
        return _matmul([h], w_in_packed, col_off=col_off, n=width, tm=tm, tn=tn, **kw)

    off = 0
    qa = proj(off, A_WIDTH, 1024, out_dtypes=(BF16,), out_scale=Q_SCALE); off += A_WIDTH
    ka, ka16 = proj(off, A_KV_WIDTH, 512, out_dtypes=(F32, BF16)); off += A_KV_WIDTH
    va, va16 = proj(off, A_KV_WIDTH, 512, out_dtypes=(F32, BF16)); off += A_KV_WIDTH
    qi = proj(off, IDX_WIDTH, 1024, out_dtypes=(BF16,)); off += IDX_WIDTH
    qb = proj(off, B_WIDTH, 1024, out_dtypes=(BF16,), out_scale=Q_SCALE); off += B_WIDTH
    kb, kb16 = proj(off, B_WIDTH, 1024, out_dtypes=(F32, BF16)); off += B_WIDTH
    vb, vb16 = proj(off, B_WIDTH, 1024, out_dtypes=(F32, BF16)); off += B_WIDTH
    small = proj(off, SMALL_WIDTH, SMALL_WIDTH)

    ki = small[:, :IDX_DIM]
    fl = small[:, IDX_DIM + IDX_HEADS:IDX_DIM + IDX_HEADS + B_HEADS]

    p_len = 0 if past is None else past[0].shape[1]
    t_pad = -(-t // LANES) * LANES
    l_real = p_len + t
    l_pad = p_len + t_pad

    fl_t = jnp.transpose(fl.reshape(b, t, B_HEADS), (0, 2, 1))
    fl_t = jnp.pad(fl_t, ((0, 0), (0, 0), (0, t_pad - t)))
    past_logf_t = None if past is None else jnp.transpose(past[5], (0, 2, 1))
    logf_t, fcum_t = _logf_cumsum(fl_t, b_forget.reshape(B_HEADS, 1), past_logf_t)
    logf = jnp.transpose(logf_t[:, :, :t], (0, 2, 1))

    ki_b = ki.reshape(b, t, IDX_DIM)
    if past is not None:
        ki_b = jnp.concatenate([past[2], ki_b], axis=1)
    ki_t = jnp.transpose(jnp.pad(ki_b, ((0, 0), (0, l_pad - l_real), (0, 0))), (0, 2, 1)).astype(BF16)
    zeros = jnp.zeros_like(ki_t)
    k_lo = jnp.concatenate([ki_t, zeros], axis=1)
    k_hi = jnp.concatenate([zeros, ki_t], axis=1)

    if past is None:
        tile = ATTN_TILE
        assert b == 1 and t % tile == 0 and tile >= MAX_DISTANCE and tile % INDEXER_TILE == 0
        mask = _indexer_mask(qi, small, k_lo, k_hi, rows_per_batch=t, tq=INDEXER_TILE,
                             tkc=INDEXER_TILE, l_real=l_real, p_len=0)
        bias = _bias_tiles(rel_bias, [0, -tile], tile, tile)
        far = rel_bias[_t5_bucket(jnp.int32(-MAX_DISTANCE))].astype(F32)
        bias = (bias - far[None, :, None, None]) * LOG2E
        out_a = _dsa_prompt(qa, ka16, va16, mask, bias, t_tile=tile)
        assert t % FOX_TILE == 0
        fk_tiles = fcum_t.reshape(B_HEADS * (t // FOX_TILE), 1, FOX_TILE)
        out_b = _fox_prompt(qb, kb16, vb16, fk_tiles, t_tile=FOX_TILE,
                            heads_per_step=FOX_HEADS_PER_STEP)
    else:
        tk_a = _pick(p_len, (512, 256, 128))
        mask = _indexer_mask(qi, small, k_lo, k_hi, rows_per_batch=t, tq=t, tkc=l_pad,
                             l_real=l_real, p_len=p_len).reshape(n, l_pad)
        n_a = p_len // tk_a
        bias_a = _bias_tiles(rel_bias, [c * tk_a - p_len for c in range(n_a)], t, tk_a) * LOG2E
        bias_b = _bias_tiles(rel_bias, [0], t, t) * LOG2E
        cache = lambda a: a.reshape(b, -1, HEAD_DIM)
        out_a = _flash_sample("dsa", qa, ka16, va16, cache(past[0]), cache(past[1]), tk_a=tk_a,
                              mask_a=mask[:, :p_len], bias_a=bias_a,
                              mask_b=mask[:, p_len:p_len + t], bias_b=bias_b)
        fq = jnp.transpose(fcum_t[:, :, p_len:p_len + t], (0, 2, 1)).reshape(n, B_HEADS)
        out_b = _flash_sample("fox", qb, kb16, vb16, cache(past[3]), cache(past[4]), tk_a=tk_a,
                              fq=fq, fk_t=fcum_t)

    gate1 = gt1.reshape(1, d) if b == 1 else jnp.repeat(gt1.reshape(b, d), t, axis=0)
    x1 = _matmul([out_a, out_b], w_out, tm=tm, tn=512, res=xf, gate=gate1)
    h2 = _norm_mod(x1, g_ffn, sc2, sh2, rows_per_batch=t, tt=tt, out_dtype=F32)

    state = (ka.reshape(1, b, t, A_KV_HEADS, HEAD_DIM), va.reshape(1, b, t, A_KV_HEADS, HEAD_DIM),
             ki.reshape(1, b, t, IDX_DIM),
             kb.reshape(1, b, t, B_HEADS, HEAD_DIM), vb.reshape(1, b, t, B_HEADS, HEAD_DIM),
             logf.reshape(1, b, t, B_HEADS))
    return x1, h2, state


def kernel(x_prompt, x_sample, cache_a_k, cache_a_v, cache_idx_k, cache_b_k, cache_b_v, cache_b_logf, c_prompt, c_sample, w_in, b_forget, w_out, rel_bias, w_mod, b_mod, g_mix, g_ffn, w_router_group, b_router_group, w_router_expert, b_router_expert, w_gate, w_up, w_down, g_final):
    assert w_in.shape[0] == 1, "single-layer trunk"
    d = x_prompt.shape[-1]
    bp, tp, _ = x_prompt.shape
    bs, ts, _ = x_sample.shape
    n_p, n_s = bp * tp, bs * ts
    layer0 = lambda a: a.reshape(a.shape[1:])

    n_c = bp + bs
    c_rows = -(-n_c // 8) * 8
    c_all = jnp.pad(jnp.concatenate([c_prompt, c_sample], axis=0), ((0, c_rows - n_c), (0, 0)))
    mod = _matmul([c_all], layer0(w_mod), tm=c_rows, tn=512, silu_a=True, bias=b_mod.reshape(1, -1))

    def mods_of(lo, hi):
        m6 = mod[lo:hi].reshape(hi - lo, 6, 1, d)
        return [m6[:, i] for i in range(6)]

    mods_p, mods_s = mods_of(0, bp), mods_of(bp, n_c)

    w = layer0(w_in)
    o1 = A_WIDTH + 2 * A_KV_WIDTH + IDX_WIDTH
    o2 = o1 + IDX_DIM + IDX_HEADS
    o3 = o2 + 3 * B_WIDTH
    tail_pad = SMALL_WIDTH - (IDX_DIM + IDX_HEADS + B_HEADS)
    w_packed = jnp.concatenate(
        [w[:, :o1].astype(BF16), w[:, o2:o3].astype(BF16), w[:, o1:o2].astype(BF16),
         w[:, o3:].astype(BF16), jnp.zeros((d, tail_pad), BF16)], axis=1)

    g_mix2, g_ffn2, g_fin2 = g_mix.reshape(1, d), g_ffn.reshape(1, d), g_final.reshape(1, d)
    past = tuple(layer0(c) for c in (cache_a_k, cache_a_v, cache_idx_k, cache_b_k, cache_b_v,
                                      cache_b_logf))
    b_f, w_o = layer0(b_forget), layer0(w_out)
    x1_p, h2_p, st_p = _mixer_half(x_prompt, mods_p, None, w_packed, b_f, w_o, rel_bias, g_mix2, g_ffn2)
    x1_s, h2_s, st_s = _mixer_half(x_sample, mods_s, past, w_packed, b_f, w_o, rel_bias, g_mix2, g_ffn2)

    n_tok = n_p + n_s
    w_r = jnp.concatenate([layer0(w_router_group), layer0(w_router_expert)], axis=1)
    n_cls = w_r.shape[1]
    w_r = jnp.pad(w_r, ((0, 0), (0, LANES - n_cls)))
    b_r = jnp.pad(jnp.concatenate([layer0(b_router_group), layer0(b_router_expert)]), (0, LANES - n_cls))
    logits = jnp.concatenate(
        [_matmul([hh], w_r, tm=_pick(hh.shape[0], (512, 256, 128)), tn=LANES,
                 bias=b_r.reshape(1, LANES)) for hh in (h2_p, h2_s)], axis=0)
    eidx_t, gate_t = _router(jnp.transpose(logits))
    expert = jnp.transpose(eidx_t)
    gates = jnp.transpose(gate_t)

    n_assign = n_tok * TOP_K_IN_GROUP
    assert n_assign % GATHER_ROWS == 0
    n_sb = n_assign // EXPERT_ROWS + N_EXPERTS
    n_x_rows = -(-n_assign // EXPERT_CHUNK) * EXPERT_CHUNK + N_EXPERTS * EXPERT_CHUNK + EXPERT_ROWS
    x_row_tok, n_live_x, y_row, sb_expert, sb_row0, sb_nch, sb_blk = _dispatch(expert, n_sb, n_x_rows)
    assert d % EXPERT_K_TILE == 0 and d % EXPERT_N_TILE == 0
    xs = _gather_rows([h2_p, h2_s], x_row_tok, n_live_x, BF16, n_slabs=d // EXPERT_K_TILE)
    yb = _experts(xs, sb_expert, sb_row0, sb_nch, sb_blk, layer0(w_gate), layer0(w_up),
                  layer0(w_down), n_sb)
    y_idx = jnp.concatenate([y_row[:, 0], y_row[:, 1]])
    y01 = _gather_rows([yb], y_idx, jnp.full((1,), n_assign // GATHER_ROWS, jnp.int32), F32)

    def finish(x1, lo, gt2, t):
        tt = _pick(t, (128, 64, 32, 16))
        return _combine(x1, y01, gates[lo:lo + x1.shape[0]], gt2, g_fin2, row_off=lo, n_tok=n_tok,
                        rows_per_batch=t, tt=tt)

    y_p = finish(x1_p, 0, mods_p[5], tp).reshape(bp, tp, d)
    y_s = finish(x1_s, n_p, mods_s[5], ts).reshape(bs, ts, d)
    return (y_p, y_s) + st_p + st_s
```

```python
import functools
import math

import numpy as np
import jax
import jax.numpy as jnp
from jax import lax
from jax.experimental import pallas as pl
from jax.experimental.pallas import tpu as pltpu

CHUNK = 64
HEAD_DIM = 128
A_HEADS = 16
A_KV_HEADS = 4
IDX_HEADS = 32
IDX_DIM = 64
TOPK_MAX = 256
B_HEADS = 16
N_BUCKETS = 32
MAX_DISTANCE = 128
N_GROUPS = 4
EXPERTS_PER_GROUP = 8
N_EXPERTS = N_GROUPS * EXPERTS_PER_GROUP
TOP_K_IN_GROUP = 2
EPS = 1e-6

A_WIDTH = A_HEADS * HEAD_DIM
A_KV_WIDTH = A_KV_HEADS * HEAD_DIM
B_WIDTH = B_HEADS * HEAD_DIM
IDX_WIDTH = IDX_HEADS * IDX_DIM
SMALL_WIDTH = 128

LANES = 128
V7X_VMEM_BYTES = 64 * 1024 * 1024
VMEM_LIMIT = 56 * 1024 * 1024

BF16 = jnp.bfloat16
F32 = jnp.float32
NEG_INF = float("-inf")
INT_MIN = -(2 ** 31)
KEY_NEG_INF = int(np.int32(np.float32(-np.inf).view(np.int32)) ^ np.int32(0x7FFFFFFF))
LOG2E = math.log2(math.e)
Q_SCALE = HEAD_DIM ** -0.5 * LOG2E


def _cparams(*sem):
    return pltpu.CompilerParams(dimension_semantics=sem, vmem_limit_bytes=VMEM_LIMIT)


def _matmul_kernel(*refs, n_a, silu_a, epilogue, out_scale, n_out):
    a_refs = refs[:n_a]
    w_ref = refs[n_a]
    rest = refs[n_a + 1:]
    o_refs = rest[len(rest) - n_out:]
    acc = None
    k0 = 0
    for a_ref in a_refs:
        a = a_ref[...]
        if silu_a:
            a = a * jax.nn.sigmoid(a)
        kk = a.shape[1]
        part = jnp.dot(a.astype(BF16), w_ref[k0:k0 + kk, :].astype(BF16),
                       preferred_element_type=F32)
        acc = part if acc is None else acc + part
        k0 += kk
    if epilogue == "bias":
        acc = acc + rest[0][...]
    elif epilogue == "gated_residual":
        acc = rest[0][...] + rest[1][...] * acc
    if out_scale is not None:
        acc = acc * out_scale
    for o_ref in o_refs:
        o_ref[...] = acc.astype(o_ref.dtype)


def _matmul(a_list, w, *, col_off=0, n=None, tm, tn, silu_a=False, bias=None, res=None,
            gate=None, out_dtypes=(F32,), out_scale=None):
    m = a_list[0].shape[0]
    k = w.shape[0]
    n = w.shape[1] - col_off if n is None else n
    assert m % tm == 0 and n % tn == 0 and col_off % tn == 0
    assert sum(a.shape[1] for a in a_list) == k
    cb = col_off // tn
    in_specs = [pl.BlockSpec((tm, a.shape[1]), lambda j, i: (i, 0)) for a in a_list]
    in_specs.append(pl.BlockSpec((k, tn), lambda j, i: (0, j + cb)))
    args = list(a_list) + [w]
    if bias is not None:
        epilogue = "bias"
        in_specs.append(pl.BlockSpec((1, tn), lambda j, i: (0, j)))
        args.append(bias)
    elif res is not None:
        epilogue = "gated_residual"
        in_specs.append(pl.BlockSpec((tm, tn), lambda j, i: (i, j)))
        args.append(res)
        if gate.shape[0] == 1:
            in_specs.append(pl.BlockSpec((1, tn), lambda j, i: (0, j)))
        else:
            in_specs.append(pl.BlockSpec((tm, tn), lambda j, i: (i, j)))
        args.append(gate)
    else:
        epilogue = None
    outs = pl.pallas_call(
        functools.partial(_matmul_kernel, n_a=len(a_list), silu_a=silu_a, epilogue=epilogue,
                          out_scale=out_scale, n_out=len(out_dtypes)),
        grid=(n // tn, m // tm),
        in_specs=in_specs,
        out_specs=[pl.BlockSpec((tm, tn), lambda j, i: (i, j)) for _ in out_dtypes],
        out_shape=[jax.ShapeDtypeStruct((m, n), dt) for dt in out_dtypes],
        compiler_params=_cparams("arbitrary", "arbitrary"),
    )(*args)
    return outs[0] if len(out_dtypes) == 1 else outs


def _rms(x, g):
    return x * lax.rsqrt(jnp.mean(x * x, axis=-1, keepdims=True) + EPS) * g


def _norm_mod_kernel(x_ref, g_ref, sc_ref, sh_ref, o_ref):
    y = _rms(x_ref[...], g_ref[...])
    o_ref[...] = (y * (1.0 + sc_ref[0]) + sh_ref[0]).astype(o_ref.dtype)


def _norm_mod(x, g, sc, sh, *, rows_per_batch, tt, out_dtype):
    n, d = x.shape
    bpb = rows_per_batch // tt
    return pl.pallas_call(
        _norm_mod_kernel,
        grid=(n // tt,),
        in_specs=[pl.BlockSpec((tt, d), lambda i: (i, 0)),
                  pl.BlockSpec((1, d), lambda i: (0, 0)),
                  pl.BlockSpec((1, 1, d), lambda i: (i // bpb, 0, 0)),
                  pl.BlockSpec((1, 1, d), lambda i: (i // bpb, 0, 0))],
        out_specs=pl.BlockSpec((tt, d), lambda i: (i, 0)),
        out_shape=jax.ShapeDtypeStruct((n, d), out_dtype),
        compiler_params=_cparams("arbitrary"),
    )(x, g, sc, sh)


def _combine_kernel(x_ref, y0_ref, y1_ref, gates_ref, gt_ref, g_ref, o_ref):
    g0 = gates_ref[:, 0:1]
    g1 = gates_ref[:, 1:2]
    moe = y0_ref[...] * g0 + y1_ref[...] * g1
    x = x_ref[...] + gt_ref[0] * moe
    o_ref[...] = _rms(x, g_ref[...])


def _combine(x, y01, gates, gt, g_final, *, row_off, n_tok, rows_per_batch, tt):
    n, d = x.shape
    bpb = rows_per_batch // tt
    b0 = row_off // tt
    b1 = (n_tok + row_off) // tt
    row = pl.BlockSpec((tt, d), lambda i: (i, 0))
    return pl.pallas_call(
        _combine_kernel,
        grid=(n // tt,),
        in_specs=[row,
                  pl.BlockSpec((tt, d), lambda i: (i + b0, 0)),
                  pl.BlockSpec((tt, d), lambda i: (i + b1, 0)),
                  pl.BlockSpec((tt, 2), lambda i: (i, 0)),
                  pl.BlockSpec((1, 1, d), lambda i: (i // bpb, 0, 0)),
                  pl.BlockSpec((1, d), lambda i: (0, 0))],
        out_specs=row,
        out_shape=jax.ShapeDtypeStruct((n, d), F32),
        compiler_params=_cparams("arbitrary"),
    )(x, y01, y01, gates, gt, g_final)


GATHER_ROWS = 256


def _gather_kernel(idx_ref, nlive_ref, *refs, src_rows):
    n_src = len(src_rows)
    src_refs = refs[:n_src]
    o_ref, buf_ref, sems = refs[n_src:]
    i = pl.program_id(0)
    n_live = nlive_ref[0]

    def row_copy(src_ref, slot, r, src_row):
        return pltpu.make_async_copy(src_ref.at[pl.ds(src_row, 1)], buf_ref.at[slot, pl.ds(r, 1)],
                                     sems.at[slot])

    def issue_block(blk, slot):
        base = blk * GATHER_ROWS

        def issue(r, priority):
            row = idx_ref[base + r]
            lo = 0
            for k, src_ref in enumerate(src_refs):
                hi = lo + src_rows[k]
                in_range = row >= lo if k == n_src - 1 else jnp.logical_and(row >= lo, row < hi)
                if n_src == 1:
                    row_copy(src_ref, slot, r, row).start(priority=priority)
                else:
                    @pl.when(in_range)
                    def _():
                        row_copy(src_ref, slot, r, row - lo).start(priority=priority)
                lo = hi

        def start(r2, carry):
            issue(2 * r2, 0)
            issue(2 * r2 + 1, 1)
            return carry

        lax.fori_loop(0, GATHER_ROWS // 2, start, 0)

    @pl.when(jnp.logical_and(i == 0, n_live > 0))
    def _():
        issue_block(0, 0)

    @pl.when(i + 1 < n_live)
    def _():
        issue_block(i + 1, (i + 1) % 2)

    @pl.when(i < n_live)
    def _():
        slot = i % 2

        def wait(r, carry):
            row_copy(src_refs[0], slot, r, 0).wait()
            return carry

        lax.fori_loop(0, GATHER_ROWS, wait, 0)
        if len(o_ref.shape) == 2:
            o_ref[...] = buf_ref[slot].astype(o_ref.dtype)
        else:
            w = o_ref.shape[2]
            for k in range(o_ref.shape[0]):
                o_ref[k] = buf_ref[slot, :, k * w:(k + 1) * w].astype(o_ref.dtype)


def _gather_rows(srcs, idx, n_live_blocks, out_dtype, n_slabs=None):
    nr = idx.shape[0]
    src = srcs[0]
    d = src.shape[1]
    assert nr % GATHER_ROWS == 0
    blk = lambda i, idx, nl: jnp.minimum(i, nl[0] - 1)
    if n_slabs is None:
        out_spec = pl.BlockSpec((GATHER_ROWS, d), lambda i, idx, nl: (blk(i, idx, nl), 0))
        out_shape = (nr, d)
    else:
        out_spec = pl.BlockSpec((n_slabs, GATHER_ROWS, d // n_slabs),
                                lambda i, idx, nl: (0, blk(i, idx, nl), 0))
        out_shape = (n_slabs, nr, d // n_slabs)
    grid_spec = pltpu.PrefetchScalarGridSpec(
        num_scalar_prefetch=2,
        grid=(nr // GATHER_ROWS,),
        in_specs=[pl.BlockSpec(memory_space=pl.ANY) for _ in srcs],
        out_specs=out_spec,
        scratch_shapes=[pltpu.VMEM((2, GATHER_ROWS, d), src.dtype), pltpu.SemaphoreType.DMA((2,))],
    )
    return pl.pallas_call(
        functools.partial(_gather_kernel, src_rows=tuple(s.shape[0] for s in srcs)),
        grid_spec=grid_spec,
        out_shape=jax.ShapeDtypeStruct(out_shape, out_dtype),
        compiler_params=_cparams("arbitrary"),
    )(idx, n_live_blocks, *srcs)


def _lane_prefix_sum(x, lane):
    s = 1
    while s < LANES:
        x = x + jnp.where(lane >= s, pltpu.roll(x, s, axis=1), 0.0)
        s *= 2
    return x


def _logf_cumsum_kernel(*refs, p_len, t_pad):
    if p_len:
        fl_ref, bf_ref, past_ref, logf_ref, cum_ref = refs
    else:
        fl_ref, bf_ref, logf_ref, cum_ref = refs
        past_ref = None
    lane = lax.broadcasted_iota(jnp.int32, (B_HEADS, LANES), 1)
    carry = jnp.zeros((B_HEADS, 1), F32)
    for c in range(p_len // LANES):
        sl = slice(c * LANES, (c + 1) * LANES)
        y = _lane_prefix_sum(past_ref[0, :, sl], lane) + carry
        cum_ref[0, :, sl] = y
        carry = y[:, LANES - 1:LANES]
    for c in range(t_pad // LANES):
        sl = slice(c * LANES, (c + 1) * LANES)
        z = fl_ref[0, :, sl] + bf_ref[...]
        logf = jnp.minimum(z, 0.0) - jnp.log1p(jnp.exp(-jnp.abs(z)))
        logf_ref[0, :, sl] = logf
        y = _lane_prefix_sum(logf, lane) + carry
        cum_ref[0, :, p_len + c * LANES:p_len + (c + 1) * LANES] = y
        carry = y[:, LANES - 1:LANES]


def _logf_cumsum(fl_t, b_forget, past_t):
    b, _, t_pad = fl_t.shape
    p_len = 0 if past_t is None else past_t.shape[2]
    in_specs = [pl.BlockSpec((1, B_HEADS, t_pad), lambda i: (i, 0, 0)),
                pl.BlockSpec((B_HEADS, 1), lambda i: (0, 0))]
    args = [fl_t, b_forget]
    if p_len:
        in_specs.append(pl.BlockSpec((1, B_HEADS, p_len), lambda i: (i, 0, 0)))
        args.append(past_t)
    return pl.pallas_call(
        functools.partial(_logf_cumsum_kernel, p_len=p_len, t_pad=t_pad),
        grid=(b,),
        in_specs=in_specs,
        out_specs=[pl.BlockSpec((1, B_HEADS, t_pad), lambda i: (i, 0, 0)),
                   pl.BlockSpec((1, B_HEADS, p_len + t_pad), lambda i: (i, 0, 0))],
        out_shape=[jax.ShapeDtypeStruct((b, B_HEADS, t_pad), F32),
                   jax.ShapeDtypeStruct((b, B_HEADS, p_len + t_pad), F32)],
        compiler_params=_cparams("arbitrary"),
    )(*args)


BISECT_ROWS = 128


def _indexer_kernel(qi_ref, small_ref, klo_ref, khi_ref, mask_ref, qh_ref, wb_ref, keys_ref, *,
                    tq, tkc, n_chunks, l_real, p_len, nq_per_batch, n_sel):
    i = pl.program_id(0)
    row0 = (i % nq_per_batch) * tq
    n_pairs = IDX_HEADS // 2
    sub = tkc // LANES if tkc % LANES == 0 else 0
    for b in range(n_pairs):
        qh_ref[b * tq:(b + 1) * tq, :] = qi_ref[:, b * LANES:(b + 1) * LANES]
    w = small_ref[:, IDX_DIM:IDX_DIM + IDX_HEADS] * (IDX_HEADS ** -0.5)
    w = w * (IDX_DIM ** -0.5)
    wcols = LANES if sub else tkc
    for h in range(IDX_HEADS):
        wb_ref[h] = jnp.broadcast_to(w[:, h:h + 1], (tq, wcols))
    qpos = p_len + row0 + lax.broadcasted_iota(jnp.int32, (tq, 1), 0)
    k_limit = jnp.minimum((qpos // CHUNK + 1) * CHUNK, l_real)
    last_allowed = (p_len + row0 + tq - 1) // CHUNK * CHUNK + CHUNK - 1
    n_allowed = jnp.minimum(last_allowed // tkc + 1, n_chunks)

    def allowed_of(c0):
        kpos = c0 + lax.broadcasted_iota(jnp.int32, (tq, tkc), 1)
        return kpos < k_limit

    def weighted(s, h):
        r = jnp.maximum(s, 0.0)
        if not sub:
            return r * wb_ref[h]
        wv = wb_ref[h]
        return jnp.concatenate([r[:, u * LANES:(u + 1) * LANES] * wv for u in range(sub)], axis=1)

    for c in range(n_chunks):
        c0 = c * tkc

        @pl.when(c < n_allowed)
        def _():
            q_all = qh_ref[...]
            s_lo = jnp.dot(q_all, klo_ref[0, :, c0:c0 + tkc], preferred_element_type=F32)
            s_hi = jnp.dot(q_all, khi_ref[0, :, c0:c0 + tkc], preferred_element_type=F32)
            score = jnp.zeros((tq, tkc), F32)
            for b in range(n_pairs):
                score = score + weighted(s_lo[b * tq:(b + 1) * tq, :], 2 * b)
                score = score + weighted(s_hi[b * tq:(b + 1) * tq, :], 2 * b + 1)
            score = score + 0.0
            score = jnp.where(allowed_of(c0), score, NEG_INF)
            bits = lax.bitcast_convert_type(score, jnp.int32)
            keys_ref[c] = jnp.where(bits < 0, bits ^ 0x7FFFFFFF, bits)

    def count(pred_of_chunk, r0=0, nr=tq):
        def body(c, acc):
            ind = jnp.where(pred_of_chunk(keys_ref[c, r0:r0 + nr, :], c), 1.0, 0.0)
            if not sub:
                return acc + ind
            for u in range(sub):
                acc = acc + ind[:, u * LANES:(u + 1) * LANES]
            return acc

        acc = lax.fori_loop(0, n_allowed, body, jnp.zeros((nr, wcols), F32))
        return jnp.sum(acc, axis=1, keepdims=True)

    k_f = float(n_sel)
    slab = min(tq, BISECT_ROWS)
    t_parts = []
    for r0 in range(0, tq, slab):
        cnt = functools.partial(count, r0=r0, nr=slab)
        t0 = jnp.where(cnt(lambda kc, c: kc >= 0) >= k_f, 0, INT_MIN).astype(jnp.int32)

        def bisect(it, t, cnt=cnt):
            cand = t + jnp.left_shift(jnp.int32(1), 30 - it)
            return jnp.where(cnt(lambda kc, c: kc >= cand) >= k_f, cand, t)

        t_parts.append(lax.fori_loop(0, 31, bisect, t0))
    t = t_parts[0] if len(t_parts) == 1 else jnp.concatenate(t_parts, axis=0)
    cge = count(lambda kc, c: kc >= t)
    excess = jnp.max(jnp.where(t > KEY_NEG_INF, cge, 0.0)) > k_f

    def emit(sel_of_chunk):
        for c in range(n_chunks):
            @pl.when(c < n_allowed)
            def _():
                sel = jnp.logical_and(sel_of_chunk(keys_ref[c], c), allowed_of(c * tkc))
                mask_ref[c] = jnp.where(sel, 0.0, NEG_INF).astype(mask_ref.dtype)

            @pl.when(c >= n_allowed)
            def _():
                mask_ref[c] = jnp.full((tq, tkc), NEG_INF, mask_ref.dtype)

    @pl.when(jnp.logical_not(excess))
    def _():
        emit(lambda kc, c: kc >= t)

    @pl.when(excess)
    def _():
        need = k_f - count(lambda kc, c: kc > t)
        lane_idx = lax.broadcasted_iota(jnp.int32, (tq, tkc), 1)
        m = jnp.zeros((tq, 1), jnp.int32)
        for bit in reversed(range(max(1, (n_chunks * tkc - 1).bit_length()))):
            cand = m + (1 << bit)
            c_lt = count(lambda kc, c: jnp.logical_and(kc == t, c * tkc + lane_idx < cand))
            m = jnp.where(c_lt < need, cand, m)
        emit(lambda kc, c: jnp.logical_or(
            kc > t, jnp.logical_and(kc == t, c * tkc + lane_idx <= m)))


def _indexer_mask(qi, small, k_lo, k_hi, *, rows_per_batch, tq, tkc, l_real, p_len):
    n = qi.shape[0]
    l_pad = k_lo.shape[2]
    n_chunks = l_pad // tkc
    nqb = rows_per_batch // tq
    n_sel = min(TOPK_MAX, l_real // 4)
    wcols = LANES if tkc % LANES == 0 else tkc
    kern = functools.partial(_indexer_kernel, tq=tq, tkc=tkc, n_chunks=n_chunks, l_real=l_real,
                             p_len=p_len, nq_per_batch=nqb, n_sel=n_sel)
    kspec = pl.BlockSpec((1, LANES, l_pad), lambda i: (i // nqb, 0, 0))
    return pl.pallas_call(
        kern,
        grid=(n // tq,),
        in_specs=[pl.BlockSpec((tq, IDX_WIDTH), lambda i: (i, 0)),
                  pl.BlockSpec((tq, SMALL_WIDTH), lambda i: (i, 0)),
                  kspec, kspec],
        out_specs=pl.BlockSpec((n_chunks, tq, tkc), lambda i: (0, i, 0)),
        out_shape=jax.ShapeDtypeStruct((n_chunks, n, tkc), BF16),
        scratch_shapes=[pltpu.VMEM((IDX_HEADS // 2 * tq, LANES), BF16),
                        pltpu.VMEM((IDX_HEADS, tq, wcols), F32),
                        pltpu.VMEM((n_chunks, tq, tkc), jnp.int32)],
        compiler_params=_cparams("arbitrary"),
    )(qi, small, k_lo, k_hi)


def _softmax_step(s, carry, v_tile):
    m, l, acc = carry
    m_new = jnp.maximum(m, jnp.max(s, axis=1, keepdims=True))
    m_safe = jnp.where(m_new == NEG_INF, 0.0, m_new)
    alpha = jnp.exp2(m - m_safe)
    p = jnp.exp2(s - m_safe)
    l = alpha * l + jnp.sum(p, axis=1, keepdims=True)
    acc = alpha * acc + jnp.dot(p.astype(BF16), v_tile, preferred_element_type=F32)
    return m_new, l, acc


def _softmax_init(rows):
    return (jnp.full((rows, 1), NEG_INF, F32), jnp.zeros((rows, 1), F32),
            jnp.zeros((rows, HEAD_DIM), F32))


def _qk(q, k_tile):
    return lax.dot_general(q, k_tile, (((1,), (1,)), ((), ())), preferred_element_type=F32)


def _fox_prompt_kernel(q_ref, k_ref, v_ref, fk_ref, o_ref, *, t_tile, n_heads, nt):
    i = pl.program_id(1)
    lanes = lambda g: slice(g * HEAD_DIM, (g + 1) * HEAD_DIM)
    qs = [q_ref[:, lanes(g)] for g in range(n_heads)]

    def tile(j, carries, diagonal):
        k0 = pl.multiple_of(j * t_tile, t_tile)
        out = []
        for g in range(n_heads):
            s = _qk(qs[g], k_ref[pl.ds(k0, t_tile), lanes(g)]) - fk_ref[g * nt + j] * LOG2E
            if diagonal:
                row = lax.broadcasted_iota(jnp.int32, (t_tile, t_tile), 0)
                col = lax.broadcasted_iota(jnp.int32, (t_tile, t_tile), 1)
                s = jnp.where(col <= row, s, NEG_INF)
            out.append(_softmax_step(s, carries[g], v_ref[pl.ds(k0, t_tile), lanes(g)]))
        return tuple(out)

    init = tuple(_softmax_init(t_tile) for _ in range(n_heads))
    carries = lax.fori_loop(0, i, lambda j, c: tile(j, c, False), init)
    carries = tile(i, carries, True)
    for g in range(n_heads):
        _, l, acc = carries[g]
        o_ref[:, lanes(g)] = (acc / l).astype(o_ref.dtype)


def _fox_prompt(q, k, v, fk_tiles, *, t_tile, heads_per_step):
    t, width = q.shape
    heads = width // HEAD_DIM
    nt = t // t_tile
    gw = heads_per_step * HEAD_DIM
    resident = dict(pipeline_mode=pl.Buffered(1))
    return pl.pallas_call(
        functools.partial(_fox_prompt_kernel, t_tile=t_tile, n_heads=heads_per_step, nt=nt),
        grid=(heads // heads_per_step, nt),
        in_specs=[pl.BlockSpec((t_tile, gw), lambda h, i: (i, h)),
                  pl.BlockSpec((t, gw), lambda h, i: (0, h), **resident),
                  pl.BlockSpec((t, gw), lambda h, i: (0, h), **resident),
                  pl.BlockSpec((heads_per_step * nt, 1, t_tile), lambda h, i: (h, 0, 0))],
        out_specs=pl.BlockSpec((t_tile, gw), lambda h, i: (i, h)),
        out_shape=jax.ShapeDtypeStruct((t, width), BF16),
        compiler_params=_cparams("arbitrary", "arbitrary"),
    )(q, k, v, fk_tiles)


def _dsa_prompt_kernel(q_ref, k_ref, v_ref, mask_ref, bias_ref, o_ref, *, t_tile, grp, mask_w):
    i = pl.program_id(1)
    q = jnp.concatenate([q_ref[:, j * HEAD_DIM:(j + 1) * HEAD_DIM] for j in range(grp)], axis=0)
    n_sub = t_tile // mask_w

    def tile(j, carry, bias_cls):
        k0 = pl.multiple_of(j * t_tile, t_tile)
        s = _qk(q, k_ref[pl.ds(k0, t_tile), :]).reshape(grp, t_tile, t_tile)
        madd = jnp.concatenate([mask_ref[j * n_sub + u] for u in range(n_sub)], axis=1)
        s = s + madd.astype(F32)[None]
        if bias_cls is not None:
            s = s + bias_ref[bias_cls]
        return _softmax_step(s.reshape(grp * t_tile, t_tile), carry, v_ref[pl.ds(k0, t_tile), :])

    carry = lax.fori_loop(0, jnp.maximum(i - 1, 0), lambda j, c: tile(j, c, None),
                          _softmax_init(grp * t_tile))
    carry = lax.fori_loop(jnp.maximum(i - 1, 0), i, lambda j, c: tile(j, c, 1), carry)
    _, l, acc = tile(i, carry, 0)
    out = acc / l
    for j in range(grp):
        o_ref[:, j * HEAD_DIM:(j + 1) * HEAD_DIM] = out[j * t_tile:(j + 1) * t_tile].astype(o_ref.dtype)


def _dsa_prompt(q, k, v, mask, bias, *, t_tile):
    t, width = q.shape
    kv_heads = k.shape[1] // HEAD_DIM
    grp = width // HEAD_DIM // kv_heads
    nt = t // t_tile
    n_mask, _, mask_w = mask.shape
    resident = dict(pipeline_mode=pl.Buffered(1))
    return pl.pallas_call(
        functools.partial(_dsa_prompt_kernel, t_tile=t_tile, grp=grp, mask_w=mask_w),
        grid=(kv_heads, nt),
        in_specs=[pl.BlockSpec((t_tile, grp * HEAD_DIM), lambda g, i: (i, g)),
                  pl.BlockSpec((t, HEAD_DIM), lambda g, i: (0, g), **resident),
                  pl.BlockSpec((t, HEAD_DIM), lambda g, i: (0, g), **resident),
                  pl.BlockSpec((n_mask, t_tile, mask_w), lambda g, i: (0, i, 0)),
                  pl.BlockSpec((2, grp, t_tile, t_tile), lambda g, i: (0, g, 0, 0), **resident)],
        out_specs=pl.BlockSpec((t_tile, grp * HEAD_DIM), lambda g, i: (i, g)),
        out_shape=jax.ShapeDtypeStruct((t, width), BF16),
        compiler_params=_cparams("arbitrary", "arbitrary"),
    )(q, k, v, mask, bias)


def _attend(q_ref, k_of, v_of, bias_of_head, m_ref, l_ref, acc_ref, *, tq, kv_heads, grp):
    for g in range(kv_heads):
        kg = k_of(g).astype(BF16)
        vg = v_of(g).astype(BF16)
        qs = [q_ref[:, (g * grp + j) * HEAD_DIM:(g * grp + j + 1) * HEAD_DIM] for j in range(grp)]
        qg = qs[0] if grp == 1 else jnp.concatenate(qs, axis=0)
        s_all = _qk(qg, kg)
        for j in range(grp):
            h = g * grp + j
            s = s_all[j * tq:(j + 1) * tq, :] + bias_of_head(h)
            m_ref[h], l_ref[h], acc_ref[h] = _softmax_step(s, (m_ref[h], l_ref[h], acc_ref[h]), vg)


def _flash_kernel(*refs, mode, n_a, tq, tk_a, tk_b, p_len, kv_heads, grp):
    heads = kv_heads * grp
    it = iter(refs)
    q_ref = next(it)
    ka_ref, va_ref, kb_ref, vb_ref = next(it), next(it), next(it), next(it)
    if mode == "dsa":
        maska_ref, biasa_ref, maskb_ref, biasb_ref = next(it), next(it), next(it), next(it)
    else:
        fq_ref, fka_ref, fkb_ref = next(it), next(it), next(it)
    o_ref, m_ref, l_ref, acc_ref = next(it), next(it), next(it), next(it)
    j = pl.program_id(1)

    @pl.when(j == 0)
    def _():
        m_ref[...] = jnp.full(m_ref.shape, NEG_INF, F32)
        l_ref[...] = jnp.zeros(l_ref.shape, F32)
        acc_ref[...] = jnp.zeros(acc_ref.shape, F32)

    common = dict(tq=tq, kv_heads=kv_heads, grp=grp)

    @pl.when(j < n_a)
    def _():
        if mode == "dsa":
            madd = maska_ref[...].astype(F32)
            bias = lambda h: biasa_ref[0, h] + madd
        else:
            bias = lambda h: (fq_ref[:, h:h + 1] - fka_ref[0, h:h + 1, :]) * LOG2E
        k_of = lambda g: ka_ref[0, pl.ds(g, tk_a, stride=kv_heads), :]
        v_of = lambda g: va_ref[0, pl.ds(g, tk_a, stride=kv_heads), :]
        _attend(q_ref, k_of, v_of, bias, m_ref, l_ref, acc_ref, **common)

    @pl.when(j == n_a)
    def _():
        if mode == "dsa":
            madd = maskb_ref[...].astype(F32)
            bias = lambda h: biasb_ref[0, h] + madd
        else:
            row = lax.broadcasted_iota(jnp.int32, (tq, tk_b), 0)
            col = lax.broadcasted_iota(jnp.int32, (tq, tk_b), 1)
            causal = jnp.where(col <= row, 0.0, NEG_INF)
            bias = lambda h: (fq_ref[:, h:h + 1] - fkb_ref[0, h:h + 1, :tk_b]) * LOG2E + causal
        k_of = lambda g: kb_ref[:, g * HEAD_DIM:(g + 1) * HEAD_DIM]
        v_of = lambda g: vb_ref[:, g * HEAD_DIM:(g + 1) * HEAD_DIM]
        _attend(q_ref, k_of, v_of, bias, m_ref, l_ref, acc_ref, **common)
        for h in range(heads):
            o_ref[:, h * HEAD_DIM:(h + 1) * HEAD_DIM] = (acc_ref[h] / l_ref[h]).astype(o_ref.dtype)


def _flash_sample(mode, q, k_new, v_new, k_past, v_past, *, tk_a, mask_a=None, bias_a=None,
                  mask_b=None, bias_b=None, fq=None, fk_t=None):
    n, width = q.shape
    kv_w = k_new.shape[1]
    b = k_past.shape[0]
    tq = n // b
    heads = width // HEAD_DIM
    kv_heads = kv_w // HEAD_DIM
    p_len = k_past.shape[1] // kv_heads
    grp = heads // kv_heads
    n_a = p_len // tk_a
    a_blk = lambda j: jnp.minimum(j, n_a - 1)

    in_specs = [pl.BlockSpec((tq, width), lambda g, j: (g, 0))]
    spec = pl.BlockSpec((1, tk_a * kv_heads, HEAD_DIM), lambda g, j: (g, a_blk(j), 0))
    in_specs += [spec, spec]
    spec = pl.BlockSpec((tq, kv_w), lambda g, j: (g, 0))
    in_specs += [spec, spec]
    args = [q, k_past, v_past, k_new, v_new]
    if mode == "dsa":
        in_specs += [pl.BlockSpec((tq, tk_a), lambda g, j: (g, a_blk(j))),
                     pl.BlockSpec((1, heads, tq, tk_a), lambda g, j: (a_blk(j), 0, 0, 0)),
                     pl.BlockSpec((tq, tq), lambda g, j: (g, 0)),
                     pl.BlockSpec((1, heads, tq, tq), lambda g, j: (0, 0, 0, 0))]
        args += [mask_a, bias_a, mask_b, bias_b]
    else:
        in_specs += [pl.BlockSpec((tq, heads), lambda g, j: (g, 0)),
                     pl.BlockSpec((1, heads, tk_a), lambda g, j: (g, 0, a_blk(j))),
                     pl.BlockSpec((1, heads, LANES), lambda g, j: (g, 0, p_len // LANES))]
        args += [fq, fk_t, fk_t]

    kern = functools.partial(_flash_kernel, mode=mode, n_a=n_a, tq=tq, tk_a=tk_a, tk_b=tq,
                             p_len=p_len, kv_heads=kv_heads, grp=grp)
    return pl.pallas_call(
        kern,
        grid=(b, n_a + 1),
        in_specs=in_specs,
        out_specs=pl.BlockSpec((tq, width), lambda g, j: (g, 0)),
        out_shape=jax.ShapeDtypeStruct((n, width), BF16),
        scratch_shapes=[pltpu.VMEM((heads, tq, 1), F32),
                        pltpu.VMEM((heads, tq, 1), F32),
                        pltpu.VMEM((heads, tq, HEAD_DIM), F32)],
        compiler_params=_cparams("arbitrary", "arbitrary"),
    )(*args)


def _router_kernel(lt_ref, eidx_ref, gate_ref):
    row = lambda r: lt_ref[r:r + 1, :]
    g = [row(r) for r in range(N_GROUPS)]
    gmax = functools.reduce(jnp.maximum, g)
    gsel = jnp.full(gmax.shape, N_GROUPS - 1, jnp.int32)
    for r in reversed(range(N_GROUPS - 1)):
        gsel = jnp.where(g[r] == gmax, r, gsel)
    denom = functools.reduce(lambda a, b: a + b, [jnp.exp(x - gmax) for x in g])
    g_prob = 1.0 / denom
    e_in = []
    for kk in range(EXPERTS_PER_GROUP):
        v = row(N_GROUPS + (N_GROUPS - 1) * EXPERTS_PER_GROUP + kk)
        for r in reversed(range(N_GROUPS - 1)):
            v = jnp.where(gsel == r, row(N_GROUPS + r * EXPERTS_PER_GROUP + kk), v)
        e_in.append(v)
    v1 = functools.reduce(jnp.maximum, e_in)
    i1 = jnp.full(v1.shape, EXPERTS_PER_GROUP - 1, jnp.int32)
    for kk in reversed(range(EXPERTS_PER_GROUP - 1)):
        i1 = jnp.where(e_in[kk] == v1, kk, i1)
    rest = [jnp.where(i1 == kk, NEG_INF, e_in[kk]) for kk in range(EXPERTS_PER_GROUP)]
    v2 = functools.reduce(jnp.maximum, rest)
    i2 = jnp.full(v2.shape, EXPERTS_PER_GROUP - 1, jnp.int32)
    for kk in reversed(range(EXPERTS_PER_GROUP - 1)):
        i2 = jnp.where(jnp.logical_and(rest[kk] == v2, i1 != kk), kk, i2)
    e2 = jnp.exp(v2 - v1)
    inv = 1.0 / (1.0 + e2)
    eidx_ref[0:1, :] = gsel * EXPERTS_PER_GROUP + i1
    eidx_ref[1:2, :] = gsel * EXPERTS_PER_GROUP + i2
    gate_ref[0:1, :] = inv * g_prob
    gate_ref[1:2, :] = (e2 * inv) * g_prob


def _router(logits_t):
    n = logits_t.shape[1]
    full = lambda r: pl.BlockSpec((r, n), lambda: (0, 0))
    return pl.pallas_call(
        _router_kernel,
        in_specs=[full(logits_t.shape[0])],
        out_specs=[full(TOP_K_IN_GROUP), full(TOP_K_IN_GROUP)],
        out_shape=[jax.ShapeDtypeStruct((TOP_K_IN_GROUP, n), jnp.int32),
                   jax.ShapeDtypeStruct((TOP_K_IN_GROUP, n), F32)],
        compiler_params=pltpu.CompilerParams(vmem_limit_bytes=VMEM_LIMIT),
    )(logits_t)


EXPERT_CHUNK = 256
EXPERT_CHUNKS_PER_BLOCK = 4
EXPERT_ROWS = EXPERT_CHUNK * EXPERT_CHUNKS_PER_BLOCK
EXPERT_K_TILE = 512
EXPERT_N_TILE = 1024


def _experts_kernel(sb_e_ref, sb_row0_ref, sb_nch_ref, sb_blk_ref, x_hbm, w1_ref, w3_ref, w2_ref,
                    o_ref, x_ref, a_ref, b_ref, act_ref, sem, *, nk):
    del sb_e_ref, sb_blk_ref
    s = pl.program_id(0)
    t = pl.program_id(1)
    n_ch = sb_nch_ref[s]

    @pl.when(jnp.logical_and(t == 0, n_ch > 0))
    def _():
        row0 = pl.multiple_of(sb_row0_ref[s], EXPERT_CHUNK)
        copies = [pltpu.make_async_copy(x_hbm.at[k, pl.ds(row0, EXPERT_ROWS)], x_ref.at[k], sem)
                  for k in range(nk)]
        for cp in copies:
            cp.start()
        for cp in copies:
            cp.wait()

    for nc in range(1, EXPERT_CHUNKS_PER_BLOCK + 1):
        rows = nc * EXPERT_CHUNK

        @pl.when(jnp.logical_and(t < nk, n_ch == nc))
        def _():
            xc = x_ref[t, 0:rows, :]
            a = jnp.dot(xc, w1_ref[0].astype(BF16), preferred_element_type=F32)
            b = jnp.dot(xc, w3_ref[0].astype(BF16), preferred_element_type=F32)

            @pl.when(t == 0)
            def _():
                a_ref[0:rows, :] = a
                b_ref[0:rows, :] = b

            @pl.when(t > 0)
            def _():
                a_ref[0:rows, :] += a
                b_ref[0:rows, :] += b

            @pl.when(t == nk - 1)
            def _():
                g = a_ref[0:rows, :]
                act_ref[0:rows, :] = (g * jax.nn.sigmoid(g) * b_ref[0:rows, :]).astype(BF16)

        @pl.when(jnp.logical_and(t >= nk, n_ch == nc))
        def _():
            o_ref[0:rows, :] = jnp.dot(act_ref[0:rows, :], w2_ref[0].astype(BF16),
                                       preferred_element_type=F32)


def _experts(xs, sb_expert, sb_row0, sb_nch, sb_blk, w1, w3, w2, n_sb):
    nk = xs.shape[0]
    d = nk * EXPERT_K_TILE
    d_e = w1.shape[2]
    nn = d // EXPERT_N_TILE
    r = EXPERT_ROWS
    grid_spec = pltpu.PrefetchScalarGridSpec(
        num_scalar_prefetch=4,
        grid=(n_sb, nk + nn),
        in_specs=[
            pl.BlockSpec(memory_space=pl.ANY),
            pl.BlockSpec((1, EXPERT_K_TILE, d_e),
                         lambda s, t, e, r0, nch, blk: (e[s], jnp.minimum(t, nk - 1), 0)),
            pl.BlockSpec((1, EXPERT_K_TILE, d_e),
                         lambda s, t, e, r0, nch, blk: (e[s], jnp.minimum(t, nk - 1), 0)),
            pl.BlockSpec((1, d_e, EXPERT_N_TILE),
                         lambda s, t, e, r0, nch, blk: (e[s], 0, jnp.maximum(t - nk, 0))),
        ],
        out_specs=pl.BlockSpec(
            (r, EXPERT_N_TILE),
            lambda s, t, e, r0, nch, blk: (blk[s], jnp.where(nch[s] > 0, jnp.maximum(t - nk, 0), 0))),
        scratch_shapes=[pltpu.VMEM((nk, r, EXPERT_K_TILE), BF16),
                        pltpu.VMEM((r, d_e), F32),
                        pltpu.VMEM((r, d_e), F32),
                        pltpu.VMEM((r, d_e), BF16),
                        pltpu.SemaphoreType.DMA(())],
    )
    return pl.pallas_call(
        functools.partial(_experts_kernel, nk=nk),
        grid_spec=grid_spec,
        out_shape=jax.ShapeDtypeStruct(((n_sb + 1) * r, d), F32),
        compiler_params=_cparams("arbitrary", "arbitrary"),
    )(sb_expert, sb_row0, sb_nch, sb_blk, xs, w1, w3, w2)


def _dispatch(expert, n_sb, n_x_rows):
    n = expert.shape[0]
    a = n * TOP_K_IN_GROUP
    ch, r = EXPERT_CHUNK, EXPERT_ROWS
    flat_e = expert.reshape(a)
    flat_tok = jnp.arange(a, dtype=jnp.int32) // TOP_K_IN_GROUP
    order = jnp.argsort(flat_e)
    se = flat_e[order]
    counts = jnp.bincount(flat_e, length=N_EXPERTS).astype(jnp.int32)
    seg_start = jnp.cumsum(counts) - counts
    rank = jnp.arange(a, dtype=jnp.int32) - seg_start[se]
    nch_e = (counts + ch - 1) // ch
    xch_end = jnp.cumsum(nch_e)
    x_start = (xch_end - nch_e) * ch
    nsb_e = (counts + r - 1) // r
    sb_end = jnp.cumsum(nsb_e)
    sb_start = sb_end - nsb_e
    x_row_tok = jnp.zeros((n_x_rows,), jnp.int32).at[x_start[se] + rank].set(flat_tok[order])
    n_live_x_blocks = (xch_end[-1] * ch + GATHER_ROWS - 1) // GATHER_ROWS
    y_row = jnp.zeros((a,), jnp.int32).at[order].set(sb_start[se] * r + rank)
    y_row = y_row.reshape(n, TOP_K_IN_GROUP)
    total = sb_end[-1]
    s_idx = jnp.arange(n_sb, dtype=jnp.int32)
    e_of_sb = jnp.minimum(jnp.searchsorted(sb_end, s_idx, side="right"), N_EXPERTS - 1).astype(jnp.int32)
    k_in_e = s_idx - sb_start[e_of_sb]
    rows_of_sb = jnp.clip(counts[e_of_sb] - k_in_e * r, 0, r)
    live = s_idx < total
    last_e = e_of_sb[jnp.maximum(total - 1, 0)]
    sb_expert = jnp.where(live, e_of_sb, last_e).astype(jnp.int32)
    sb_nch = jnp.where(live, (rows_of_sb + ch - 1) // ch, 0).astype(jnp.int32)
    sb_row0 = jnp.where(live, x_start[e_of_sb] + k_in_e * r, 0).astype(jnp.int32)
    sb_blk = jnp.where(live, s_idx, n_sb).astype(jnp.int32)
    return (x_row_tok, n_live_x_blocks.astype(jnp.int32).reshape(1), y_row, sb_expert, sb_row0,
            sb_nch, sb_blk)


def _t5_bucket(rel):
    half = N_BUCKETS // 2
    max_exact = half // 2
    bucket = jnp.where(rel > 0, half, 0).astype(jnp.int32)
    n = jnp.abs(rel)
    large = max_exact + (jnp.log(jnp.maximum(n, 1).astype(jnp.float32) / max_exact)
                         / math.log(MAX_DISTANCE / max_exact) * (half - max_exact)).astype(jnp.int32)
    large = jnp.minimum(large, half - 1)
    return bucket + jnp.where(n < max_exact, n, large).astype(jnp.int32)


def _bias_tiles(rel_bias, rel0_list, tq, tk):
    span = tq + tk
    tiles = []
    for rel0 in rel0_list:
        u = jnp.arange(span, dtype=jnp.int32)
        rel = rel0 + jnp.where(u < tk, u, u - span)
        table = jnp.transpose(rel_bias[_t5_bucket(rel)].astype(F32))
        heads = table.shape[0]
        skew = jnp.tile(table, (1, tq))[:, :tq * (span - 1)].reshape(heads, tq, span - 1)
        tiles.append(skew[:, :, :tk])
    return jnp.stack(tiles, axis=0)


def _pick(n, candidates):
    for c in candidates:
        if n % c == 0:
            return c
    return n


INDEXER_TILE = 256
ATTN_TILE = 512
FOX_TILE = 1024
FOX_HEADS_PER_STEP = 2


def _mixer_half(x, mods, past, w_in_packed, b_forget, w_out, rel_bias, g_mix, g_ffn):
    b, t, d = x.shape
    n = b * t
    sh1, sc1, gt1, sh2, sc2, _ = mods
    xf = x.reshape(n, d)
    tt = _pick(t, (256, 128, 64, 32, 16))
    h = _norm_mod(xf, g_mix, sc1, sh1, rows_per_batch=t, tt=tt, out_dtype=BF16)

    tm = _pick(n, (1024, 512, 256, 128))

    def proj(col_off, width, tn, **kw):
        return _matmul([h], w_in_packed, col_off=col_off, n=width, tm=tm, tn=tn, **kw)

    off = 0
    qa = proj(off, A_WIDTH, 1024, out_dtypes=(BF16,), out_scale=Q_SCALE); off += A_WIDTH
    ka, ka16 = proj(off, A_KV_WIDTH, 512, out_dtypes=(F32, BF16)); off += A_KV_WIDTH
    va, va16 = proj(off, A_KV_WIDTH, 512, out_dtypes=(F32, BF16)); off += A_KV_WIDTH
    qi = proj(off, IDX_WIDTH, 1024, out_dtypes=(BF16,)); off += IDX_WIDTH
    qb = proj(off, B_WIDTH, 1024, out_dtypes=(BF16,), out_scale=Q_SCALE); off += B_WIDTH
    kb, kb16 = proj(off, B_WIDTH, 1024, out_dtypes=(F32, BF16)); off += B_WIDTH
    vb, vb16 = proj(off, B_WIDTH, 1024, out_dtypes=(F32, BF16)); off += B_WIDTH
    small = proj(off, SMALL_WIDTH, SMALL_WIDTH)

    ki = small[:, :IDX_DIM]
    fl = small[:, IDX_DIM + IDX_HEADS:IDX_DIM + IDX_HEADS + B_HEADS]

    p_len = 0 if past is None else past[0].shape[1]
    t_pad = -(-t // LANES) * LANES
    l_real = p_len + t
    l_pad = p_len + t_pad

    fl_t = jnp.transpose(fl.reshape(b, t, B_HEADS), (0, 2, 1))
    fl_t = jnp.pad(fl_t, ((0, 0), (0, 0), (0, t_pad - t)))
    past_logf_t = None if past is None else jnp.transpose(past[5], (0, 2, 1))
    logf_t, fcum_t = _logf_cumsum(fl_t, b_forget.reshape(B_HEADS, 1), past_logf_t)
    logf = jnp.transpose(logf_t[:, :, :t], (0, 2, 1))

    ki_b = ki.reshape(b, t, IDX_DIM)
    if past is not None:
        ki_b = jnp.concatenate([past[2], ki_b], axis=1)
    ki_t = jnp.transpose(jnp.pad(ki_b, ((0, 0), (0, l_pad - l_real), (0, 0))), (0, 2, 1)).astype(BF16)
    zeros = jnp.zeros_like(ki_t)
    k_lo = jnp.concatenate([ki_t, zeros], axis=1)
    k_hi = jnp.concatenate([zeros, ki_t], axis=1)

    if past is None:
        tile = ATTN_TILE
        assert b == 1 and t % tile == 0 and tile >= MAX_DISTANCE and tile % INDEXER_TILE == 0
        mask = _indexer_mask(qi, small, k_lo, k_hi, rows_per_batch=t, tq=INDEXER_TILE,
                             tkc=INDEXER_TILE, l_real=l_real, p_len=0)
        bias = _bias_tiles(rel_bias, [0, -tile], tile, tile)
        far = rel_bias[_t5_bucket(jnp.int32(-MAX_DISTANCE))].astype(F32)
        bias = (bias - far[None, :, None, None]) * LOG2E
        out_a = _dsa_prompt(qa, ka16, va16, mask, bias, t_tile=tile)
        assert t % FOX_TILE == 0
        fk_tiles = fcum_t.reshape(B_HEADS * (t // FOX_TILE), 1, FOX_TILE)
        out_b = _fox_prompt(qb, kb16, vb16, fk_tiles, t_tile=FOX_TILE,
                            heads_per_step=FOX_HEADS_PER_STEP)
    else:
        tk_a = _pick(p_len, (1024, 512, 256, 128))
        mask = _indexer_mask(qi, small, k_lo, k_hi, rows_per_batch=t, tq=t, tkc=l_pad,
                             l_real=l_real, p_len=p_len).reshape(n, l_pad)
        n_a = p_len // tk_a
        bias_a = _bias_tiles(rel_bias, [c * tk_a - p_len for c in range(n_a)], t, tk_a) * LOG2E
        bias_b = _bias_tiles(rel_bias, [0], t, t) * LOG2E
        cache = lambda a: a.reshape(b, -1, HEAD_DIM)
        out_a = _flash_sample("dsa", qa, ka16, va16, cache(past[0]), cache(past[1]), tk_a=tk_a,
                              mask_a=mask[:, :p_len], bias_a=bias_a,
                              mask_b=mask[:, p_len:p_len + t], bias_b=bias_b)
        fq = jnp.transpose(fcum_t[:, :, p_len:p_len + t], (0, 2, 1)).reshape(n, B_HEADS)
        out_b = _flash_sample("fox", qb, kb16, vb16, cache(past[3]), cache(past[4]), tk_a=tk_a,
                              fq=fq, fk_t=fcum_t)

    gate1 = gt1.reshape(1, d) if b == 1 else jnp.repeat(gt1.reshape(b, d), t, axis=0)
    x1 = _matmul([out_a, out_b], w_out, tm=tm, tn=512, res=xf, gate=gate1)
    h2 = _norm_mod(x1, g_ffn, sc2, sh2, rows_per_batch=t, tt=tt, out_dtype=F32)

    state = (ka.reshape(1, b, t, A_KV_HEADS, HEAD_DIM), va.reshape(1, b, t, A_KV_HEADS, HEAD_DIM),
             ki.reshape(1, b, t, IDX_DIM),
             kb.reshape(1, b, t, B_HEADS, HEAD_DIM), vb.reshape(1, b, t, B_HEADS, HEAD_DIM),
             logf.reshape(1, b, t, B_HEADS))
    return x1, h2, state


def kernel(x_prompt, x_sample, cache_a_k, cache_a_v, cache_idx_k, cache_b_k, cache_b_v, cache_b_logf, c_prompt, c_sample, w_in, b_forget, w_out, rel_bias, w_mod, b_mod, g_mix, g_ffn, w_router_group, b_router_group, w_router_expert, b_router_expert, w_gate, w_up, w_down, g_final):
    assert w_in.shape[0] == 1, "single-layer trunk"
    d = x_prompt.shape[-1]
    bp, tp, _ = x_prompt.shape
    bs, ts, _ = x_sample.shape
    n_p, n_s = bp * tp, bs * ts
    layer0 = lambda a: a.reshape(a.shape[1:])

    n_c = bp + bs
    c_rows = -(-n_c // 8) * 8
    c_all = jnp.pad(jnp.concatenate([c_prompt, c_sample], axis=0), ((0, c_rows - n_c), (0, 0)))
    mod = _matmul([c_all], layer0(w_mod), tm=c_rows, tn=512, silu_a=True, bias=b_mod.reshape(1, -1))

    def mods_of(lo, hi):
        m6 = mod[lo:hi].reshape(hi - lo, 6, 1, d)
        return [m6[:, i] for i in range(6)]

    mods_p, mods_s = mods_of(0, bp), mods_of(bp, n_c)

    w = layer0(w_in)
    o1 = A_WIDTH + 2 * A_KV_WIDTH + IDX_WIDTH
    o2 = o1 + IDX_DIM + IDX_HEADS
    o3 = o2 + 3 * B_WIDTH
    tail_pad = SMALL_WIDTH - (IDX_DIM + IDX_HEADS + B_HEADS)
    w_packed = jnp.concatenate(
        [w[:, :o1].astype(BF16), w[:, o2:o3].astype(BF16), w[:, o1:o2].astype(BF16),
         w[:, o3:].astype(BF16), jnp.zeros((d, tail_pad), BF16)], axis=1)

    g_mix2, g_ffn2, g_fin2 = g_mix.reshape(1, d), g_ffn.reshape(1, d), g_final.reshape(1, d)
    past = tuple(layer0(c) for c in (cache_a_k, cache_a_v, cache_idx_k, cache_b_k, cache_b_v,
                                      cache_b_logf))
    b_f, w_o = layer0(b_forget), layer0(w_out)
    x1_p, h2_p, st_p = _mixer_half(x_prompt, mods_p, None, w_packed, b_f, w_o, rel_bias, g_mix2, g_ffn2)
    x1_s, h2_s, st_s = _mixer_half(x_sample, mods_s, past, w_packed, b_f, w_o, rel_bias, g_mix2, g_ffn2)

    n_tok = n_p + n_s
    w_r = jnp.concatenate([layer0(w_router_group), layer0(w_router_expert)], axis=1)
    n_cls = w_r.shape[1]
    w_r = jnp.pad(w_r, ((0, 0), (0, LANES - n_cls)))
    b_r = jnp.pad(jnp.concatenate([layer0(b_router_group), layer0(b_router_expert)]), (0, LANES - n_cls))
    logits = jnp.concatenate(
        [_matmul([hh], w_r, tm=_pick(hh.shape[0], (512, 256, 128)), tn=LANES,
                 bias=b_r.reshape(1, LANES)) for hh in (h2_p, h2_s)], axis=0)
    eidx_t, gate_t = _router(jnp.transpose(logits))
    expert = jnp.transpose(eidx_t)
    gates = jnp.transpose(gate_t)

    n_assign = n_tok * TOP_K_IN_GROUP
    assert n_assign % GATHER_ROWS == 0
    n_sb = n_assign // EXPERT_ROWS + N_EXPERTS
    n_x_rows = -(-n_assign // EXPERT_CHUNK) * EXPERT_CHUNK + N_EXPERTS * EXPERT_CHUNK + EXPERT_ROWS
    x_row_tok, n_live_x, y_row, sb_expert, sb_row0, sb_nch, sb_blk = _dispatch(expert, n_sb, n_x_rows)
    assert d % EXPERT_K_TILE == 0 and d % EXPERT_N_TILE == 0
    xs = _gather_rows([h2_p, h2_s], x_row_tok, n_live_x, BF16, n_slabs=d // EXPERT_K_TILE)
    yb = _experts(xs, sb_expert, sb_row0, sb_nch, sb_blk, layer0(w_gate), layer0(w_up),
                  layer0(w_down), n_sb)
    y_idx = jnp.concatenate([y_row[:, 0], y_row[:, 1]])
    y01 = _gather_rows([yb], y_idx, jnp.full((1,), n_assign // GATHER_ROWS, jnp.int32), F32)

    def finish(x1, lo, gt2, t):
        tt = _pick(t, (128, 64, 32, 16))
        return _combine(x1, y01, gates[lo:lo + x1.shape[0]], gt2, g_fin2, row_off=lo, n_tok=n_tok,
                        rows_per_batch=t, tt=tt)

    y_p = finish(x1_p, 0, mods_p[5], tp).reshape(bp, tp, d)
    y_s = finish(x1_s, n_p, mods_s[5], ts).reshape(bs, ts, d)
    return (y_p, y_s) + st_p + st_s
```

```python
import functools
import math

import numpy as np
import jax
import jax.numpy as jnp
from jax import lax
from jax.experimental import pallas as pl
from jax.experimental.pallas import tpu as pltpu

CHUNK = 64
HEAD_DIM = 128
A_HEADS = 16
A_KV_HEADS = 4
IDX_HEADS = 32
IDX_DIM = 64
TOPK_MAX = 256
B_HEADS = 16
N_BUCKETS = 32
MAX_DISTANCE = 128
N_GROUPS = 4
EXPERTS_PER_GROUP = 8
N_EXPERTS = N_GROUPS * EXPERTS_PER_GROUP
TOP_K_IN_GROUP = 2
EPS = 1e-6

A_WIDTH = A_HEADS * HEAD_DIM
A_KV_WIDTH = A_KV_HEADS * HEAD_DIM
B_WIDTH = B_HEADS * HEAD_DIM
IDX_WIDTH = IDX_HEADS * IDX_DIM
SMALL_WIDTH = 128

LANES = 128
V7X_VMEM_BYTES = 64 * 1024 * 1024
VMEM_LIMIT = 56 * 1024 * 1024

BF16 = jnp.bfloat16
F32 = jnp.float32
NEG_INF = float("-inf")
INT_MIN = -(2 ** 31)
KEY_NEG_INF = int(np.int32(np.float32(-np.inf).view(np.int32)) ^ np.int32(0x7FFFFFFF))
LOG2E = math.log2(math.e)
Q_SCALE = HEAD_DIM ** -0.5 * LOG2E


def _cparams(*sem):
    return pltpu.CompilerParams(dimension_semantics=sem, vmem_limit_bytes=VMEM_LIMIT)


def _matmul_kernel(*refs, n_a, silu_a, epilogue, out_scale, n_out):
    a_refs = refs[:n_a]
    w_ref = refs[n_a]
    rest = refs[n_a + 1:]
    o_refs = rest[len(rest) - n_out:]
    acc = None
    k0 = 0
    for a_ref in a_refs:
        a = a_ref[...]
        if silu_a:
            a = a * jax.nn.sigmoid(a)
        kk = a.shape[1]
        part = jnp.dot(a.astype(BF16), w_ref[k0:k0 + kk, :].astype(BF16),
                       preferred_element_type=F32)
        acc = part if acc is None else acc + part
        k0 += kk
    if epilogue == "bias":
        acc = acc + rest[0][...]
    elif epilogue == "gated_residual":
        acc = rest[0][...] + rest[1][...] * acc
    if out_scale is not None:
        acc = acc * out_scale
    for o_ref in o_refs:
        o_ref[...] = acc.astype(o_ref.dtype)


def _matmul(a_list, w, *, col_off=0, n=None, tm, tn, silu_a=False, bias=None, res=None,
            gate=None, out_dtypes=(F32,), out_scale=None):
    m = a_list[0].shape[0]
    k = w.shape[0]
    n = w.shape[1] - col_off if n is None else n
    assert m % tm == 0 and n % tn == 0 and col_off % tn == 0
    assert sum(a.shape[1] for a in a_list) == k
    cb = col_off // tn
    in_specs = [pl.BlockSpec((tm, a.shape[1]), lambda j, i: (i, 0)) for a in a_list]
    in_specs.append(pl.BlockSpec((k, tn), lambda j, i: (0, j + cb)))
    args = list(a_list) + [w]
    if bias is not None:
        epilogue = "bias"
        in_specs.append(pl.BlockSpec((1, tn), lambda j, i: (0, j)))
        args.append(bias)
    elif res is not None:
        epilogue = "gated_residual"
        in_specs.append(pl.BlockSpec((tm, tn), lambda j, i: (i, j)))
        args.append(res)
        if gate.shape[0] == 1:
            in_specs.append(pl.BlockSpec((1, tn), lambda j, i: (0, j)))
        else:
            in_specs.append(pl.BlockSpec((tm, tn), lambda j, i: (i, j)))
        args.append(gate)
    else:
        epilogue = None
    outs = pl.pallas_call(
        functools.partial(_matmul_kernel, n_a=len(a_list), silu_a=silu_a, epilogue=epilogue,
                          out_scale=out_scale, n_out=len(out_dtypes)),
        grid=(n // tn, m // tm),
        in_specs=in_specs,
        out_specs=[pl.BlockSpec((tm, tn), lambda j, i: (i, j)) for _ in out_dtypes],
        out_shape=[jax.ShapeDtypeStruct((m, n), dt) for dt in out_dtypes],
        compiler_params=_cparams("arbitrary", "arbitrary"),
    )(*args)
    return outs[0] if len(out_dtypes) == 1 else outs


def _rms(x, g):
    return x * lax.rsqrt(jnp.mean(x * x, axis=-1, keepdims=True) + EPS) * g


def _norm_mod_kernel(x_ref, g_ref, sc_ref, sh_ref, o_ref):
    y = _rms(x_ref[...], g_ref[...])
    o_ref[...] = (y * (1.0 + sc_ref[0]) + sh_ref[0]).astype(o_ref.dtype)


def _norm_mod(x, g, sc, sh, *, rows_per_batch, tt, out_dtype):
    n, d = x.shape
    bpb = rows_per_batch // tt
    return pl.pallas_call(
        _norm_mod_kernel,
        grid=(n // tt,),
        in_specs=[pl.BlockSpec((tt, d), lambda i: (i, 0)),
                  pl.BlockSpec((1, d), lambda i: (0, 0)),
                  pl.BlockSpec((1, 1, d), lambda i: (i // bpb, 0, 0)),
                  pl.BlockSpec((1, 1, d), lambda i: (i // bpb, 0, 0))],
        out_specs=pl.BlockSpec((tt, d), lambda i: (i, 0)),
        out_shape=jax.ShapeDtypeStruct((n, d), out_dtype),
        compiler_params=_cparams("arbitrary"),
    )(x, g, sc, sh)


def _combine_kernel(idx0_ref, idx1_ref, x_ref, gates_ref, gt_ref, g_ref, yb_hbm, o_ref, ybuf_ref,
                    sems, *, tt):
    i = pl.program_id(0)
    n_blocks = pl.num_programs(0)

    def row_copy(slot, which, r, src_row):
        return pltpu.make_async_copy(yb_hbm.at[pl.ds(src_row, 1)],
                                     ybuf_ref.at[slot, which, pl.ds(r, 1)], sems.at[slot])

    def issue_block(blk, slot):
        base = blk * tt

        def start(r, carry):
            row_copy(slot, 0, r, idx0_ref[base + r]).start(priority=0)
            row_copy(slot, 1, r, idx1_ref[base + r]).start(priority=1)
            return carry

        lax.fori_loop(0, tt, start, 0)

    @pl.when(i == 0)
    def _():
        issue_block(0, 0)

    @pl.when(i + 1 < n_blocks)
    def _():
        issue_block(i + 1, (i + 1) % 2)

    slot = i % 2

    def wait(r, carry):
        row_copy(slot, 0, r, 0).wait()
        row_copy(slot, 1, r, 0).wait()
        return carry

    lax.fori_loop(0, tt, wait, 0)
    g0 = gates_ref[:, 0:1]
    g1 = gates_ref[:, 1:2]
    moe = ybuf_ref[slot, 0] * g0 + ybuf_ref[slot, 1] * g1
    x = x_ref[...] + gt_ref[0] * moe
    o_ref[...] = _rms(x, g_ref[...])


def _combine(x, yb, y_row, gates, gt, g_final, *, rows_per_batch, tt):
    n, d = x.shape
    bpb = rows_per_batch // tt
    grid_spec = pltpu.PrefetchScalarGridSpec(
        num_scalar_prefetch=2,
        grid=(n // tt,),
        in_specs=[pl.BlockSpec((tt, d), lambda i, a, b: (i, 0)),
                  pl.BlockSpec((tt, 2), lambda i, a, b: (i, 0)),
                  pl.BlockSpec((1, 1, d), lambda i, a, b: (i // bpb, 0, 0)),
                  pl.BlockSpec((1, d), lambda i, a, b: (0, 0)),
                  pl.BlockSpec(memory_space=pl.ANY)],
        out_specs=pl.BlockSpec((tt, d), lambda i, a, b: (i, 0)),
        scratch_shapes=[pltpu.VMEM((2, 2, tt, d), F32), pltpu.SemaphoreType.DMA((2,))],
    )
    return pl.pallas_call(
        functools.partial(_combine_kernel, tt=tt),
        grid_spec=grid_spec,
        out_shape=jax.ShapeDtypeStruct((n, d), F32),
        compiler_params=_cparams("arbitrary"),
    )(y_row[:, 0], y_row[:, 1], x, gates, gt, g_final, yb)


GATHER_ROWS = 256


def _gather_kernel(idx_ref, nlive_ref, *refs, src_rows):
    n_src = len(src_rows)
    src_refs = refs[:n_src]
    o_ref, buf_ref, sems = refs[n_src:]
    i = pl.program_id(0)
    n_live = nlive_ref[0]

    def row_copy(src_ref, slot, r, src_row):
        return pltpu.make_async_copy(src_ref.at[pl.ds(src_row, 1)], buf_ref.at[slot, pl.ds(r, 1)],
                                     sems.at[slot])

    def issue_block(blk, slot):
        base = blk * GATHER_ROWS

        def issue(r, priority):
            row = idx_ref[base + r]
            lo = 0
            for k, src_ref in enumerate(src_refs):
                hi = lo + src_rows[k]
                in_range = row >= lo if k == n_src - 1 else jnp.logical_and(row >= lo, row < hi)
                if n_src == 1:
                    row_copy(src_ref, slot, r, row).start(priority=priority)
                else:
                    @pl.when(in_range)
                    def _():
                        row_copy(src_ref, slot, r, row - lo).start(priority=priority)
                lo = hi

        def start(r2, carry):
            issue(2 * r2, 0)
            issue(2 * r2 + 1, 1)
            return carry

        lax.fori_loop(0, GATHER_ROWS // 2, start, 0)

    @pl.when(jnp.logical_and(i == 0, n_live > 0))
    def _():
        issue_block(0, 0)

    @pl.when(i + 1 < n_live)
    def _():
        issue_block(i + 1, (i + 1) % 2)

    @pl.when(i < n_live)
    def _():
        slot = i % 2

        def wait(r, carry):
            row_copy(src_refs[0], slot, r, 0).wait()
            return carry

        lax.fori_loop(0, GATHER_ROWS, wait, 0)
        if len(o_ref.shape) == 2:
            o_ref[...] = buf_ref[slot].astype(o_ref.dtype)
        else:
            w = o_ref.shape[2]
            for k in range(o_ref.shape[0]):
                o_ref[k] = buf_ref[slot, :, k * w:(k + 1) * w].astype(o_ref.dtype)


def _gather_rows(srcs, idx, n_live_blocks, out_dtype, n_slabs=None):
    nr = idx.shape[0]
    src = srcs[0]
    d = src.shape[1]
    assert nr % GATHER_ROWS == 0
    blk = lambda i, idx, nl: jnp.minimum(i, nl[0] - 1)
    if n_slabs is None:
        out_spec = pl.BlockSpec((GATHER_ROWS, d), lambda i, idx, nl: (blk(i, idx, nl), 0))
        out_shape = (nr, d)
    else:
        out_spec = pl.BlockSpec((n_slabs, GATHER_ROWS, d // n_slabs),
                                lambda i, idx, nl: (0, blk(i, idx, nl), 0))
        out_shape = (n_slabs, nr, d // n_slabs)
    grid_spec = pltpu.PrefetchScalarGridSpec(
        num_scalar_prefetch=2,
        grid=(nr // GATHER_ROWS,),
        in_specs=[pl.BlockSpec(memory_space=pl.ANY) for _ in srcs],
        out_specs=out_spec,
        scratch_shapes=[pltpu.VMEM((2, GATHER_ROWS, d), src.dtype), pltpu.SemaphoreType.DMA((2,))],
    )
    return pl.pallas_call(
        functools.partial(_gather_kernel, src_rows=tuple(s.shape[0] for s in srcs)),
        grid_spec=grid_spec,
        out_shape=jax.ShapeDtypeStruct(out_shape, out_dtype),
        compiler_params=_cparams("arbitrary"),
    )(idx, n_live_blocks, *srcs)


def _lane_prefix_sum(x, lane):
    s = 1
    while s < LANES:
        x = x + jnp.where(lane >= s, pltpu.roll(x, s, axis=1), 0.0)
        s *= 2
    return x


def _logf_cumsum_kernel(*refs, p_len, t_pad):
    if p_len:
        fl_ref, bf_ref, past_ref, logf_ref, cum_ref = refs
    else:
        fl_ref, bf_ref, logf_ref, cum_ref = refs
        past_ref = None
    lane = lax.broadcasted_iota(jnp.int32, (B_HEADS, LANES), 1)
    carry = jnp.zeros((B_HEADS, 1), F32)
    for c in range(p_len // LANES):
        sl = slice(c * LANES, (c + 1) * LANES)
        y = _lane_prefix_sum(past_ref[0, :, sl], lane) + carry
        cum_ref[0, :, sl] = y
        carry = y[:, LANES - 1:LANES]
    for c in range(t_pad // LANES):
        sl = slice(c * LANES, (c + 1) * LANES)
        z = fl_ref[0, :, sl] + bf_ref[...]
        logf = jnp.minimum(z, 0.0) - jnp.log1p(jnp.exp(-jnp.abs(z)))
        logf_ref[0, :, sl] = logf
        y = _lane_prefix_sum(logf, lane) + carry
        cum_ref[0, :, p_len + c * LANES:p_len + (c + 1) * LANES] = y
        carry = y[:, LANES - 1:LANES]


def _logf_cumsum(fl_t, b_forget, past_t):
    b, _, t_pad = fl_t.shape
    p_len = 0 if past_t is None else past_t.shape[2]
    in_specs = [pl.BlockSpec((1, B_HEADS, t_pad), lambda i: (i, 0, 0)),
                pl.BlockSpec((B_HEADS, 1), lambda i: (0, 0))]
    args = [fl_t, b_forget]
    if p_len:
        in_specs.append(pl.BlockSpec((1, B_HEADS, p_len), lambda i: (i, 0, 0)))
        args.append(past_t)
    return pl.pallas_call(
        functools.partial(_logf_cumsum_kernel, p_len=p_len, t_pad=t_pad),
        grid=(b,),
        in_specs=in_specs,
        out_specs=[pl.BlockSpec((1, B_HEADS, t_pad), lambda i: (i, 0, 0)),
                   pl.BlockSpec((1, B_HEADS, p_len + t_pad), lambda i: (i, 0, 0))],
        out_shape=[jax.ShapeDtypeStruct((b, B_HEADS, t_pad), F32),
                   jax.ShapeDtypeStruct((b, B_HEADS, p_len + t_pad), F32)],
        compiler_params=_cparams("arbitrary"),
    )(*args)


BISECT_ROWS = 128


def _indexer_kernel(qi_ref, small_ref, klo_ref, khi_ref, mask_ref, qh_ref, wb_ref, keys_ref, *,
                    tq, tkc, n_chunks, l_real, p_len, nq_per_batch, n_sel):
    i = pl.program_id(0)
    row0 = (i % nq_per_batch) * tq
    n_pairs = IDX_HEADS // 2
    sub = tkc // LANES if tkc % LANES == 0 else 0
    for b in range(n_pairs):
        qh_ref[b * tq:(b + 1) * tq, :] = qi_ref[:, b * LANES:(b + 1) * LANES]
    w = small_ref[:, IDX_DIM:IDX_DIM + IDX_HEADS] * (IDX_HEADS ** -0.5)
    w = w * (IDX_DIM ** -0.5)
    wcols = LANES if sub else tkc
    for h in range(IDX_HEADS):
        wb_ref[h] = jnp.broadcast_to(w[:, h:h + 1], (tq, wcols))
    qpos = p_len + row0 + lax.broadcasted_iota(jnp.int32, (tq, 1), 0)
    k_limit = jnp.minimum((qpos // CHUNK + 1) * CHUNK, l_real)
    last_allowed = (p_len + row0 + tq - 1) // CHUNK * CHUNK + CHUNK - 1
    n_allowed = jnp.minimum(last_allowed // tkc + 1, n_chunks)

    def allowed_of(c0):
        kpos = c0 + lax.broadcasted_iota(jnp.int32, (tq, tkc), 1)
        return kpos < k_limit

    def weighted(s, h):
        r = jnp.maximum(s, 0.0)
        if not sub:
            return r * wb_ref[h]
        wv = wb_ref[h]
        return jnp.concatenate([r[:, u * LANES:(u + 1) * LANES] * wv for u in range(sub)], axis=1)

    for c in range(n_chunks):
        c0 = c * tkc

        @pl.when(c < n_allowed)
        def _():
            q_all = qh_ref[...]
            s_lo = jnp.dot(q_all, klo_ref[0, :, c0:c0 + tkc], preferred_element_type=F32)
            s_hi = jnp.dot(q_all, khi_ref[0, :, c0:c0 + tkc], preferred_element_type=F32)
            score = jnp.zeros((tq, tkc), F32)
            for b in range(n_pairs):
                score = score + weighted(s_lo[b * tq:(b + 1) * tq, :], 2 * b)
                score = score + weighted(s_hi[b * tq:(b + 1) * tq, :], 2 * b + 1)
            score = score + 0.0
            score = jnp.where(allowed_of(c0), score, NEG_INF)
            bits = lax.bitcast_convert_type(score, jnp.int32)
            keys_ref[c] = jnp.where(bits < 0, bits ^ 0x7FFFFFFF, bits)

    def count(pred_of_chunk, r0=0, nr=tq):
        def body(c, acc):
            ind = jnp.where(pred_of_chunk(keys_ref[c, r0:r0 + nr, :], c), 1.0, 0.0)
            if not sub:
                return acc + ind
            for u in range(sub):
                acc = acc + ind[:, u * LANES:(u + 1) * LANES]
            return acc

        acc = lax.fori_loop(0, n_allowed, body, jnp.zeros((nr, wcols), F32))
        return jnp.sum(acc, axis=1, keepdims=True)

    k_f = float(n_sel)
    slab = min(tq, BISECT_ROWS)
    t_parts = []
    for r0 in range(0, tq, slab):
        cnt = functools.partial(count, r0=r0, nr=slab)
        t0 = jnp.where(cnt(lambda kc, c: kc >= 0) >= k_f, 0, INT_MIN).astype(jnp.int32)

        def bisect(it, t, cnt=cnt):
            cand = t + jnp.left_shift(jnp.int32(1), 30 - it)
            return jnp.where(cnt(lambda kc, c: kc >= cand) >= k_f, cand, t)

        t_parts.append(lax.fori_loop(0, 31, bisect, t0))
    t = t_parts[0] if len(t_parts) == 1 else jnp.concatenate(t_parts, axis=0)
    cge = count(lambda kc, c: kc >= t)
    excess = jnp.max(jnp.where(t > KEY_NEG_INF, cge, 0.0)) > k_f

    def emit(sel_of_chunk):
        for c in range(n_chunks):
            @pl.when(c < n_allowed)
            def _():
                sel = jnp.logical_and(sel_of_chunk(keys_ref[c], c), allowed_of(c * tkc))
                mask_ref[c] = jnp.where(sel, 0.0, NEG_INF).astype(mask_ref.dtype)

            @pl.when(c >= n_allowed)
            def _():
                mask_ref[c] = jnp.full((tq, tkc), NEG_INF, mask_ref.dtype)

    @pl.when(jnp.logical_not(excess))
    def _():
        emit(lambda kc, c: kc >= t)

    @pl.when(excess)
    def _():
        need = k_f - count(lambda kc, c: kc > t)
        lane_idx = lax.broadcasted_iota(jnp.int32, (tq, tkc), 1)
        m = jnp.zeros((tq, 1), jnp.int32)
        for bit in reversed(range(max(1, (n_chunks * tkc - 1).bit_length()))):
            cand = m + (1 << bit)
            c_lt = count(lambda kc, c: jnp.logical_and(kc == t, c * tkc + lane_idx < cand))
            m = jnp.where(c_lt < need, cand, m)
        emit(lambda kc, c: jnp.logical_or(
            kc > t, jnp.logical_and(kc == t, c * tkc + lane_idx <= m)))


def _indexer_mask(qi, small, k_lo, k_hi, *, rows_per_batch, tq, tkc, l_real, p_len):
    n = qi.shape[0]
    l_pad = k_lo.shape[2]
    n_chunks = l_pad // tkc
    nqb = rows_per_batch // tq
    n_sel = min(TOPK_MAX, l_real // 4)
    wcols = LANES if tkc % LANES == 0 else tkc
    kern = functools.partial(_indexer_kernel, tq=tq, tkc=tkc, n_chunks=n_chunks, l_real=l_real,
                             p_len=p_len, nq_per_batch=nqb, n_sel=n_sel)
    kspec = pl.BlockSpec((1, LANES, l_pad), lambda i: (i // nqb, 0, 0))
    return pl.pallas_call(
        kern,
        grid=(n // tq,),
        in_specs=[pl.BlockSpec((tq, IDX_WIDTH), lambda i: (i, 0)),
                  pl.BlockSpec((tq, SMALL_WIDTH), lambda i: (i, 0)),
                  kspec, kspec],
        out_specs=pl.BlockSpec((n_chunks, tq, tkc), lambda i: (0, i, 0)),
        out_shape=jax.ShapeDtypeStruct((n_chunks, n, tkc), BF16),
        scratch_shapes=[pltpu.VMEM((IDX_HEADS // 2 * tq, LANES), BF16),
                        pltpu.VMEM((IDX_HEADS, tq, wcols), F32),
                        pltpu.VMEM((n_chunks, tq, tkc), jnp.int32)],
        compiler_params=_cparams("arbitrary"),
    )(qi, small, k_lo, k_hi)


def _softmax_step(s, carry, v_tile):
    m, l, acc = carry
    m_new = jnp.maximum(m, jnp.max(s, axis=1, keepdims=True))
    m_safe = jnp.where(m_new == NEG_INF, 0.0, m_new)
    alpha = jnp.exp2(m - m_safe)
    p = jnp.exp2(s - m_safe)
    l = alpha * l + jnp.sum(p, axis=1, keepdims=True)
    acc = alpha * acc + jnp.dot(p.astype(BF16), v_tile, preferred_element_type=F32)
    return m_new, l, acc


def _softmax_init(rows):
    return (jnp.full((rows, 1), NEG_INF, F32), jnp.zeros((rows, 1), F32),
            jnp.zeros((rows, HEAD_DIM), F32))


def _qk(q, k_tile):
    return lax.dot_general(q, k_tile, (((1,), (1,)), ((), ())), preferred_element_type=F32)


def _fox_prompt_kernel(q_ref, k_ref, v_ref, fk_ref, o_ref, *, t_tile, n_heads, nt):
    i = pl.program_id(1)
    lanes = lambda g: slice(g * HEAD_DIM, (g + 1) * HEAD_DIM)
    qs = [q_ref[:, lanes(g)] for g in range(n_heads)]

    def tile(j, carries, diagonal):
        k0 = pl.multiple_of(j * t_tile, t_tile)
        out = []
        for g in range(n_heads):
            s = _qk(qs[g], k_ref[pl.ds(k0, t_tile), lanes(g)]) - fk_ref[g * nt + j] * LOG2E
            if diagonal:
                row = lax.broadcasted_iota(jnp.int32, (t_tile, t_tile), 0)
                col = lax.broadcasted_iota(jnp.int32, (t_tile, t_tile), 1)
                s = jnp.where(col <= row, s, NEG_INF)
            out.append(_softmax_step(s, carries[g], v_ref[pl.ds(k0, t_tile), lanes(g)]))
        return tuple(out)

    init = tuple(_softmax_init(t_tile) for _ in range(n_heads))
    carries = lax.fori_loop(0, i, lambda j, c: tile(j, c, False), init)
    carries = tile(i, carries, True)
    for g in range(n_heads):
        _, l, acc = carries[g]
        o_ref[:, lanes(g)] = (acc / l).astype(o_ref.dtype)


def _fox_prompt(q, k, v, fk_tiles, *, t_tile, heads_per_step):
    t, width = q.shape
    heads = width // HEAD_DIM
    nt = t // t_tile
    gw = heads_per_step * HEAD_DIM
    resident = dict(pipeline_mode=pl.Buffered(1))
    return pl.pallas_call(
        functools.partial(_fox_prompt_kernel, t_tile=t_tile, n_heads=heads_per_step, nt=nt),
        grid=(heads // heads_per_step, nt),
        in_specs=[pl.BlockSpec((t_tile, gw), lambda h, i: (i, h)),
                  pl.BlockSpec((t, gw), lambda h, i: (0, h), **resident),
                  pl.BlockSpec((t, gw), lambda h, i: (0, h), **resident),
                  pl.BlockSpec((heads_per_step * nt, 1, t_tile), lambda h, i: (h, 0, 0))],
        out_specs=pl.BlockSpec((t_tile, gw), lambda h, i: (i, h)),
        out_shape=jax.ShapeDtypeStruct((t, width), BF16),
        compiler_params=_cparams("arbitrary", "arbitrary"),
    )(q, k, v, fk_tiles)


def _dsa_prompt_kernel(q_ref, k_ref, v_ref, mask_ref, bias_ref, o_ref, *, t_tile, grp, mask_w):
    i = pl.program_id(1)
    q = jnp.concatenate([q_ref[:, j * HEAD_DIM:(j + 1) * HEAD_DIM] for j in range(grp)], axis=0)
    n_sub = t_tile // mask_w

    def tile(j, carry, bias_cls):
        k0 = pl.multiple_of(j * t_tile, t_tile)
        s = _qk(q, k_ref[pl.ds(k0, t_tile), :]).reshape(grp, t_tile, t_tile)
        madd = jnp.concatenate([mask_ref[j * n_sub + u] for u in range(n_sub)], axis=1)
        s = s + madd.astype(F32)[None]
        if bias_cls is not None:
            s = s + bias_ref[bias_cls]
        return _softmax_step(s.reshape(grp * t_tile, t_tile), carry, v_ref[pl.ds(k0, t_tile), :])

    carry = lax.fori_loop(0, jnp.maximum(i - 1, 0), lambda j, c: tile(j, c, None),
                          _softmax_init(grp * t_tile))
    carry = lax.fori_loop(jnp.maximum(i - 1, 0), i, lambda j, c: tile(j, c, 1), carry)
    _, l, acc = tile(i, carry, 0)
    out = acc / l
    for j in range(grp):
        o_ref[:, j * HEAD_DIM:(j + 1) * HEAD_DIM] = out[j * t_tile:(j + 1) * t_tile].astype(o_ref.dtype)


def _dsa_prompt(q, k, v, mask, bias, *, t_tile):
    t, width = q.shape
    kv_heads = k.shape[1] // HEAD_DIM
    grp = width // HEAD_DIM // kv_heads
    nt = t // t_tile
    n_mask, _, mask_w = mask.shape
    resident = dict(pipeline_mode=pl.Buffered(1))
    return pl.pallas_call(
        functools.partial(_dsa_prompt_kernel, t_tile=t_tile, grp=grp, mask_w=mask_w),
        grid=(kv_heads, nt),
        in_specs=[pl.BlockSpec((t_tile, grp * HEAD_DIM), lambda g, i: (i, g)),
                  pl.BlockSpec((t, HEAD_DIM), lambda g, i: (0, g), **resident),
                  pl.BlockSpec((t, HEAD_DIM), lambda g, i: (0, g), **resident),
                  pl.BlockSpec((n_mask, t_tile, mask_w), lambda g, i: (0, i, 0)),
                  pl.BlockSpec((2, grp, t_tile, t_tile), lambda g, i: (0, g, 0, 0), **resident)],
        out_specs=pl.BlockSpec((t_tile, grp * HEAD_DIM), lambda g, i: (i, g)),
        out_shape=jax.ShapeDtypeStruct((t, width), BF16),
        compiler_params=_cparams("arbitrary", "arbitrary"),
    )(q, k, v, mask, bias)


def _attend(q_ref, k_of, v_of, bias_of_head, m_ref, l_ref, acc_ref, *, tq, kv_heads, grp):
    for g in range(kv_heads):
        kg = k_of(g).astype(BF16)
        vg = v_of(g).astype(BF16)
        qs = [q_ref[:, (g * grp + j) * HEAD_DIM:(g * grp + j + 1) * HEAD_DIM] for j in range(grp)]
        qg = qs[0] if grp == 1 else jnp.concatenate(qs, axis=0)
        s_all = _qk(qg, kg)
        for j in range(grp):
            h = g * grp + j
            s = s_all[j * tq:(j + 1) * tq, :] + bias_of_head(h)
            m_ref[h], l_ref[h], acc_ref[h] = _softmax_step(s, (m_ref[h], l_ref[h], acc_ref[h]), vg)


def _flash_kernel(*refs, mode, n_a, tq, tk_a, tk_b, p_len, kv_heads, grp):
    heads = kv_heads * grp
    it = iter(refs)
    q_ref = next(it)
    ka_ref, va_ref, kb_ref, vb_ref = next(it), next(it), next(it), next(it)
    if mode == "dsa":
        maska_ref, biasa_ref, maskb_ref, biasb_ref = next(it), next(it), next(it), next(it)
    else:
        fq_ref, fka_ref, fkb_ref = next(it), next(it), next(it)
    o_ref, m_ref, l_ref, acc_ref = next(it), next(it), next(it), next(it)
    j = pl.program_id(1)

    @pl.when(j == 0)
    def _():
        m_ref[...] = jnp.full(m_ref.shape, NEG_INF, F32)
        l_ref[...] = jnp.zeros(l_ref.shape, F32)
        acc_ref[...] = jnp.zeros(acc_ref.shape, F32)

    common = dict(tq=tq, kv_heads=kv_heads, grp=grp)

    @pl.when(j < n_a)
    def _():
        if mode == "dsa":
            madd = maska_ref[...].astype(F32)
            bias = lambda h: biasa_ref[0, h] + madd
        else:
            bias = lambda h: (fq_ref[:, h:h + 1] - fka_ref[0, h:h + 1, :]) * LOG2E
        k_of = lambda g: ka_ref[0, pl.ds(g, tk_a, stride=kv_heads), :]
        v_of = lambda g: va_ref[0, pl.ds(g, tk_a, stride=kv_heads), :]
        _attend(q_ref, k_of, v_of, bias, m_ref, l_ref, acc_ref, **common)

    @pl.when(j == n_a)
    def _():
        if mode == "dsa":
            madd = maskb_ref[...].astype(F32)
            bias = lambda h: biasb_ref[0, h] + madd
        else:
            row = lax.broadcasted_iota(jnp.int32, (tq, tk_b), 0)
            col = lax.broadcasted_iota(jnp.int32, (tq, tk_b), 1)
            causal = jnp.where(col <= row, 0.0, NEG_INF)
            bias = lambda h: (fq_ref[:, h:h + 1] - fkb_ref[0, h:h + 1, :tk_b]) * LOG2E + causal
        k_of = lambda g: kb_ref[:, g * HEAD_DIM:(g + 1) * HEAD_DIM]
        v_of = lambda g: vb_ref[:, g * HEAD_DIM:(g + 1) * HEAD_DIM]
        _attend(q_ref, k_of, v_of, bias, m_ref, l_ref, acc_ref, **common)
        for h in range(heads):
            o_ref[:, h * HEAD_DIM:(h + 1) * HEAD_DIM] = (acc_ref[h] / l_ref[h]).astype(o_ref.dtype)


def _flash_sample(mode, q, k_new, v_new, k_past, v_past, *, tk_a, mask_a=None, bias_a=None,
                  mask_b=None, bias_b=None, fq=None, fk_t=None):
    n, width = q.shape
    kv_w = k_new.shape[1]
    b = k_past.shape[0]
    tq = n // b
    heads = width // HEAD_DIM
    kv_heads = kv_w // HEAD_DIM
    p_len = k_past.shape[1] // kv_heads
    grp = heads // kv_heads
    n_a = p_len // tk_a
    a_blk = lambda j: jnp.minimum(j, n_a - 1)

    in_specs = [pl.BlockSpec((tq, width), lambda g, j: (g, 0))]
    spec = pl.BlockSpec((1, tk_a * kv_heads, HEAD_DIM), lambda g, j: (g, a_blk(j), 0))
    in_specs += [spec, spec]
    spec = pl.BlockSpec((tq, kv_w), lambda g, j: (g, 0))
    in_specs += [spec, spec]
    args = [q, k_past, v_past, k_new, v_new]
    if mode == "dsa":
        in_specs += [pl.BlockSpec((tq, tk_a), lambda g, j: (g, a_blk(j))),
                     pl.BlockSpec((1, heads, tq, tk_a), lambda g, j: (a_blk(j), 0, 0, 0)),
                     pl.BlockSpec((tq, tq), lambda g, j: (g, 0)),
                     pl.BlockSpec((1, heads, tq, tq), lambda g, j: (0, 0, 0, 0))]
        args += [mask_a, bias_a, mask_b, bias_b]
    else:
        in_specs += [pl.BlockSpec((tq, heads), lambda g, j: (g, 0)),
                     pl.BlockSpec((1, heads, tk_a), lambda g, j: (g, 0, a_blk(j))),
                     pl.BlockSpec((1, heads, LANES), lambda g, j: (g, 0, p_len // LANES))]
        args += [fq, fk_t, fk_t]

    kern = functools.partial(_flash_kernel, mode=mode, n_a=n_a, tq=tq, tk_a=tk_a, tk_b=tq,
                             p_len=p_len, kv_heads=kv_heads, grp=grp)
    return pl.pallas_call(
        kern,
        grid=(b, n_a + 1),
        in_specs=in_specs,
        out_specs=pl.BlockSpec((tq, width), lambda g, j: (g, 0)),
        out_shape=jax.ShapeDtypeStruct((n, width), BF16),
        scratch_shapes=[pltpu.VMEM((heads, tq, 1), F32),
                        pltpu.VMEM((heads, tq, 1), F32),
                        pltpu.VMEM((heads, tq, HEAD_DIM), F32)],
        compiler_params=_cparams("arbitrary", "arbitrary"),
    )(*args)


def _router_kernel(lt_ref, eidx_ref, gate_ref):
    row = lambda r: lt_ref[r:r + 1, :]
    g = [row(r) for r in range(N_GROUPS)]
    gmax = functools.reduce(jnp.maximum, g)
    gsel = jnp.full(gmax.shape, N_GROUPS - 1, jnp.int32)
    for r in reversed(range(N_GROUPS - 1)):
        gsel = jnp.where(g[r] == gmax, r, gsel)
    denom = functools.reduce(lambda a, b: a + b, [jnp.exp(x - gmax) for x in g])
    g_prob = 1.0 / denom
    e_in = []
    for kk in range(EXPERTS_PER_GROUP):
        v = row(N_GROUPS + (N_GROUPS - 1) * EXPERTS_PER_GROUP + kk)
        for r in reversed(range(N_GROUPS - 1)):
            v = jnp.where(gsel == r, row(N_GROUPS + r * EXPERTS_PER_GROUP + kk), v)
        e_in.append(v)
    v1 = functools.reduce(jnp.maximum, e_in)
    i1 = jnp.full(v1.shape, EXPERTS_PER_GROUP - 1, jnp.int32)
    for kk in reversed(range(EXPERTS_PER_GROUP - 1)):
        i1 = jnp.where(e_in[kk] == v1, kk, i1)
    rest = [jnp.where(i1 == kk, NEG_INF, e_in[kk]) for kk in range(EXPERTS_PER_GROUP)]
    v2 = functools.reduce(jnp.maximum, rest)
    i2 = jnp.full(v2.shape, EXPERTS_PER_GROUP - 1, jnp.int32)
    for kk in reversed(range(EXPERTS_PER_GROUP - 1)):
        i2 = jnp.where(jnp.logical_and(rest[kk] == v2, i1 != kk), kk, i2)
    e2 = jnp.exp(v2 - v1)
    inv = 1.0 / (1.0 + e2)
    eidx_ref[0:1, :] = gsel * EXPERTS_PER_GROUP + i1
    eidx_ref[1:2, :] = gsel * EXPERTS_PER_GROUP + i2
    gate_ref[0:1, :] = inv * g_prob
    gate_ref[1:2, :] = (e2 * inv) * g_prob


def _router(logits_t):
    n = logits_t.shape[1]
    full = lambda r: pl.BlockSpec((r, n), lambda: (0, 0))
    return pl.pallas_call(
        _router_kernel,
        in_specs=[full(logits_t.shape[0])],
        out_specs=[full(TOP_K_IN_GROUP), full(TOP_K_IN_GROUP)],
        out_shape=[jax.ShapeDtypeStruct((TOP_K_IN_GROUP, n), jnp.int32),
                   jax.ShapeDtypeStruct((TOP_K_IN_GROUP, n), F32)],
        compiler_params=pltpu.CompilerParams(vmem_limit_bytes=VMEM_LIMIT),
    )(logits_t)


EXPERT_CHUNK = 256
EXPERT_CHUNKS_PER_BLOCK = 4
EXPERT_ROWS = EXPERT_CHUNK * EXPERT_CHUNKS_PER_BLOCK
EXPERT_K_TILE = 512
EXPERT_N_TILE = 1024


def _experts_kernel(sb_e_ref, sb_row0_ref, sb_nch_ref, sb_blk_ref, x_hbm, w1_ref, w3_ref, w2_ref,
                    o_ref, x_ref, a_ref, b_ref, act_ref, sem, *, nk):
    del sb_e_ref, sb_blk_ref
    s = pl.program_id(0)
    t = pl.program_id(1)
    n_ch = sb_nch_ref[s]

    @pl.when(jnp.logical_and(t == 0, n_ch > 0))
    def _():
        row0 = pl.multiple_of(sb_row0_ref[s], EXPERT_CHUNK)
        copies = [pltpu.make_async_copy(x_hbm.at[k, pl.ds(row0, EXPERT_ROWS)], x_ref.at[k], sem)
                  for k in range(nk)]
        for cp in copies:
            cp.start()
        for cp in copies:
            cp.wait()

    for nc in range(1, EXPERT_CHUNKS_PER_BLOCK + 1):
        rows = nc * EXPERT_CHUNK

        @pl.when(jnp.logical_and(t < nk, n_ch == nc))
        def _():
            xc = x_ref[t, 0:rows, :]
            a = jnp.dot(xc, w1_ref[0].astype(BF16), preferred_element_type=F32)
            b = jnp.dot(xc, w3_ref[0].astype(BF16), preferred_element_type=F32)

            @pl.when(t == 0)
            def _():
                a_ref[0:rows, :] = a
                b_ref[0:rows, :] = b

            @pl.when(t > 0)
            def _():
                a_ref[0:rows, :] += a
                b_ref[0:rows, :] += b

            @pl.when(t == nk - 1)
            def _():
                g = a_ref[0:rows, :]
                act_ref[0:rows, :] = (g * jax.nn.sigmoid(g) * b_ref[0:rows, :]).astype(BF16)

        @pl.when(jnp.logical_and(t >= nk, n_ch == nc))
        def _():
            o_ref[0:rows, :] = jnp.dot(act_ref[0:rows, :], w2_ref[0].astype(BF16),
                                       preferred_element_type=F32)


def _experts(xs, sb_expert, sb_row0, sb_nch, sb_blk, w1, w3, w2, n_sb):
    nk = xs.shape[0]
    d = nk * EXPERT_K_TILE
    d_e = w1.shape[2]
    nn = d // EXPERT_N_TILE
    r = EXPERT_ROWS
    grid_spec = pltpu.PrefetchScalarGridSpec(
        num_scalar_prefetch=4,
        grid=(n_sb, nk + nn),
        in_specs=[
            pl.BlockSpec(memory_space=pl.ANY),
            pl.BlockSpec((1, EXPERT_K_TILE, d_e),
                         lambda s, t, e, r0, nch, blk: (e[s], jnp.minimum(t, nk - 1), 0)),
            pl.BlockSpec((1, EXPERT_K_TILE, d_e),
                         lambda s, t, e, r0, nch, blk: (e[s], jnp.minimum(t, nk - 1), 0)),
            pl.BlockSpec((1, d_e, EXPERT_N_TILE),
                         lambda s, t, e, r0, nch, blk: (e[s], 0, jnp.maximum(t - nk, 0))),
        ],
        out_specs=pl.BlockSpec(
            (r, EXPERT_N_TILE),
            lambda s, t, e, r0, nch, blk: (blk[s], jnp.where(nch[s] > 0, jnp.maximum(t - nk, 0), 0))),
        scratch_shapes=[pltpu.VMEM((nk, r, EXPERT_K_TILE), BF16),
                        pltpu.VMEM((r, d_e), F32),
                        pltpu.VMEM((r, d_e), F32),
                        pltpu.VMEM((r, d_e), BF16),
                        pltpu.SemaphoreType.DMA(())],
    )
    return pl.pallas_call(
        functools.partial(_experts_kernel, nk=nk),
        grid_spec=grid_spec,
        out_shape=jax.ShapeDtypeStruct(((n_sb + 1) * r, d), F32),
        compiler_params=_cparams("arbitrary", "arbitrary"),
    )(sb_expert, sb_row0, sb_nch, sb_blk, xs, w1, w3, w2)


def _dispatch(expert, n_sb, n_x_rows):
    n = expert.shape[0]
    a = n * TOP_K_IN_GROUP
    ch, r = EXPERT_CHUNK, EXPERT_ROWS
    flat_e = expert.reshape(a)
    flat_tok = jnp.arange(a, dtype=jnp.int32) // TOP_K_IN_GROUP
    order = jnp.argsort(flat_e)
    se = flat_e[order]
    counts = jnp.bincount(flat_e, length=N_EXPERTS).astype(jnp.int32)
    seg_start = jnp.cumsum(counts) - counts
    rank = jnp.arange(a, dtype=jnp.int32) - seg_start[se]
    nch_e = (counts + ch - 1) // ch
    xch_end = jnp.cumsum(nch_e)
    x_start = (xch_end - nch_e) * ch
    nsb_e = (counts + r - 1) // r
    sb_end = jnp.cumsum(nsb_e)
    sb_start = sb_end - nsb_e
    x_row_tok = jnp.zeros((n_x_rows,), jnp.int32).at[x_start[se] + rank].set(flat_tok[order])
    n_live_x_blocks = (xch_end[-1] * ch + GATHER_ROWS - 1) // GATHER_ROWS
    y_row = jnp.zeros((a,), jnp.int32).at[order].set(sb_start[se] * r + rank)
    y_row = y_row.reshape(n, TOP_K_IN_GROUP)
    total = sb_end[-1]
    s_idx = jnp.arange(n_sb, dtype=jnp.int32)
    e_of_sb = jnp.minimum(jnp.searchsorted(sb_end, s_idx, side="right"), N_EXPERTS - 1).astype(jnp.int32)
    k_in_e = s_idx - sb_start[e_of_sb]
    rows_of_sb = jnp.clip(counts[e_of_sb] - k_in_e * r, 0, r)
    live = s_idx < total
    last_e = e_of_sb[jnp.maximum(total - 1, 0)]
    sb_expert = jnp.where(live, e_of_sb, last_e).astype(jnp.int32)
    sb_nch = jnp.where(live, (rows_of_sb + ch - 1) // ch, 0).astype(jnp.int32)
    sb_row0 = jnp.where(live, x_start[e_of_sb] + k_in_e * r, 0).astype(jnp.int32)
    sb_blk = jnp.where(live, s_idx, n_sb).astype(jnp.int32)
    return (x_row_tok, n_live_x_blocks.astype(jnp.int32).reshape(1), y_row, sb_expert, sb_row0,
            sb_nch, sb_blk)


def _t5_bucket(rel):
    half = N_BUCKETS // 2
    max_exact = half // 2
    bucket = jnp.where(rel > 0, half, 0).astype(jnp.int32)
    n = jnp.abs(rel)
    large = max_exact + (jnp.log(jnp.maximum(n, 1).astype(jnp.float32) / max_exact)
                         / math.log(MAX_DISTANCE / max_exact) * (half - max_exact)).astype(jnp.int32)
    large = jnp.minimum(large, half - 1)
    return bucket + jnp.where(n < max_exact, n, large).astype(jnp.int32)


def _bias_tiles(rel_bias, rel0_list, tq, tk):
    span = tq + tk
    tiles = []
    for rel0 in rel0_list:
        u = jnp.arange(span, dtype=jnp.int32)
        rel = rel0 + jnp.where(u < tk, u, u - span)
        table = jnp.transpose(rel_bias[_t5_bucket(rel)].astype(F32))
        heads = table.shape[0]
        skew = jnp.tile(table, (1, tq))[:, :tq * (span - 1)].reshape(heads, tq, span - 1)
        tiles.append(skew[:, :, :tk])
    return jnp.stack(tiles, axis=0)


def _pick(n, candidates):
    for c in candidates:
        if n % c == 0:
            return c
    return n


INDEXER_TILE = 256
ATTN_TILE = 512
FOX_TILE = 1024
FOX_HEADS_PER_STEP = 2


def _mixer_half(x, mods, past, w_in_packed, b_forget, w_out, rel_bias, g_mix, g_ffn):
    b, t, d = x.shape
    n = b * t
    sh1, sc1, gt1, sh2, sc2, _ = mods
    xf = x.reshape(n, d)
    tt = _pick(t, (256, 128, 64, 32, 16))
    h = _norm_mod(xf, g_mix, sc1, sh1, rows_per_batch=t, tt=tt, out_dtype=BF16)

    tm = _pick(n, (1024, 512, 256, 128))

    def proj(col_off, width, tn, **kw):
        return _matmul([h], w_in_packed, col_off=col_off, n=width, tm=tm, tn=tn, **kw)

    off = 0
    qa = proj(off, A_WIDTH, 1024, out_dtypes=(BF16,), out_scale=Q_SCALE); off += A_WIDTH
    ka, ka16 = proj(off, A_KV_WIDTH, 512, out_dtypes=(F32, BF16)); off += A_KV_WIDTH
    va, va16 = proj(off, A_KV_WIDTH, 512, out_dtypes=(F32, BF16)); off += A_KV_WIDTH
    qi = proj(off, IDX_WIDTH, 1024, out_dtypes=(BF16,)); off += IDX_WIDTH
    qb = proj(off, B_WIDTH, 1024, out_dtypes=(BF16,), out_scale=Q_SCALE); off += B_WIDTH
    kb, kb16 = proj(off, B_WIDTH, 1024, out_dtypes=(F32, BF16)); off += B_WIDTH
    vb, vb16 = proj(off, B_WIDTH, 1024, out_dtypes=(F32, BF16)); off += B_WIDTH
    small = proj(off, SMALL_WIDTH, SMALL_WIDTH)

    ki = small[:, :IDX_DIM]
    fl = small[:, IDX_DIM + IDX_HEADS:IDX_DIM + IDX_HEADS + B_HEADS]

    p_len = 0 if past is None else past[0].shape[1]
    t_pad = -(-t // LANES) * LANES
    l_real = p_len + t
    l_pad = p_len + t_pad

    fl_t = jnp.transpose(fl.reshape(b, t, B_HEADS), (0, 2, 1))
    fl_t = jnp.pad(fl_t, ((0, 0), (0, 0), (0, t_pad - t)))
    past_logf_t = None if past is None else jnp.transpose(past[5], (0, 2, 1))
    logf_t, fcum_t = _logf_cumsum(fl_t, b_forget.reshape(B_HEADS, 1), past_logf_t)
    logf = jnp.transpose(logf_t[:, :, :t], (0, 2, 1))

    ki_b = ki.reshape(b, t, IDX_DIM)
    if past is not None:
        ki_b = jnp.concatenate([past[2], ki_b], axis=1)
    ki_t = jnp.transpose(jnp.pad(ki_b, ((0, 0), (0, l_pad - l_real), (0, 0))), (0, 2, 1)).astype(BF16)
    zeros = jnp.zeros_like(ki_t)
    k_lo = jnp.concatenate([ki_t, zeros], axis=1)
    k_hi = jnp.concatenate([zeros, ki_t], axis=1)

    if past is None:
        tile = ATTN_TILE
        assert b == 1 and t % tile == 0 and tile >= MAX_DISTANCE and tile % INDEXER_TILE == 0
        mask = _indexer_mask(qi, small, k_lo, k_hi, rows_per_batch=t, tq=INDEXER_TILE,
                             tkc=INDEXER_TILE, l_real=l_real, p_len=0)
        bias = _bias_tiles(rel_bias, [0, -tile], tile, tile)
        far = rel_bias[_t5_bucket(jnp.int32(-MAX_DISTANCE))].astype(F32)
        bias = (bias - far[None, :, None, None]) * LOG2E
        out_a = _dsa_prompt(qa, ka16, va16, mask, bias, t_tile=tile)
        assert t % FOX_TILE == 0
        fk_tiles = fcum_t.reshape(B_HEADS * (t // FOX_TILE), 1, FOX_TILE)
        out_b = _fox_prompt(qb, kb16, vb16, fk_tiles, t_tile=FOX_TILE,
                            heads_per_step=FOX_HEADS_PER_STEP)
    else:
        tk_a = _pick(p_len, (1024, 512, 256, 128))
        mask = _indexer_mask(qi, small, k_lo, k_hi, rows_per_batch=t, tq=t, tkc=l_pad,
                             l_real=l_real, p_len=p_len).reshape(n, l_pad)
        n_a = p_len // tk_a
        bias_a = _bias_tiles(rel_bias, [c * tk_a - p_len for c in range(n_a)], t, tk_a) * LOG2E
        bias_b = _bias_tiles(rel_bias, [0], t, t) * LOG2E
        cache = lambda a: a.reshape(b, -1, HEAD_DIM)
        out_a = _flash_sample("dsa", qa, ka16, va16, cache(past[0]), cache(past[1]), tk_a=tk_a,
                              mask_a=mask[:, :p_len], bias_a=bias_a,
                              mask_b=mask[:, p_len:p_len + t], bias_b=bias_b)
        fq = jnp.transpose(fcum_t[:, :, p_len:p_len + t], (0, 2, 1)).reshape(n, B_HEADS)
        out_b = _flash_sample("fox", qb, kb16, vb16, cache(past[3]), cache(past[4]), tk_a=tk_a,
                              fq=fq, fk_t=fcum_t)

    gate1 = gt1.reshape(1, d) if b == 1 else jnp.repeat(gt1.reshape(b, d), t, axis=0)
    x1 = _matmul([out_a, out_b], w_out, tm=tm, tn=512, res=xf, gate=gate1)
    h2 = _norm_mod(x1, g_ffn, sc2, sh2, rows_per_batch=t, tt=tt, out_dtype=F32)

    state = (ka.reshape(1, b, t, A_KV_HEADS, HEAD_DIM), va.reshape(1, b, t, A_KV_HEADS, HEAD_DIM),
             ki.reshape(1, b, t, IDX_DIM),
             kb.reshape(1, b, t, B_HEADS, HEAD_DIM), vb.reshape(1, b, t, B_HEADS, HEAD_DIM),
             logf.reshape(1, b, t, B_HEADS))
    return x1, h2, state


def kernel(x_prompt, x_sample, cache_a_k, cache_a_v, cache_idx_k, cache_b_k, cache_b_v, cache_b_logf, c_prompt, c_sample, w_in, b_forget, w_out, rel_bias, w_mod, b_mod, g_mix, g_ffn, w_router_group, b_router_group, w_router_expert, b_router_expert, w_gate, w_up, w_down, g_final):
    assert w_in.shape[0] == 1, "single-layer trunk"
    d = x_prompt.shape[-1]
    bp, tp, _ = x_prompt.shape
    bs, ts, _ = x_sample.shape
    n_p, n_s = bp * tp, bs * ts
    layer0 = lambda a: a.reshape(a.shape[1:])

    n_c = bp + bs
    c_rows = -(-n_c // 8) * 8
    c_all = jnp.pad(jnp.concatenate([c_prompt, c_sample], axis=0), ((0, c_rows - n_c), (0, 0)))
    mod = _matmul([c_all], layer0(w_mod), tm=c_rows, tn=512, silu_a=True, bias=b_mod.reshape(1, -1))

    def mods_of(lo, hi):
        m6 = mod[lo:hi].reshape(hi - lo, 6, 1, d)
        return [m6[:, i] for i in range(6)]

    mods_p, mods_s = mods_of(0, bp), mods_of(bp, n_c)

    w = layer0(w_in)
    o1 = A_WIDTH + 2 * A_KV_WIDTH + IDX_WIDTH
    o2 = o1 + IDX_DIM + IDX_HEADS
    o3 = o2 + 3 * B_WIDTH
    tail_pad = SMALL_WIDTH - (IDX_DIM + IDX_HEADS + B_HEADS)
    w_packed = jnp.concatenate(
        [w[:, :o1].astype(BF16), w[:, o2:o3].astype(BF16), w[:, o1:o2].astype(BF16),
         w[:, o3:].astype(BF16), jnp.zeros((d, tail_pad), BF16)], axis=1)

    g_mix2, g_ffn2, g_fin2 = g_mix.reshape(1, d), g_ffn.reshape(1, d), g_final.reshape(1, d)
    past = tuple(layer0(c) for c in (cache_a_k, cache_a_v, cache_idx_k, cache_b_k, cache_b_v,
                                      cache_b_logf))
    b_f, w_o = layer0(b_forget), layer0(w_out)
    x1_p, h2_p, st_p = _mixer_half(x_prompt, mods_p, None, w_packed, b_f, w_o, rel_bias, g_mix2, g_ffn2)
    x1_s, h2_s, st_s = _mixer_half(x_sample, mods_s, past, w_packed, b_f, w_o, rel_bias, g_mix2, g_ffn2)

    n_tok = n_p + n_s
    w_r = jnp.concatenate([layer0(w_router_group), layer0(w_router_expert)], axis=1)
    n_cls = w_r.shape[1]
    w_r = jnp.pad(w_r, ((0, 0), (0, LANES - n_cls)))
    b_r = jnp.pad(jnp.concatenate([layer0(b_router_group), layer0(b_router_expert)]), (0, LANES - n_cls))
    logits = jnp.concatenate(
        [_matmul([hh], w_r, tm=_pick(hh.shape[0], (512, 256, 128)), tn=LANES,
                 bias=b_r.reshape(1, LANES)) for hh in (h2_p, h2_s)], axis=0)
    eidx_t, gate_t = _router(jnp.transpose(logits))
    expert = jnp.transpose(eidx_t)
    gates = jnp.transpose(gate_t)

    n_assign = n_tok * TOP_K_IN_GROUP
    assert n_assign % GATHER_ROWS == 0
    n_sb = n_assign // EXPERT_ROWS + N_EXPERTS
    n_x_rows = -(-n_assign // EXPERT_CHUNK) * EXPERT_CHUNK + N_EXPERTS * EXPERT_CHUNK + EXPERT_ROWS
    x_row_tok, n_live_x, y_row, sb_expert, sb_row0, sb_nch, sb_blk = _dispatch(expert, n_sb, n_x_rows)
    assert d % EXPERT_K_TILE == 0 and d % EXPERT_N_TILE == 0
    xs = _gather_rows([h2_p, h2_s], x_row_tok, n_live_x, BF16, n_slabs=d // EXPERT_K_TILE)
    yb = _experts(xs, sb_expert, sb_row0, sb_nch, sb_blk, layer0(w_gate), layer0(w_up),
                  layer0(w_down), n_sb)
    def finish(x1, lo, gt2, t):
        tt = _pick(t, (128, 64, 32, 16))
        hi = lo + x1.shape[0]
        return _combine(x1, yb, y_row[lo:hi], gates[lo:hi], gt2, g_fin2, rows_per_batch=t, tt=tt)

    y_p = finish(x1_p, 0, mods_p[5], tp).reshape(bp, tp, d)
    y_s = finish(x1_s, n_p, mods_s[5], ts).reshape(bs, ts, d)
    return (y_p, y_s) + st_p + st_s
```

```python
import functools
import math

import numpy as np
import jax
import jax.numpy as jnp
from jax import lax
from jax.experimental import pallas as pl
from jax.experimental.pallas import tpu as pltpu

CHUNK = 64
HEAD_DIM = 128
A_HEADS = 16
A_KV_HEADS = 4
IDX_HEADS = 32
IDX_DIM = 64
TOPK_MAX = 256
B_HEADS = 16
N_BUCKETS = 32
MAX_DISTANCE = 128
N_GROUPS = 4
EXPERTS_PER_GROUP = 8
N_EXPERTS = N_GROUPS * EXPERTS_PER_GROUP
TOP_K_IN_GROUP = 2
EPS = 1e-6

A_WIDTH = A_HEADS * HEAD_DIM
A_KV_WIDTH = A_KV_HEADS * HEAD_DIM
B_WIDTH = B_HEADS * HEAD_DIM
IDX_WIDTH = IDX_HEADS * IDX_DIM
SMALL_WIDTH = 128

LANES = 128
SUBLANES = 8
V7X_VMEM_BYTES = 64 * 1024 * 1024
VMEM_LIMIT = 56 * 1024 * 1024

BF16 = jnp.bfloat16
F32 = jnp.float32
NEG_INF = float("-inf")
INT_MIN = -(2 ** 31)
KEY_NEG_INF = int(np.int32(np.float32(-np.inf).view(np.int32)) ^ np.int32(0x7FFFFFFF))
LOG2E = math.log2(math.e)
Q_SCALE = HEAD_DIM ** -0.5 * LOG2E


def _cparams(*sem):
    return pltpu.CompilerParams(dimension_semantics=sem, vmem_limit_bytes=VMEM_LIMIT)


def _matmul_kernel(*refs, n_a, silu_a, epilogue, out_scale, n_out):
    a_refs = refs[:n_a]
    w_ref = refs[n_a]
    rest = refs[n_a + 1:]
    o_refs = rest[len(rest) - n_out:]
    acc = None
    k0 = 0
    for a_ref in a_refs:
        a = a_ref[...]
        if silu_a:
            a = a * jax.nn.sigmoid(a)
        kk = a.shape[1]
        part = jnp.dot(a.astype(BF16), w_ref[k0:k0 + kk, :].astype(BF16),
                       preferred_element_type=F32)
        acc = part if acc is None else acc + part
        k0 += kk
    if epilogue == "bias":
        acc = acc + rest[0][...]
    elif epilogue == "gated_residual":
        acc = rest[0][...] + rest[1][...] * acc
    if out_scale is not None:
        acc = acc * out_scale
    for o_ref in o_refs:
        o_ref[...] = acc.astype(o_ref.dtype)


def _matmul(a_list, w, *, col_off=0, n=None, tm, tn, silu_a=False, bias=None, res=None,
            gate=None, out_dtypes=(F32,), out_scale=None):
    m = a_list[0].shape[0]
    k = w.shape[0]
    n = w.shape[1] - col_off if n is None else n
    assert m % tm == 0 and n % tn == 0 and col_off % tn == 0
    assert sum(a.shape[1] for a in a_list) == k
    cb = col_off // tn
    in_specs = [pl.BlockSpec((tm, a.shape[1]), lambda j, i: (i, 0)) for a in a_list]
    in_specs.append(pl.BlockSpec((k, tn), lambda j, i: (0, j + cb)))
    args = list(a_list) + [w]
    if bias is not None:
        epilogue = "bias"
        in_specs.append(pl.BlockSpec((1, tn), lambda j, i: (0, j)))
        args.append(bias)
    elif res is not None:
        epilogue = "gated_residual"
        in_specs.append(pl.BlockSpec((tm, tn), lambda j, i: (i, j)))
        args.append(res)
        if gate.shape[0] == 1:
            in_specs.append(pl.BlockSpec((1, tn), lambda j, i: (0, j)))
        else:
            in_specs.append(pl.BlockSpec((tm, tn), lambda j, i: (i, j)))
        args.append(gate)
    else:
        epilogue = None
    outs = pl.pallas_call(
        functools.partial(_matmul_kernel, n_a=len(a_list), silu_a=silu_a, epilogue=epilogue,
                          out_scale=out_scale, n_out=len(out_dtypes)),
        grid=(n // tn, m // tm),
        in_specs=in_specs,
        out_specs=[pl.BlockSpec((tm, tn), lambda j, i: (i, j)) for _ in out_dtypes],
        out_shape=[jax.ShapeDtypeStruct((m, n), dt) for dt in out_dtypes],
        compiler_params=_cparams("arbitrary", "arbitrary"),
    )(*args)
    return outs[0] if len(out_dtypes) == 1 else outs


def _rms(x, g):
    return x * lax.rsqrt(jnp.mean(x * x, axis=-1, keepdims=True) + EPS) * g


def _norm_mod_kernel(x_ref, g_ref, sc_ref, sh_ref, o_ref):
    y = _rms(x_ref[...], g_ref[...])
    o_ref[...] = (y * (1.0 + sc_ref[0]) + sh_ref[0]).astype(o_ref.dtype)


def _norm_mod(x, g, sc, sh, *, rows_per_batch, tt, out_dtype):
    n, d = x.shape
    bpb = rows_per_batch // tt
    return pl.pallas_call(
        _norm_mod_kernel,
        grid=(n // tt,),
        in_specs=[pl.BlockSpec((tt, d), lambda i: (i, 0)),
                  pl.BlockSpec((1, d), lambda i: (0, 0)),
                  pl.BlockSpec((1, 1, d), lambda i: (i // bpb, 0, 0)),
                  pl.BlockSpec((1, 1, d), lambda i: (i // bpb, 0, 0))],
        out_specs=pl.BlockSpec((tt, d), lambda i: (i, 0)),
        out_shape=jax.ShapeDtypeStruct((n, d), out_dtype),
        compiler_params=_cparams("arbitrary"),
    )(x, g, sc, sh)


def _combine_kernel(idx0_ref, idx1_ref, x_ref, gates_ref, gt_ref, g_ref, yb_hbm, o_ref, ybuf_ref,
                    sems, *, tt):
    i = pl.program_id(0)
    n_blocks = pl.num_programs(0)

    def row_copy(slot, which, r, src_row):
        return pltpu.make_async_copy(yb_hbm.at[pl.ds(src_row, 1)],
                                     ybuf_ref.at[slot, which, pl.ds(r, 1)], sems.at[slot])

    def issue_block(blk, slot):
        base = blk * tt

        def start(r, carry):
            row_copy(slot, 0, r, idx0_ref[base + r]).start(priority=0)
            row_copy(slot, 1, r, idx1_ref[base + r]).start(priority=1)
            return carry

        lax.fori_loop(0, tt, start, 0)

    @pl.when(i == 0)
    def _():
        issue_block(0, 0)

    @pl.when(i + 1 < n_blocks)
    def _():
        issue_block(i + 1, (i + 1) % 2)

    slot = i % 2

    def wait(r, carry):
        row_copy(slot, 0, r, 0).wait()
        row_copy(slot, 1, r, 0).wait()
        return carry

    lax.fori_loop(0, tt, wait, 0)
    g0 = gates_ref[:, 0:1]
    g1 = gates_ref[:, 1:2]
    moe = ybuf_ref[slot, 0] * g0 + ybuf_ref[slot, 1] * g1
    x = x_ref[...] + gt_ref[0] * moe
    o_ref[...] = _rms(x, g_ref[...])


def _combine(x, yb, y_row, gates, gt, g_final, *, rows_per_batch, tt):
    n, d = x.shape
    bpb = rows_per_batch // tt
    grid_spec = pltpu.PrefetchScalarGridSpec(
        num_scalar_prefetch=2,
        grid=(n // tt,),
        in_specs=[pl.BlockSpec((tt, d), lambda i, a, b: (i, 0)),
                  pl.BlockSpec((tt, 2), lambda i, a, b: (i, 0)),
                  pl.BlockSpec((1, 1, d), lambda i, a, b: (i // bpb, 0, 0)),
                  pl.BlockSpec((1, d), lambda i, a, b: (0, 0)),
                  pl.BlockSpec(memory_space=pl.ANY)],
        out_specs=pl.BlockSpec((tt, d), lambda i, a, b: (i, 0)),
        scratch_shapes=[pltpu.VMEM((2, 2, tt, d), F32), pltpu.SemaphoreType.DMA((2,))],
    )
    return pl.pallas_call(
        functools.partial(_combine_kernel, tt=tt),
        grid_spec=grid_spec,
        out_shape=jax.ShapeDtypeStruct((n, d), F32),
        compiler_params=_cparams("arbitrary"),
    )(y_row[:, 0], y_row[:, 1], x, gates, gt, g_final, yb)


GATHER_ROWS = 256


def _gather_kernel(idx_ref, nlive_ref, *refs, src_rows):
    n_src = len(src_rows)
    src_refs = refs[:n_src]
    o_ref, buf_ref, sems = refs[n_src:]
    i = pl.program_id(0)
    n_live = nlive_ref[0]

    def row_copy(src_ref, slot, r, src_row):
        return pltpu.make_async_copy(src_ref.at[pl.ds(src_row, 1)], buf_ref.at[slot, pl.ds(r, 1)],
                                     sems.at[slot])

    def issue_block(blk, slot):
        base = blk * GATHER_ROWS

        def issue(r, priority):
            row = idx_ref[base + r]
            lo = 0
            for k, src_ref in enumerate(src_refs):
                hi = lo + src_rows[k]
                in_range = row >= lo if k == n_src - 1 else jnp.logical_and(row >= lo, row < hi)
                if n_src == 1:
                    row_copy(src_ref, slot, r, row).start(priority=priority)
                else:
                    @pl.when(in_range)
                    def _():
                        row_copy(src_ref, slot, r, row - lo).start(priority=priority)
                lo = hi

        def start(r2, carry):
            issue(2 * r2, 0)
            issue(2 * r2 + 1, 1)
            return carry

        lax.fori_loop(0, GATHER_ROWS // 2, start, 0)

    @pl.when(jnp.logical_and(i == 0, n_live > 0))
    def _():
        issue_block(0, 0)

    @pl.when(i + 1 < n_live)
    def _():
        issue_block(i + 1, (i + 1) % 2)

    @pl.when(i < n_live)
    def _():
        slot = i % 2

        def wait(r, carry):
            row_copy(src_refs[0], slot, r, 0).wait()
            return carry

        lax.fori_loop(0, GATHER_ROWS, wait, 0)
        if len(o_ref.shape) == 2:
            o_ref[...] = buf_ref[slot].astype(o_ref.dtype)
        else:
            w = o_ref.shape[2]
            for k in range(o_ref.shape[0]):
                o_ref[k] = buf_ref[slot, :, k * w:(k + 1) * w].astype(o_ref.dtype)


def _gather_rows(srcs, idx, n_live_blocks, out_dtype, n_slabs=None):
    nr = idx.shape[0]
    src = srcs[0]
    d = src.shape[1]
    assert nr % GATHER_ROWS == 0
    blk = lambda i, idx, nl: jnp.minimum(i, nl[0] - 1)
    if n_slabs is None:
        out_spec = pl.BlockSpec((GATHER_ROWS, d), lambda i, idx, nl: (blk(i, idx, nl), 0))
        out_shape = (nr, d)
    else:
        out_spec = pl.BlockSpec((n_slabs, GATHER_ROWS, d // n_slabs),
                                lambda i, idx, nl: (0, blk(i, idx, nl), 0))
        out_shape = (n_slabs, nr, d // n_slabs)
    grid_spec = pltpu.PrefetchScalarGridSpec(
        num_scalar_prefetch=2,
        grid=(nr // GATHER_ROWS,),
        in_specs=[pl.BlockSpec(memory_space=pl.ANY) for _ in srcs],
        out_specs=out_spec,
        scratch_shapes=[pltpu.VMEM((2, GATHER_ROWS, d), src.dtype), pltpu.SemaphoreType.DMA((2,))],
    )
    return pl.pallas_call(
        functools.partial(_gather_kernel, src_rows=tuple(s.shape[0] for s in srcs)),
        grid_spec=grid_spec,
        out_shape=jax.ShapeDtypeStruct(out_shape, out_dtype),
        compiler_params=_cparams("arbitrary"),
    )(idx, n_live_blocks, *srcs)


def _lane_prefix_sum(x, lane):
    s = 1
    while s < LANES:
        x = x + jnp.where(lane >= s, pltpu.roll(x, s, axis=1), 0.0)
        s *= 2
    return x


def _logf_cumsum_kernel(*refs, p_len, t_pad):
    if p_len:
        fl_ref, bf_ref, past_ref, logf_ref, cum_ref = refs
    else:
        fl_ref, bf_ref, logf_ref, cum_ref = refs
        past_ref = None
    lane = lax.broadcasted_iota(jnp.int32, (B_HEADS, LANES), 1)
    carry = jnp.zeros((B_HEADS, 1), F32)
    for c in range(p_len // LANES):
        sl = slice(c * LANES, (c + 1) * LANES)
        y = _lane_prefix_sum(past_ref[0, :, sl], lane) + carry
        cum_ref[0, :, sl] = y
        carry = y[:, LANES - 1:LANES]
    for c in range(t_pad // LANES):
        sl = slice(c * LANES, (c + 1) * LANES)
        z = fl_ref[0, :, sl] + bf_ref[...]
        logf = jnp.minimum(z, 0.0) - jnp.log1p(jnp.exp(-jnp.abs(z)))
        logf_ref[0, :, sl] = logf
        y = _lane_prefix_sum(logf, lane) + carry
        cum_ref[0, :, p_len + c * LANES:p_len + (c + 1) * LANES] = y
        carry = y[:, LANES - 1:LANES]


def _logf_cumsum(fl_t, b_forget, past_t):
    b, _, t_pad = fl_t.shape
    p_len = 0 if past_t is None else past_t.shape[2]
    in_specs = [pl.BlockSpec((1, B_HEADS, t_pad), lambda i: (i, 0, 0)),
                pl.BlockSpec((B_HEADS, 1), lambda i: (0, 0))]
    args = [fl_t, b_forget]
    if p_len:
        in_specs.append(pl.BlockSpec((1, B_HEADS, p_len), lambda i: (i, 0, 0)))
        args.append(past_t)
    return pl.pallas_call(
        functools.partial(_logf_cumsum_kernel, p_len=p_len, t_pad=t_pad),
        grid=(b,),
        in_specs=in_specs,
        out_specs=[pl.BlockSpec((1, B_HEADS, t_pad), lambda i: (i, 0, 0)),
                   pl.BlockSpec((1, B_HEADS, p_len + t_pad), lambda i: (i, 0, 0))],
        out_shape=[jax.ShapeDtypeStruct((b, B_HEADS, t_pad), F32),
                   jax.ShapeDtypeStruct((b, B_HEADS, p_len + t_pad), F32)],
        compiler_params=_cparams("arbitrary"),
    )(*args)


BISECT_ROWS = 128


def _indexer_kernel(qi_ref, small_ref, klo_ref, khi_ref, mask_ref, qh_ref, wb_ref, keys_ref, *,
                    tq, tkc, n_chunks, l_real, p_len, nq_per_batch, n_sel):
    i = pl.program_id(0)
    row0 = (i % nq_per_batch) * tq
    n_pairs = IDX_HEADS // 2
    sub = tkc // LANES if tkc % LANES == 0 else 0
    for b in range(n_pairs):
        qh_ref[b * tq:(b + 1) * tq, :] = qi_ref[:, b * LANES:(b + 1) * LANES]
    w = small_ref[:, IDX_DIM:IDX_DIM + IDX_HEADS] * (IDX_HEADS ** -0.5)
    w = w * (IDX_DIM ** -0.5)
    wcols = LANES if sub else tkc
    for h in range(IDX_HEADS):
        wb_ref[h] = jnp.broadcast_to(w[:, h:h + 1], (tq, wcols))
    qpos = p_len + row0 + lax.broadcasted_iota(jnp.int32, (tq, 1), 0)
    k_limit = jnp.minimum((qpos // CHUNK + 1) * CHUNK, l_real)
    last_allowed = (p_len + row0 + tq - 1) // CHUNK * CHUNK + CHUNK - 1
    n_allowed = jnp.minimum(last_allowed // tkc + 1, n_chunks)

    def allowed_of(c0):
        kpos = c0 + lax.broadcasted_iota(jnp.int32, (tq, tkc), 1)
        return kpos < k_limit

    def weighted(s, h):
        r = jnp.maximum(s, 0.0)
        if not sub:
            return r * wb_ref[h]
        wv = wb_ref[h]
        return jnp.concatenate([r[:, u * LANES:(u + 1) * LANES] * wv for u in range(sub)], axis=1)

    for c in range(n_chunks):
        c0 = c * tkc

        @pl.when(c < n_allowed)
        def _():
            q_all = qh_ref[...]
            s_lo = jnp.dot(q_all, klo_ref[0, :, c0:c0 + tkc], preferred_element_type=F32)
            s_hi = jnp.dot(q_all, khi_ref[0, :, c0:c0 + tkc], preferred_element_type=F32)
            score = jnp.zeros((tq, tkc), F32)
            for b in range(n_pairs):
                score = score + weighted(s_lo[b * tq:(b + 1) * tq, :], 2 * b)
                score = score + weighted(s_hi[b * tq:(b + 1) * tq, :], 2 * b + 1)
            score = score + 0.0
            score = jnp.where(allowed_of(c0), score, NEG_INF)
            bits = lax.bitcast_convert_type(score, jnp.int32)
            keys_ref[c] = jnp.where(bits < 0, bits ^ 0x7FFFFFFF, bits)

    def count(pred_of_chunk, r0=0, nr=tq):
        def body(c, acc):
            ind = jnp.where(pred_of_chunk(keys_ref[c, r0:r0 + nr, :], c), 1.0, 0.0)
            if not sub:
                return acc + ind
            for u in range(sub):
                acc = acc + ind[:, u * LANES:(u + 1) * LANES]
            return acc

        acc = lax.fori_loop(0, n_allowed, body, jnp.zeros((nr, wcols), F32))
        return jnp.sum(acc, axis=1, keepdims=True)

    k_f = float(n_sel)
    slab = min(tq, BISECT_ROWS)
    t_parts = []
    for r0 in range(0, tq, slab):
        cnt = functools.partial(count, r0=r0, nr=slab)
        t0 = jnp.where(cnt(lambda kc, c: kc >= 0) >= k_f, 0, INT_MIN).astype(jnp.int32)

        def bisect(it, t, cnt=cnt):
            cand = t + jnp.left_shift(jnp.int32(1), 30 - it)
            return jnp.where(cnt(lambda kc, c: kc >= cand) >= k_f, cand, t)

        t_parts.append(lax.fori_loop(0, 31, bisect, t0))
    t = t_parts[0] if len(t_parts) == 1 else jnp.concatenate(t_parts, axis=0)
    cge = count(lambda kc, c: kc >= t)
    excess = jnp.max(jnp.where(t > KEY_NEG_INF, cge, 0.0)) > k_f

    def emit(sel_of_chunk):
        for c in range(n_chunks):
            @pl.when(c < n_allowed)
            def _():
                sel = jnp.logical_and(sel_of_chunk(keys_ref[c], c), allowed_of(c * tkc))
                mask_ref[c] = jnp.where(sel, 0.0, NEG_INF).astype(mask_ref.dtype)

            @pl.when(c >= n_allowed)
            def _():
                mask_ref[c] = jnp.full((tq, tkc), NEG_INF, mask_ref.dtype)

    @pl.when(jnp.logical_not(excess))
    def _():
        emit(lambda kc, c: kc >= t)

    @pl.when(excess)
    def _():
        need = k_f - count(lambda kc, c: kc > t)
        lane_idx = lax.broadcasted_iota(jnp.int32, (tq, tkc), 1)
        m = jnp.zeros((tq, 1), jnp.int32)
        for bit in reversed(range(max(1, (n_chunks * tkc - 1).bit_length()))):
            cand = m + (1 << bit)
            c_lt = count(lambda kc, c: jnp.logical_and(kc == t, c * tkc + lane_idx < cand))
            m = jnp.where(c_lt < need, cand, m)
        emit(lambda kc, c: jnp.logical_or(
            kc > t, jnp.logical_and(kc == t, c * tkc + lane_idx <= m)))


def _indexer_mask(qi, small, k_lo, k_hi, *, rows_per_batch, tq, tkc, l_real, p_len):
    n = qi.shape[0]
    l_pad = k_lo.shape[2]
    n_chunks = l_pad // tkc
    nqb = rows_per_batch // tq
    n_sel = min(TOPK_MAX, l_real // 4)
    wcols = LANES if tkc % LANES == 0 else tkc
    kern = functools.partial(_indexer_kernel, tq=tq, tkc=tkc, n_chunks=n_chunks, l_real=l_real,
                             p_len=p_len, nq_per_batch=nqb, n_sel=n_sel)
    kspec = pl.BlockSpec((1, LANES, l_pad), lambda i: (i // nqb, 0, 0))
    return pl.pallas_call(
        kern,
        grid=(n // tq,),
        in_specs=[pl.BlockSpec((tq, IDX_WIDTH), lambda i: (i, 0)),
                  pl.BlockSpec((tq, SMALL_WIDTH), lambda i: (i, 0)),
                  kspec, kspec],
        out_specs=pl.BlockSpec((n_chunks, tq, tkc), lambda i: (0, i, 0)),
        out_shape=jax.ShapeDtypeStruct((n_chunks, n, tkc), BF16),
        scratch_shapes=[pltpu.VMEM((IDX_HEADS // 2 * tq, LANES), BF16),
                        pltpu.VMEM((IDX_HEADS, tq, wcols), F32),
                        pltpu.VMEM((n_chunks, tq, tkc), jnp.int32)],
        compiler_params=_cparams("arbitrary"),
    )(qi, small, k_lo, k_hi)


def _softmax_step(s, carry, v_tile):
    m, l, acc = carry
    m_new = jnp.maximum(m, jnp.max(s, axis=1, keepdims=True))
    m_safe = jnp.where(m_new == NEG_INF, 0.0, m_new)
    alpha = jnp.exp2(m - m_safe)
    p = jnp.exp2(s - m_safe)
    l = alpha * l + jnp.sum(p, axis=1, keepdims=True)
    acc = alpha * acc + jnp.dot(p.astype(BF16), v_tile, preferred_element_type=F32)
    return m_new, l, acc


def _softmax_init(rows):
    return (jnp.full((rows, 1), NEG_INF, F32), jnp.zeros((rows, 1), F32),
            jnp.zeros((rows, HEAD_DIM), F32))


def _qk(q, k_tile):
    return lax.dot_general(q, k_tile, (((1,), (1,)), ((), ())), preferred_element_type=F32)


def _fox_prompt_kernel(q_ref, k_ref, v_ref, fk_ref, o_ref, *, t_tile, n_heads, nt):
    i = pl.program_id(1)
    lanes = lambda g: slice(g * HEAD_DIM, (g + 1) * HEAD_DIM)
    qs = [q_ref[:, lanes(g)] for g in range(n_heads)]

    def tile(j, carries, diagonal):
        k0 = pl.multiple_of(j * t_tile, t_tile)
        out = []
        for g in range(n_heads):
            s = _qk(qs[g], k_ref[pl.ds(k0, t_tile), lanes(g)]) - fk_ref[g * nt + j] * LOG2E
            if diagonal:
                row = lax.broadcasted_iota(jnp.int32, (t_tile, t_tile), 0)
                col = lax.broadcasted_iota(jnp.int32, (t_tile, t_tile), 1)
                s = jnp.where(col <= row, s, NEG_INF)
            out.append(_softmax_step(s, carries[g], v_ref[pl.ds(k0, t_tile), lanes(g)]))
        return tuple(out)

    init = tuple(_softmax_init(t_tile) for _ in range(n_heads))
    carries = lax.fori_loop(0, i, lambda j, c: tile(j, c, False), init)
    carries = tile(i, carries, True)
    for g in range(n_heads):
        _, l, acc = carries[g]
        o_ref[:, lanes(g)] = (acc / l).astype(o_ref.dtype)


def _fox_prompt(q, k, v, fk_tiles, *, t_tile, heads_per_step):
    t, width = q.shape
    heads = width // HEAD_DIM
    nt = t // t_tile
    gw = heads_per_step * HEAD_DIM
    resident = dict(pipeline_mode=pl.Buffered(1))
    return pl.pallas_call(
        functools.partial(_fox_prompt_kernel, t_tile=t_tile, n_heads=heads_per_step, nt=nt),
        grid=(heads // heads_per_step, nt),
        in_specs=[pl.BlockSpec((t_tile, gw), lambda h, i: (i, h)),
                  pl.BlockSpec((t, gw), lambda h, i: (0, h), **resident),
                  pl.BlockSpec((t, gw), lambda h, i: (0, h), **resident),
                  pl.BlockSpec((heads_per_step * nt, 1, t_tile), lambda h, i: (h, 0, 0))],
        out_specs=pl.BlockSpec((t_tile, gw), lambda h, i: (i, h)),
        out_shape=jax.ShapeDtypeStruct((t, width), BF16),
        compiler_params=_cparams("arbitrary", "arbitrary"),
    )(q, k, v, fk_tiles)


def _dsa_prompt_kernel(q_ref, k_ref, v_ref, mask_ref, bias_ref, o_ref, *, t_tile, grp, mask_w):
    i = pl.program_id(1)
    q = jnp.concatenate([q_ref[:, j * HEAD_DIM:(j + 1) * HEAD_DIM] for j in range(grp)], axis=0)
    n_sub = t_tile // mask_w

    def tile(j, carry, bias_cls):
        k0 = pl.multiple_of(j * t_tile, t_tile)
        s = _qk(q, k_ref[pl.ds(k0, t_tile), :]).reshape(grp, t_tile, t_tile)
        madd = jnp.concatenate([mask_ref[j * n_sub + u] for u in range(n_sub)], axis=1)
        s = s + madd.astype(F32)[None]
        if bias_cls is not None:
            s = s + bias_ref[bias_cls]
        return _softmax_step(s.reshape(grp * t_tile, t_tile), carry, v_ref[pl.ds(k0, t_tile), :])

    carry = lax.fori_loop(0, jnp.maximum(i - 1, 0), lambda j, c: tile(j, c, None),
                          _softmax_init(grp * t_tile))
    carry = lax.fori_loop(jnp.maximum(i - 1, 0), i, lambda j, c: tile(j, c, 1), carry)
    _, l, acc = tile(i, carry, 0)
    out = acc / l
    for j in range(grp):
        o_ref[:, j * HEAD_DIM:(j + 1) * HEAD_DIM] = out[j * t_tile:(j + 1) * t_tile].astype(o_ref.dtype)


def _dsa_prompt(q, k, v, mask, bias, *, t_tile):
    t, width = q.shape
    kv_heads = k.shape[1] // HEAD_DIM
    grp = width // HEAD_DIM // kv_heads
    nt = t // t_tile
    n_mask, _, mask_w = mask.shape
    resident = dict(pipeline_mode=pl.Buffered(1))
    return pl.pallas_call(
        functools.partial(_dsa_prompt_kernel, t_tile=t_tile, grp=grp, mask_w=mask_w),
        grid=(kv_heads, nt),
        in_specs=[pl.BlockSpec((t_tile, grp * HEAD_DIM), lambda g, i: (i, g)),
                  pl.BlockSpec((t, HEAD_DIM), lambda g, i: (0, g), **resident),
                  pl.BlockSpec((t, HEAD_DIM), lambda g, i: (0, g), **resident),
                  pl.BlockSpec((n_mask, t_tile, mask_w), lambda g, i: (0, i, 0)),
                  pl.BlockSpec((2, grp, t_tile, t_tile), lambda g, i: (0, g, 0, 0), **resident)],
        out_specs=pl.BlockSpec((t_tile, grp * HEAD_DIM), lambda g, i: (i, g)),
        out_shape=jax.ShapeDtypeStruct((t, width), BF16),
        compiler_params=_cparams("arbitrary", "arbitrary"),
    )(q, k, v, mask, bias)


def _attend(q_ref, k_of, v_of, bias_of_head, m_ref, l_ref, acc_ref, *, tq, kv_heads, grp):
    for g in range(kv_heads):
        kg = k_of(g).astype(BF16)
        vg = v_of(g).astype(BF16)
        qs = [q_ref[:, (g * grp + j) * HEAD_DIM:(g * grp + j + 1) * HEAD_DIM] for j in range(grp)]
        qg = qs[0] if grp == 1 else jnp.concatenate(qs, axis=0)
        s_all = _qk(qg, kg)
        for j in range(grp):
            h = g * grp + j
            s = s_all[j * tq:(j + 1) * tq, :] + bias_of_head(h)
            m_ref[h], l_ref[h], acc_ref[h] = _softmax_step(s, (m_ref[h], l_ref[h], acc_ref[h]), vg)


def _flash_kernel(*refs, mode, n_a, tq, tk_a, tk_b, p_len, kv_heads, grp, cache_4d):
    heads = kv_heads * grp
    it = iter(refs)
    q_ref = next(it)
    ka_ref, va_ref, kb_ref, vb_ref = next(it), next(it), next(it), next(it)
    if mode == "dsa":
        maska_ref, biasa_ref, maskb_ref, biasb_ref = next(it), next(it), next(it), next(it)
    else:
        fq_ref, fka_ref, fkb_ref = next(it), next(it), next(it)
    o_ref, m_ref, l_ref, acc_ref = next(it), next(it), next(it), next(it)
    j = pl.program_id(2)

    @pl.when(j == 0)
    def _():
        m_ref[...] = jnp.full(m_ref.shape, NEG_INF, F32)
        l_ref[...] = jnp.zeros(l_ref.shape, F32)
        acc_ref[...] = jnp.zeros(acc_ref.shape, F32)

    common = dict(tq=tq, kv_heads=kv_heads, grp=grp)

    @pl.when(j < n_a)
    def _():
        if mode == "dsa":
            madd = maska_ref[...].astype(F32)
            bias = lambda h: biasa_ref[0, h] + madd
        else:
            bias = lambda h: (fq_ref[0, :, h:h + 1] - fka_ref[0, h:h + 1, :]) * LOG2E
        if cache_4d:
            ka2 = ka_ref.reshape(tk_a * kv_heads, HEAD_DIM)
            va2 = va_ref.reshape(tk_a * kv_heads, HEAD_DIM)
            k_of = lambda g: ka2[pl.ds(g, tk_a, stride=kv_heads), :]
            v_of = lambda g: va2[pl.ds(g, tk_a, stride=kv_heads), :]
        else:
            k_of = lambda g: ka_ref[0, pl.ds(g, tk_a, stride=kv_heads), :]
            v_of = lambda g: va_ref[0, pl.ds(g, tk_a, stride=kv_heads), :]
        _attend(q_ref, k_of, v_of, bias, m_ref, l_ref, acc_ref, **common)

    @pl.when(j == n_a)
    def _():
        if mode == "dsa":
            madd = maskb_ref[...].astype(F32)
            bias = lambda h: biasb_ref[0, h] + madd
        else:
            row = lax.broadcasted_iota(jnp.int32, (tq, tk_b), 0)
            col = lax.broadcasted_iota(jnp.int32, (tq, tk_b), 1)
            causal = jnp.where(col <= row, 0.0, NEG_INF)
            bias = lambda h: (fq_ref[0, :, h:h + 1] - fkb_ref[0, h:h + 1, :tk_b]) * LOG2E + causal
        k_of = lambda g: kb_ref[:, g * HEAD_DIM:(g + 1) * HEAD_DIM]
        v_of = lambda g: vb_ref[:, g * HEAD_DIM:(g + 1) * HEAD_DIM]
        _attend(q_ref, k_of, v_of, bias, m_ref, l_ref, acc_ref, **common)
        for h in range(heads):
            o_ref[:, h * HEAD_DIM:(h + 1) * HEAD_DIM] = (acc_ref[h] / l_ref[h]).astype(o_ref.dtype)


def _flash_sample(mode, q, k_new, v_new, k_past, v_past, *, tk_a, mask_a=None, bias_a=None,
                  mask_b=None, bias_b=None, fq=None, fk_t=None):
    n, width = q.shape
    kv_w = k_new.shape[1]
    b = k_past.shape[0]
    tq = n // b
    heads = width // HEAD_DIM
    kv_heads = kv_w // HEAD_DIM
    p_len = k_past.shape[1] // kv_heads
    grp = heads // kv_heads
    n_a = p_len // tk_a
    a_blk = lambda j: jnp.minimum(j, n_a - 1)
    split = kv_heads // SUBLANES if mode == "fox" and kv_heads % SUBLANES == 0 else 1
    cache_4d = split > 1
    hs = heads // split
    kvs = kv_heads // split

    in_specs = [pl.BlockSpec((tq, hs * HEAD_DIM), lambda g, s, j: (g, s))]
    if cache_4d:
        k_past = k_past.reshape(b, p_len, kv_heads, HEAD_DIM)
        v_past = v_past.reshape(b, p_len, kv_heads, HEAD_DIM)
        spec = pl.BlockSpec((1, tk_a, kvs, HEAD_DIM), lambda g, s, j: (g, a_blk(j), s, 0))
    else:
        spec = pl.BlockSpec((1, tk_a * kv_heads, HEAD_DIM), lambda g, s, j: (g, a_blk(j), 0))
    in_specs += [spec, spec]
    spec = pl.BlockSpec((tq, kvs * HEAD_DIM), lambda g, s, j: (g, s))
    in_specs += [spec, spec]
    args = [q, k_past, v_past, k_new, v_new]
    if mode == "dsa":
        in_specs += [pl.BlockSpec((tq, tk_a), lambda g, s, j: (g, a_blk(j))),
                     pl.BlockSpec((1, heads, tq, tk_a), lambda g, s, j: (a_blk(j), 0, 0, 0)),
                     pl.BlockSpec((tq, tq), lambda g, s, j: (g, 0)),
                     pl.BlockSpec((1, heads, tq, tq), lambda g, s, j: (0, 0, 0, 0))]
        args += [mask_a, bias_a, mask_b, bias_b]
    else:
        fq_split = jnp.transpose(fq.reshape(n, split, hs), (1, 0, 2))
        in_specs += [pl.BlockSpec((1, tq, hs), lambda g, s, j: (s, g, 0)),
                     pl.BlockSpec((1, hs, tk_a), lambda g, s, j: (g, s, a_blk(j))),
                     pl.BlockSpec((1, hs, LANES), lambda g, s, j: (g, s, p_len // LANES))]
        args += [fq_split, fk_t, fk_t]

    kern = functools.partial(_flash_kernel, mode=mode, n_a=n_a, tq=tq, tk_a=tk_a, tk_b=tq,
                             p_len=p_len, kv_heads=kvs, grp=grp, cache_4d=cache_4d)
    return pl.pallas_call(
        kern,
        grid=(b, split, n_a + 1),
        in_specs=in_specs,
        out_specs=pl.BlockSpec((tq, hs * HEAD_DIM), lambda g, s, j: (g, s)),
        out_shape=jax.ShapeDtypeStruct((n, width), BF16),
        scratch_shapes=[pltpu.VMEM((hs, tq, 1), F32),
                        pltpu.VMEM((hs, tq, 1), F32),
                        pltpu.VMEM((hs, tq, HEAD_DIM), F32)],
        compiler_params=_cparams("arbitrary", "arbitrary", "arbitrary"),
    )(*args)


def _router_kernel(lt_ref, eidx_ref, gate_ref):
    row = lambda r: lt_ref[r:r + 1, :]
    g = [row(r) for r in range(N_GROUPS)]
    gmax = functools.reduce(jnp.maximum, g)
    gsel = jnp.full(gmax.shape, N_GROUPS - 1, jnp.int32)
    for r in reversed(range(N_GROUPS - 1)):
        gsel = jnp.where(g[r] == gmax, r, gsel)
    denom = functools.reduce(lambda a, b: a + b, [jnp.exp(x - gmax) for x in g])
    g_prob = 1.0 / denom
    e_in = []
    for kk in range(EXPERTS_PER_GROUP):
        v = row(N_GROUPS + (N_GROUPS - 1) * EXPERTS_PER_GROUP + kk)
        for r in reversed(range(N_GROUPS - 1)):
            v = jnp.where(gsel == r, row(N_GROUPS + r * EXPERTS_PER_GROUP + kk), v)
        e_in.append(v)
    v1 = functools.reduce(jnp.maximum, e_in)
    i1 = jnp.full(v1.shape, EXPERTS_PER_GROUP - 1, jnp.int32)
    for kk in reversed(range(EXPERTS_PER_GROUP - 1)):
        i1 = jnp.where(e_in[kk] == v1, kk, i1)
    rest = [jnp.where(i1 == kk, NEG_INF, e_in[kk]) for kk in range(EXPERTS_PER_GROUP)]
    v2 = functools.reduce(jnp.maximum, rest)
    i2 = jnp.full(v2.shape, EXPERTS_PER_GROUP - 1, jnp.int32)
    for kk in reversed(range(EXPERTS_PER_GROUP - 1)):
        i2 = jnp.where(jnp.logical_and(rest[kk] == v2, i1 != kk), kk, i2)
    e2 = jnp.exp(v2 - v1)
    inv = 1.0 / (1.0 + e2)
    eidx_ref[0:1, :] = gsel * EXPERTS_PER_GROUP + i1
    eidx_ref[1:2, :] = gsel * EXPERTS_PER_GROUP + i2
    gate_ref[0:1, :] = inv * g_prob
    gate_ref[1:2, :] = (e2 * inv) * g_prob


def _router(logits_t):
    n = logits_t.shape[1]
    full = lambda r: pl.BlockSpec((r, n), lambda: (0, 0))
    return pl.pallas_call(
        _router_kernel,
        in_specs=[full(logits_t.shape[0])],
        out_specs=[full(TOP_K_IN_GROUP), full(TOP_K_IN_GROUP)],
        out_shape=[jax.ShapeDtypeStruct((TOP_K_IN_GROUP, n), jnp.int32),
                   jax.ShapeDtypeStruct((TOP_K_IN_GROUP, n), F32)],
        compiler_params=pltpu.CompilerParams(vmem_limit_bytes=VMEM_LIMIT),
    )(logits_t)


EXPERT_CHUNK = 256
EXPERT_CHUNKS_PER_BLOCK = 4
EXPERT_ROWS = EXPERT_CHUNK * EXPERT_CHUNKS_PER_BLOCK
EXPERT_K_TILE = 512
EXPERT_N_TILE = 1024


def _experts_kernel(sb_e_ref, sb_row0_ref, sb_nch_ref, sb_blk_ref, x_hbm, w1_ref, w3_ref, w2_ref,
                    o_ref, x_ref, a_ref, b_ref, act_ref, sem, *, nk):
    del sb_e_ref, sb_blk_ref
    s = pl.program_id(0)
    t = pl.program_id(1)
    n_ch = sb_nch_ref[s]

    @pl.when(jnp.logical_and(t == 0, n_ch > 0))
    def _():
        row0 = pl.multiple_of(sb_row0_ref[s], EXPERT_CHUNK)
        copies = [pltpu.make_async_copy(x_hbm.at[k, pl.ds(row0, EXPERT_ROWS)], x_ref.at[k], sem)
                  for k in range(nk)]
        for cp in copies:
            cp.start()
        for cp in copies:
            cp.wait()

    for nc in range(1, EXPERT_CHUNKS_PER_BLOCK + 1):
        rows = nc * EXPERT_CHUNK

        @pl.when(jnp.logical_and(t < nk, n_ch == nc))
        def _():
            xc = x_ref[t, 0:rows, :]
            a = jnp.dot(xc, w1_ref[0].astype(BF16), preferred_element_type=F32)
            b = jnp.dot(xc, w3_ref[0].astype(BF16), preferred_element_type=F32)

            @pl.when(t == 0)
            def _():
                a_ref[0:rows, :] = a
                b_ref[0:rows, :] = b

            @pl.when(t > 0)
            def _():
                a_ref[0:rows, :] += a
                b_ref[0:rows, :] += b

            @pl.when(t == nk - 1)
            def _():
                g = a_ref[0:rows, :]
                act_ref[0:rows, :] = (g * jax.nn.sigmoid(g) * b_ref[0:rows, :]).astype(BF16)

        @pl.when(jnp.logical_and(t >= nk, n_ch == nc))
        def _():
            o_ref[0:rows, :] = jnp.dot(act_ref[0:rows, :], w2_ref[0].astype(BF16),
                                       preferred_element_type=F32)


def _experts(xs, sb_expert, sb_row0, sb_nch, sb_blk, w1, w3, w2, n_sb):
    nk = xs.shape[0]
    d = nk * EXPERT_K_TILE
    d_e = w1.shape[2]
    nn = d // EXPERT_N_TILE
    r = EXPERT_ROWS
    grid_spec = pltpu.PrefetchScalarGridSpec(
        num_scalar_prefetch=4,
        grid=(n_sb, nk + nn),
        in_specs=[
            pl.BlockSpec(memory_space=pl.ANY),
            pl.BlockSpec((1, EXPERT_K_TILE, d_e),
                         lambda s, t, e, r0, nch, blk: (e[s], jnp.minimum(t, nk - 1), 0)),
            pl.BlockSpec((1, EXPERT_K_TILE, d_e),
                         lambda s, t, e, r0, nch, blk: (e[s], jnp.minimum(t, nk - 1), 0)),
            pl.BlockSpec((1, d_e, EXPERT_N_TILE),
                         lambda s, t, e, r0, nch, blk: (e[s], 0, jnp.maximum(t - nk, 0))),
        ],
        out_specs=pl.BlockSpec(
            (r, EXPERT_N_TILE),
            lambda s, t, e, r0, nch, blk: (blk[s], jnp.where(nch[s] > 0, jnp.maximum(t - nk, 0), 0))),
        scratch_shapes=[pltpu.VMEM((nk, r, EXPERT_K_TILE), BF16),
                        pltpu.VMEM((r, d_e), F32),
                        pltpu.VMEM((r, d_e), F32),
                        pltpu.VMEM((r, d_e), BF16),
                        pltpu.SemaphoreType.DMA(())],
    )
    return pl.pallas_call(
        functools.partial(_experts_kernel, nk=nk),
        grid_spec=grid_spec,
        out_shape=jax.ShapeDtypeStruct(((n_sb + 1) * r, d), F32),
        compiler_params=_cparams("arbitrary", "arbitrary"),
    )(sb_expert, sb_row0, sb_nch, sb_blk, xs, w1, w3, w2)


def _dispatch(expert, n_sb, n_x_rows):
    n = expert.shape[0]
    a = n * TOP_K_IN_GROUP
    ch, r = EXPERT_CHUNK, EXPERT_ROWS
    flat_e = expert.reshape(a)
    flat_tok = jnp.arange(a, dtype=jnp.int32) // TOP_K_IN_GROUP
    order = jnp.argsort(flat_e)
    se = flat_e[order]
    counts = jnp.bincount(flat_e, length=N_EXPERTS).astype(jnp.int32)
    seg_start = jnp.cumsum(counts) - counts
    rank = jnp.arange(a, dtype=jnp.int32) - seg_start[se]
    nch_e = (counts + ch - 1) // ch
    xch_end = jnp.cumsum(nch_e)
    x_start = (xch_end - nch_e) * ch
    nsb_e = (counts + r - 1) // r
    sb_end = jnp.cumsum(nsb_e)
    sb_start = sb_end - nsb_e
    x_row_tok = jnp.zeros((n_x_rows,), jnp.int32).at[x_start[se] + rank].set(flat_tok[order])
    n_live_x_blocks = (xch_end[-1] * ch + GATHER_ROWS - 1) // GATHER_ROWS
    y_row = jnp.zeros((a,), jnp.int32).at[order].set(sb_start[se] * r + rank)
    y_row = y_row.reshape(n, TOP_K_IN_GROUP)
    total = sb_end[-1]
    s_idx = jnp.arange(n_sb, dtype=jnp.int32)
    e_of_sb = jnp.minimum(jnp.searchsorted(sb_end, s_idx, side="right"), N_EXPERTS - 1).astype(jnp.int32)
    k_in_e = s_idx - sb_start[e_of_sb]
    rows_of_sb = jnp.clip(counts[e_of_sb] - k_in_e * r, 0, r)
    live = s_idx < total
    last_e = e_of_sb[jnp.maximum(total - 1, 0)]
    sb_expert = jnp.where(live, e_of_sb, last_e).astype(jnp.int32)
    sb_nch = jnp.where(live, (rows_of_sb + ch - 1) // ch, 0).astype(jnp.int32)
    sb_row0 = jnp.where(live, x_start[e_of_sb] + k_in_e * r, 0).astype(jnp.int32)
    sb_blk = jnp.where(live, s_idx, n_sb).astype(jnp.int32)
    return (x_row_tok, n_live_x_blocks.astype(jnp.int32).reshape(1), y_row, sb_expert, sb_row0,
            sb_nch, sb_blk)


def _t5_bucket(rel):
    half = N_BUCKETS // 2
    max_exact = half // 2
    bucket = jnp.where(rel > 0, half, 0).astype(jnp.int32)
    n = jnp.abs(rel)
    large = max_exact + (jnp.log(jnp.maximum(n, 1).astype(jnp.float32) / max_exact)
                         / math.log(MAX_DISTANCE / max_exact) * (half - max_exact)).astype(jnp.int32)
    large = jnp.minimum(large, half - 1)
    return bucket + jnp.where(n < max_exact, n, large).astype(jnp.int32)


def _bias_tiles(rel_bias, rel0_list, tq, tk):
    span = tq + tk
    tiles = []
    for rel0 in rel0_list:
        u = jnp.arange(span, dtype=jnp.int32)
        rel = rel0 + jnp.where(u < tk, u, u - span)
        table = jnp.transpose(rel_bias[_t5_bucket(rel)].astype(F32))
        heads = table.shape[0]
        skew = jnp.tile(table, (1, tq))[:, :tq * (span - 1)].reshape(heads, tq, span - 1)
        tiles.append(skew[:, :, :tk])
    return jnp.stack(tiles, axis=0)


def _pick(n, candidates):
    for c in candidates:
        if n % c == 0:
            return c
    return n


INDEXER_TILE = 256
ATTN_TILE = 512
FOX_TILE = 1024
FOX_HEADS_PER_STEP = 2


def _mixer_half(x, mods, past, w_in_packed, b_forget, w_out, rel_bias, g_mix, g_ffn):
    b, t, d = x.shape
    n = b * t
    sh1, sc1, gt1, sh2, sc2, _ = mods
    xf = x.reshape(n, d)
    tt = _pick(t, (256, 128, 64, 32, 16))
    h = _norm_mod(xf, g_mix, sc1, sh1, rows_per_batch=t, tt=tt, out_dtype=BF16)

    tm = _pick(n, (1024, 512, 256, 128))

    def proj(col_off, width, tn, **kw):
        return _matmul([h], w_in_packed, col_off=col_off, n=width, tm=tm, tn=tn, **kw)

    off = 0
    qa = proj(off, A_WIDTH, 1024, out_dtypes=(BF16,), out_scale=Q_SCALE); off += A_WIDTH
    ka, ka16 = proj(off, A_KV_WIDTH, 512, out_dtypes=(F32, BF16)); off += A_KV_WIDTH
    va, va16 = proj(off, A_KV_WIDTH, 512, out_dtypes=(F32, BF16)); off += A_KV_WIDTH
    qi = proj(off, IDX_WIDTH, 1024, out_dtypes=(BF16,)); off += IDX_WIDTH
    qb = proj(off, B_WIDTH, 1024, out_dtypes=(BF16,), out_scale=Q_SCALE); off += B_WIDTH
    kb, kb16 = proj(off, B_WIDTH, 1024, out_dtypes=(F32, BF16)); off += B_WIDTH
    vb, vb16 = proj(off, B_WIDTH, 1024, out_dtypes=(F32, BF16)); off += B_WIDTH
    small = proj(off, SMALL_WIDTH, SMALL_WIDTH)

    ki = small[:, :IDX_DIM]
    fl = small[:, IDX_DIM + IDX_HEADS:IDX_DIM + IDX_HEADS + B_HEADS]

    p_len = 0 if past is None else past[0].shape[1]
    t_pad = -(-t // LANES) * LANES
    l_real = p_len + t
    l_pad = p_len + t_pad

    fl_t = jnp.transpose(fl.reshape(b, t, B_HEADS), (0, 2, 1))
    fl_t = jnp.pad(fl_t, ((0, 0), (0, 0), (0, t_pad - t)))
    past_logf_t = None if past is None else jnp.transpose(past[5], (0, 2, 1))
    logf_t, fcum_t = _logf_cumsum(fl_t, b_forget.reshape(B_HEADS, 1), past_logf_t)
    logf = jnp.transpose(logf_t[:, :, :t], (0, 2, 1))

    ki_b = ki.reshape(b, t, IDX_DIM)
    if past is not None:
        ki_b = jnp.concatenate([past[2], ki_b], axis=1)
    ki_t = jnp.transpose(jnp.pad(ki_b, ((0, 0), (0, l_pad - l_real), (0, 0))), (0, 2, 1)).astype(BF16)
    zeros = jnp.zeros_like(ki_t)
    k_lo = jnp.concatenate([ki_t, zeros], axis=1)
    k_hi = jnp.concatenate([zeros, ki_t], axis=1)

    if past is None:
        tile = ATTN_TILE
        assert b == 1 and t % tile == 0 and tile >= MAX_DISTANCE and tile % INDEXER_TILE == 0
        mask = _indexer_mask(qi, small, k_lo, k_hi, rows_per_batch=t, tq=INDEXER_TILE,
                             tkc=INDEXER_TILE, l_real=l_real, p_len=0)
        bias = _bias_tiles(rel_bias, [0, -tile], tile, tile)
        far = rel_bias[_t5_bucket(jnp.int32(-MAX_DISTANCE))].astype(F32)
        bias = (bias - far[None, :, None, None]) * LOG2E
        out_a = _dsa_prompt(qa, ka16, va16, mask, bias, t_tile=tile)
        assert t % FOX_TILE == 0
        fk_tiles = fcum_t.reshape(B_HEADS * (t // FOX_TILE), 1, FOX_TILE)
        out_b = _fox_prompt(qb, kb16, vb16, fk_tiles, t_tile=FOX_TILE,
                            heads_per_step=FOX_HEADS_PER_STEP)
    else:
        tk_a = _pick(p_len, (1024, 512, 256, 128))
        mask = _indexer_mask(qi, small, k_lo, k_hi, rows_per_batch=t, tq=t, tkc=l_pad,
                             l_real=l_real, p_len=p_len).reshape(n, l_pad)
        n_a = p_len // tk_a
        bias_a = _bias_tiles(rel_bias, [c * tk_a - p_len for c in range(n_a)], t, tk_a) * LOG2E
        bias_b = _bias_tiles(rel_bias, [0], t, t) * LOG2E
        cache = lambda a: a.reshape(b, -1, HEAD_DIM)
        out_a = _flash_sample("dsa", qa, ka16, va16, cache(past[0]), cache(past[1]), tk_a=tk_a,
                              mask_a=mask[:, :p_len], bias_a=bias_a,
                              mask_b=mask[:, p_len:p_len + t], bias_b=bias_b)
        fq = jnp.transpose(fcum_t[:, :, p_len:p_len + t], (0, 2, 1)).reshape(n, B_HEADS)
        out_b = _flash_sample("fox", qb, kb16, vb16, cache(past[3]), cache(past[4]), tk_a=tk_a,
                              fq=fq, fk_t=fcum_t)

    gate1 = gt1.reshape(1, d) if b == 1 else jnp.repeat(gt1.reshape(b, d), t, axis=0)
    x1 = _matmul([out_a, out_b], w_out, tm=tm, tn=512, res=xf, gate=gate1)
    h2 = _norm_mod(x1, g_ffn, sc2, sh2, rows_per_batch=t, tt=tt, out_dtype=F32)

    state = (ka.reshape(1, b, t, A_KV_HEADS, HEAD_DIM), va.reshape(1, b, t, A_KV_HEADS, HEAD_DIM),
             ki.reshape(1, b, t, IDX_DIM),
             kb.reshape(1, b, t, B_HEADS, HEAD_DIM), vb.reshape(1, b, t, B_HEADS, HEAD_DIM),
             logf.reshape(1, b, t, B_HEADS))
    return x1, h2, state


def kernel(x_prompt, x_sample, cache_a_k, cache_a_v, cache_idx_k, cache_b_k, cache_b_v, cache_b_logf, c_prompt, c_sample, w_in, b_forget, w_out, rel_bias, w_mod, b_mod, g_mix, g_ffn, w_router_group, b_router_group, w_router_expert, b_router_expert, w_gate, w_up, w_down, g_final):
    assert w_in.shape[0] == 1, "single-layer trunk"
    d = x_prompt.shape[-1]
    bp, tp, _ = x_prompt.shape
    bs, ts, _ = x_sample.shape
    n_p, n_s = bp * tp, bs * ts
    layer0 = lambda a: a.reshape(a.shape[1:])

    n_c = bp + bs
    c_rows = -(-n_c // 8) * 8
    c_all = jnp.pad(jnp.concatenate([c_prompt, c_sample], axis=0), ((0, c_rows - n_c), (0, 0)))
    mod = _matmul([c_all], layer0(w_mod), tm=c_rows, tn=512, silu_a=True, bias=b_mod.reshape(1, -1))

    def mods_of(lo, hi):
        m6 = mod[lo:hi].reshape(hi - lo, 6, 1, d)
        return [m6[:, i] for i in range(6)]

    mods_p, mods_s = mods_of(0, bp), mods_of(bp, n_c)

    w = layer0(w_in)
    o1 = A_WIDTH + 2 * A_KV_WIDTH + IDX_WIDTH
    o2 = o1 + IDX_DIM + IDX_HEADS
    o3 = o2 + 3 * B_WIDTH
    tail_pad = SMALL_WIDTH - (IDX_DIM + IDX_HEADS + B_HEADS)
    w_packed = jnp.concatenate(
        [w[:, :o1].astype(BF16), w[:, o2:o3].astype(BF16), w[:, o1:o2].astype(BF16),
         w[:, o3:].astype(BF16), jnp.zeros((d, tail_pad), BF16)], axis=1)

    g_mix2, g_ffn2, g_fin2 = g_mix.reshape(1, d), g_ffn.reshape(1, d), g_final.reshape(1, d)
    past = tuple(layer0(c) for c in (cache_a_k, cache_a_v, cache_idx_k, cache_b_k, cache_b_v,
                                      cache_b_logf))
    b_f, w_o = layer0(b_forget), layer0(w_out)
    x1_p, h2_p, st_p = _mixer_half(x_prompt, mods_p, None, w_packed, b_f, w_o, rel_bias, g_mix2, g_ffn2)
    x1_s, h2_s, st_s = _mixer_half(x_sample, mods_s, past, w_packed, b_f, w_o, rel_bias, g_mix2, g_ffn2)

    n_tok = n_p + n_s
    w_r = jnp.concatenate([layer0(w_router_group), layer0(w_router_expert)], axis=1)
    n_cls = w_r.shape[1]
    w_r = jnp.pad(w_r, ((0, 0), (0, LANES - n_cls)))
    b_r = jnp.pad(jnp.concatenate([layer0(b_router_group), layer0(b_router_expert)]), (0, LANES - n_cls))
    logits = jnp.concatenate(
        [_matmul([hh], w_r, tm=_pick(hh.shape[0], (512, 256, 128)), tn=LANES,
                 bias=b_r.reshape(1, LANES)) for hh in (h2_p, h2_s)], axis=0)
    eidx_t, gate_t = _router(jnp.transpose(logits))
    expert = jnp.transpose(eidx_t)
    gates = jnp.transpose(gate_t)

    n_assign = n_tok * TOP_K_IN_GROUP
    assert n_assign % GATHER_ROWS == 0
    n_sb = n_assign // EXPERT_ROWS + N_EXPERTS
    n_x_rows = -(-n_assign // EXPERT_CHUNK) * EXPERT_CHUNK + N_EXPERTS * EXPERT_CHUNK + EXPERT_ROWS
    x_row_tok, n_live_x, y_row, sb_expert, sb_row0, sb_nch, sb_blk = _dispatch(expert, n_sb, n_x_rows)
    assert d % EXPERT_K_TILE == 0 and d % EXPERT_N_TILE == 0
    xs = _gather_rows([h2_p, h2_s], x_row_tok, n_live_x, BF16, n_slabs=d // EXPERT_K_TILE)
    yb = _experts(xs, sb_expert, sb_row0, sb_nch, sb_blk, layer0(w_gate), layer0(w_up),
                  layer0(w_down), n_sb)
    def finish(x1, lo, gt2, t):
        tt = _pick(t, (128, 64, 32, 16))
        hi = lo + x1.shape[0]
        return _combine(x1, yb, y_row[lo:hi], gates[lo:hi], gt2, g_fin2, rows_per_batch=t, tt=tt)

    y_p = finish(x1_p, 0, mods_p[5], tp).reshape(bp, tp, d)
    y_s = finish(x1_s, n_p, mods_s[5], ts).reshape(bs, ts, d)
    return (y_p, y_s) + st_p + st_s
```
